```python
import jax, jax.numpy as jnp
from jax import lax
import numpy as np

D_MODEL = 2048
BATCH = 4
SEQ = 2048
DEPTH = 1
DEC_BATCH = 32
DEC_SEQ = 4
PAST_LEN = 8192
PAGE_SIZE = 128

D_A = D_MODEL // 2
HEAD_A = 64
N_HEADS_A = D_A // HEAD_A
LORA_W = 64
LORA_A = 64
LORA_G = 160
Z_A = 3 * D_A + LORA_W + LORA_A + LORA_G
LNX_EPS = 64e-5
D_B = D_MODEL // 2
HEAD_B = 128
N_HEADS_B = D_B // HEAD_B
MOBA_BLOCK = 256
MOBA_TOPK = 3
MOBA_Q_CHUNK = 16
ROPE_THETA = 10000.0
Z_IN = Z_A + 3 * D_B + 2 * D_MODEL
D_FF = 5632
CONV_W = 3
RMS_EPS = 1e-6
NEG_INF = -1e30

kernel_name = 'hybrid_rwkv7_moba_convffn_step'


def _rmsnorm(x, g):
    x32 = x.astype(jnp.float32)
    y = x32 * lax.rsqrt(jnp.mean(x32 * x32, axis=-1, keepdims=True) + RMS_EPS)
    return (y * g.astype(jnp.float32)).astype(x.dtype)


def _rope(x, pos):
    half = x.shape[-1] // 2
    inv = ROPE_THETA ** (-jnp.arange(half, dtype=jnp.float32) / half)
    ang = pos.astype(jnp.float32)[:, None] * inv[None, :]
    cos, sin = jnp.cos(ang)[:, None, :], jnp.sin(ang)[:, None, :]
    x32 = x.astype(jnp.float32)
    x1, x2 = x32[..., :half], x32[..., half:]
    return jnp.concatenate([x1 * cos - x2 * sin, x2 * cos + x1 * sin], axis=-1).astype(x.dtype)


def _wkv_scan(s0, r, w, k, v, kk, a):
    xs = tuple(jnp.moveaxis(t.astype(jnp.float32), 1, 0) for t in (r, w, k, v, kk, a))

    def step(s, inp):
        r_t, w_t, k_t, v_t, kk_t, a_t = inp
        s_kk = jnp.einsum('bhvk,bhk->bhv', s, kk_t)
        s = (s * w_t[:, :, None, :] - s_kk[..., None] * (kk_t * a_t)[:, :, None, :]
             + v_t[..., None] * k_t[:, :, None, :])
        return s, jnp.einsum('bhvk,bhk->bhv', s, r_t)

    s, ys = lax.scan(step, s0.astype(jnp.float32), xs)
    return s, jnp.moveaxis(ys, 0, 1)


def _moba_chunk(q, q_pos, k_blk, v_blk, k_mean):
    bsz, n_q, n_h, d = q.shape
    n_blk = k_blk.shape[1]
    own = q_pos // MOBA_BLOCK
    gate = jnp.einsum('bqhd,bnhd->bhqn', q.astype(jnp.float32), k_mean)
    fully_past = jnp.arange(n_blk)[None, :] < own[:, None]
    gate = jnp.where(fully_past[None, None], gate, NEG_INF)
    _, sel = lax.top_k(gate, min(MOBA_TOPK, n_blk))
    sel_ok = sel < own[None, None, :, None]
    own_idx = jnp.broadcast_to(own[None, None, :, None], (bsz, n_h, n_q, 1)).astype(sel.dtype)
    idx = jnp.concatenate([sel, own_idx], axis=-1)
    ok = jnp.concatenate([sel_ok, jnp.ones_like(own_idx, dtype=bool)], axis=-1)
    b_i = jnp.arange(bsz)[:, None, None, None]
    h_i = jnp.arange(n_h)[None, :, None, None]
    k_g = k_blk[b_i, idx, :, h_i]
    v_g = v_blk[b_i, idx, :, h_i]
    key_pos = idx[..., None] * MOBA_BLOCK + jnp.arange(MOBA_BLOCK)
    allowed = ok[..., None] & (key_pos <= q_pos[None, None, :, None, None])
    s = jnp.einsum('bqhd,bhqnjd->bhqnj', q, k_g).astype(jnp.float32) * (d ** -0.5)
    s = jnp.where(allowed, s, NEG_INF)
    p = jax.nn.softmax(s.reshape(bsz, n_h, n_q, -1), axis=-1).reshape(s.shape)
    return jnp.einsum('bhqnj,bhqnjd->bqhd', p.astype(v_g.dtype), v_g)


def _moba(q, q_pos, k_blk, v_blk):
    bsz, t_len, n_h, d = q.shape
    k_mean = jnp.mean(k_blk.astype(jnp.float32), axis=2)
    qc = MOBA_Q_CHUNK if t_len % MOBA_Q_CHUNK == 0 else t_len
    n_chunks = t_len // qc
    q_chunks = q.reshape(bsz, n_chunks, qc, n_h, d).swapaxes(0, 1)
    pos_chunks = q_pos.reshape(n_chunks, qc)
    out = lax.map(lambda args: _moba_chunk(args[0], args[1], k_blk, v_blk, k_mean),
                  (q_chunks, pos_chunks))
    return out.swapaxes(0, 1).reshape(bsz, t_len, n_h, d)


def _layer(x, c, pos, k_past, v_past, wkv0, shift0, conv0, p):
    bsz, t_len, _ = x.shape
    f32 = jnp.float32
    mod = (jax.nn.silu(c) @ p['w_ada'] + p['b_ada']).reshape(bsz, 6, 1, D_MODEL)
    shift1, scale1, gate1, shift2, scale2, gate2 = (mod[:, i] for i in range(6))

    h = _rmsnorm(x, p['norm1_g']) * (1 + scale1) + shift1
    z = h @ p['w_in']
    z_a = z[..., :Z_A]
    ob = Z_A
    q_b = z[..., ob:ob + D_B]
    k_b = z[..., ob + D_B:ob + 2 * D_B]
    v_b = z[..., ob + 2 * D_B:ob + 3 * D_B]
    gate_a = z[..., ob + 3 * D_B:ob + 3 * D_B + D_MODEL]
    gate_b = z[..., ob + 3 * D_B + D_MODEL:]

    z_prev = jnp.concatenate([shift0[:, None].astype(z_a.dtype), z_a[:, :-1]], axis=1)
    z_mix = z_a + (z_prev - z_a) * p['mu_shift']
    new_shift = z_a[:, -1]
    r = z_mix[..., :D_A]
    k_a = z_mix[..., D_A:2 * D_A]
    v_a = z_mix[..., 2 * D_A:3 * D_A]
    oa = 3 * D_A
    x_w = z_mix[..., oa:oa + LORA_W]
    x_aa = z_mix[..., oa + LORA_W:oa + LORA_W + LORA_A]
    x_g = z_mix[..., oa + LORA_W + LORA_A:]
    w_log = -jax.nn.softplus(-(p['rwkv_w0'] + jnp.tanh(x_w) @ p['rwkv_w2']).astype(f32)) - 0.5
    decay = jnp.exp(-jnp.exp(w_log))
    a = jax.nn.sigmoid(p['rwkv_a0'] + x_aa @ p['rwkv_a2'])
    g = jax.nn.sigmoid(x_g) @ p['rwkv_g2']
    heads = lambda t: t.reshape(bsz, t_len, N_HEADS_A, HEAD_A)
    kk = heads(k_a * p['rwkv_kk']).astype(f32)
    kk = kk * lax.rsqrt(jnp.maximum(jnp.sum(kk * kk, axis=-1, keepdims=True), 1e-24))
    k_a = k_a * (1 + (a - 1) * p['rwkv_ka'])
    r_h, k_h, v_h, a_h = heads(r), heads(k_a), heads(v_a), heads(a)
    wkv, y_h = _wkv_scan(wkv0, r_h, heads(decay), k_h, v_h, kk, a_h)
    mu = jnp.mean(y_h, axis=-1, keepdims=True)
    var = jnp.mean(jnp.square(y_h - mu), axis=-1, keepdims=True)
    y_h = (y_h - mu) * lax.rsqrt(var + LNX_EPS)
    bonus = jnp.sum((r_h * k_h * p['rwkv_rk']).astype(f32), axis=-1, keepdims=True) * v_h.astype(f32)
    y_a = (y_h.reshape(bsz, t_len, D_A) * p['lnx_g'] + p['lnx_b']
           + bonus.reshape(bsz, t_len, D_A))
    o_a = (y_a.astype(x.dtype) * g) @ p['w_proj_a']

    q_h = _rope(q_b.reshape(bsz, t_len, N_HEADS_B, HEAD_B), pos)
    k_hb = _rope(k_b.reshape(bsz, t_len, N_HEADS_B, HEAD_B), pos)
    v_hb = v_b.reshape(bsz, t_len, N_HEADS_B, HEAD_B)
    n_ctx = t_len if k_past is None else k_past.shape[1] + t_len
    n_blk = -(-n_ctx // MOBA_BLOCK)
    pad = jnp.zeros((bsz, n_blk * MOBA_BLOCK - n_ctx, N_HEADS_B, HEAD_B), k_hb.dtype)
    if k_past is None:
        k_all = jnp.concatenate([k_hb, pad], axis=1)
        v_all = jnp.concatenate([v_hb, pad], axis=1)
    else:
        k_all = jnp.concatenate([k_past.astype(k_hb.dtype), k_hb, pad], axis=1)
        v_all = jnp.concatenate([v_past.astype(v_hb.dtype), v_hb, pad], axis=1)
    blocks = lambda t: t.reshape(bsz, n_blk, MOBA_BLOCK, N_HEADS_B, HEAD_B)
    o_b = _moba(q_h, pos, blocks(k_all), blocks(v_all)).reshape(bsz, t_len, D_B) @ p['w_proj_b']

    mix = jax.nn.sigmoid(gate_a) * o_a + jax.nn.sigmoid(gate_b) * o_b
    x = x + gate1 * (mix @ p['w_out'])

    h2 = _rmsnorm(x, p['norm2_g']) * (1 + scale2) + shift2
    u = h2 @ p['w_up']
    u_g, u_v = u[..., :D_FF], u[..., D_FF:]
    ext = jnp.concatenate([conv0.astype(u_g.dtype), u_g], axis=1)
    conv = p['conv_b'] + ext[:, 0:t_len] * p['conv_w'][0]
    for j in range(1, CONV_W):
        conv = conv + ext[:, j:j + t_len] * p['conv_w'][j]
    new_conv = ext[:, -(CONV_W - 1):]
    f = jax.nn.gelu(conv) * u_v
    x = x + gate2 * (f @ p['w_down'])
    return x, k_hb, v_hb, wkv, new_shift, new_conv


def setup_inputs(seed: int = 0) -> dict:
    key = jax.random.key(seed)
    ks = iter(jax.random.split(key, 48))
    f32 = jnp.float32

    def nrm(shape, scale):
        return jax.random.normal(next(ks), shape, f32) * scale

    def unif(shape, lo, hi):
        return jax.random.uniform(next(ks), shape, f32, lo, hi)

    n_pages = PAST_LEN // PAGE_SIZE
    n_used = DEC_BATCH * n_pages
    n_pool = n_used + n_used // 4
    page_table = jax.random.permutation(next(ks), n_pool)[:n_used].reshape(DEC_BATCH, n_pages).astype(jnp.int32)
    return {
        'x_prompt': nrm((BATCH, SEQ, D_MODEL), 1.0),
        'x_sample': nrm((DEC_BATCH, DEC_SEQ, D_MODEL), 1.0),
        'cache_k': nrm((n_pool, PAGE_SIZE, N_HEADS_B, HEAD_B), 1.0),
        'cache_v': nrm((n_pool, PAGE_SIZE, N_HEADS_B, HEAD_B), 1.0),
        'state_wkv': nrm((DEC_BATCH, N_HEADS_A, HEAD_A, HEAD_A), 0.3),
        'state_shift': nrm((DEC_BATCH, Z_A), 1.0),
        'state_conv': nrm((DEC_BATCH, CONV_W - 1, D_FF), 1.0),
        'page_table': page_table,
        'c_prompt': nrm((BATCH, D_MODEL), 1.0),
        'c_sample': nrm((DEC_BATCH, D_MODEL), 1.0),
        'w_ada': nrm((D_MODEL, 6 * D_MODEL), 0.5 * D_MODEL ** -0.5),
        'b_ada': nrm((6 * D_MODEL,), 0.01),
        'norm1_g': 1.0 + nrm((D_MODEL,), 0.05),
        'w_in': nrm((D_MODEL, Z_IN), D_MODEL ** -0.5),
        'mu_shift': unif((Z_A,), 0.0, 1.0),
        'rwkv_w0': unif((D_A,), -4.0, 0.0),
        'rwkv_w2': nrm((LORA_W, D_A), 0.1 * LORA_W ** -0.5),
        'rwkv_a0': nrm((D_A,), 0.1),
        'rwkv_a2': nrm((LORA_A, D_A), 0.5 * LORA_A ** -0.5),
        'rwkv_g2': nrm((LORA_G, D_A), LORA_G ** -0.5),
        'rwkv_kk': 1.0 + nrm((D_A,), 0.1),
        'rwkv_ka': 1.0 + nrm((D_A,), 0.1),
        'rwkv_rk': nrm((N_HEADS_A, HEAD_A), 0.1),
        'lnx_g': 1.0 + nrm((D_A,), 0.05),
        'lnx_b': nrm((D_A,), 0.01),
        'w_proj_a': nrm((D_A, D_MODEL), D_A ** -0.5),
        'w_proj_b': nrm((D_B, D_MODEL), D_B ** -0.5),
        'w_out': nrm((D_MODEL, D_MODEL), D_MODEL ** -0.5),
        'norm2_g': 1.0 + nrm((D_MODEL,), 0.05),
        'w_up': nrm((D_MODEL, 2 * D_FF), D_MODEL ** -0.5),
        'conv_w': nrm((CONV_W, D_FF), CONV_W ** -0.5),
        'conv_b': nrm((D_FF,), 0.01),
        'w_down': nrm((D_FF, D_MODEL), D_FF ** -0.5),
        'normf_g': 1.0 + nrm((D_MODEL,), 0.05),
    }


def reference(x_prompt, x_sample, cache_k, cache_v, state_wkv, state_shift, state_conv, page_table,
              c_prompt, c_sample, w_ada, b_ada, norm1_g, w_in, mu_shift, rwkv_w0, rwkv_w2, rwkv_a0,
              rwkv_a2, rwkv_g2, rwkv_kk, rwkv_ka, rwkv_rk, lnx_g, lnx_b, w_proj_a, w_proj_b, w_out,
              norm2_g, w_up, conv_w, conv_b, w_down, normf_g):
    p = {'w_ada': w_ada, 'b_ada': b_ada, 'norm1_g': norm1_g, 'w_in': w_in, 'mu_shift': mu_shift,
         'rwkv_w0': rwkv_w0, 'rwkv_w2': rwkv_w2, 'rwkv_a0': rwkv_a0, 'rwkv_a2': rwkv_a2,
         'rwkv_g2': rwkv_g2, 'rwkv_kk': rwkv_kk, 'rwkv_ka': rwkv_ka, 'rwkv_rk': rwkv_rk,
         'lnx_g': lnx_g, 'lnx_b': lnx_b, 'w_proj_a': w_proj_a, 'w_proj_b': w_proj_b,
         'w_out': w_out, 'norm2_g': norm2_g, 'w_up': w_up, 'conv_w': conv_w, 'conv_b': conv_b,
         'w_down': w_down}

    n_p, t_p = x_prompt.shape[0], x_prompt.shape[1]
    pos_p = jnp.arange(t_p, dtype=jnp.int32)
    h_p, k_prompt, v_prompt, wkv_prompt, shift_prompt, conv_prompt = _layer(
        x_prompt, c_prompt, pos_p, None, None,
        jnp.zeros((n_p, N_HEADS_A, HEAD_A, HEAD_A), jnp.float32),
        jnp.zeros((n_p, Z_A), x_prompt.dtype),
        jnp.zeros((n_p, CONV_W - 1, D_FF), x_prompt.dtype), p)
    y_prompt = _rmsnorm(h_p, normf_g)

    n_s, t_s = x_sample.shape[0], x_sample.shape[1]
    n_pages = page_table.shape[1]
    past_len = n_pages * PAGE_SIZE
    k_past = cache_k[page_table].reshape(n_s, past_len, N_HEADS_B, HEAD_B)
    v_past = cache_v[page_table].reshape(n_s, past_len, N_HEADS_B, HEAD_B)
    pos_s = past_len + jnp.arange(t_s, dtype=jnp.int32)
    h_s, k_sample, v_sample, wkv_sample, shift_sample, conv_sample = _layer(
        x_sample, c_sample, pos_s, k_past, v_past, state_wkv, state_shift, state_conv, p)
    y_sample = _rmsnorm(h_s, normf_g)
    return (y_prompt, y_sample, k_prompt, v_prompt, wkv_prompt, shift_prompt, conv_prompt, k_sample, v_sample, wkv_sample, shift_sample, conv_sample)
```

```python
import functools

import jax
import jax.numpy as jnp
from jax import lax
from jax.experimental import pallas as pl
from jax.experimental.pallas import tpu as pltpu

F32 = jnp.float32
BF16 = jnp.bfloat16

HEAD_A = 64
HEAD_B = 128
MOBA_BLOCK = 256
MOBA_TOPK = 3
PAGE_SIZE = 128
ROPE_THETA = 10000.0
LNX_EPS = 64e-5
RMS_EPS = 1e-6
NEG_INF = -1e30
CONV_W = 3
LORA_W = 64
LORA_A = 64

LANES = 128
SUBLANES = 8
VMEM_LIMIT = 52 * 1024 * 1024


def _round_up(x, m):
    return (x + m - 1) // m * m


def _pick(n, target, align):
    if n <= target:
        return n
    t = target - target % align
    while t >= align:
        if n % t == 0:
            return t
        t -= align
    return n


def _params(sem):
    return pltpu.CompilerParams(dimension_semantics=sem, vmem_limit_bytes=VMEM_LIMIT)


def _split3(x):
    hi = x.astype(BF16)
    r1 = x - hi.astype(F32)
    mid = r1.astype(BF16)
    lo = (r1 - mid.astype(F32)).astype(BF16)
    return hi, mid, lo


def _dg(a, b, dims):
    return lax.dot_general(a, b, (dims, ((), ())), preferred_element_type=F32)


_NN = ((1,), (0,))
_NT = ((1,), (1,))
_TN = ((0,), (0,))


def _dotx(a, b, dims=_NN):
    ah, am, _ = _split3(a)
    bh, bm, _ = _split3(b)
    return _dg(ah, bh, dims) + (_dg(ah, bm, dims) + _dg(am, bh, dims))


def _dot01(a01, b, dims=_NN):
    bh, bm, bl = _split3(b)
    return _dg(a01, bh, dims) + (_dg(a01, bm, dims) + _dg(a01, bl, dims))


def _dotr01(a, b01, dims=_NN):
    ah, am, al = _split3(a)
    return _dg(ah, b01, dims) + (_dg(am, b01, dims) + _dg(al, b01, dims))


def _sigmoid(x):
    return 1.0 / (1.0 + jnp.exp(-x))


def _softplus(x):
    return jnp.maximum(x, 0.0) + jnp.log(1.0 + jnp.exp(-jnp.abs(x)))


def _norm_mod(x, g, shift, scale):
    ms = jnp.mean(x * x, axis=-1, keepdims=True)
    y = x * lax.rsqrt(ms + RMS_EPS)
    return (y * g) * (1.0 + scale) + shift


def _mod_rows(mod_ref, idx, lo, n):
    if mod_ref.shape[1] == 1:
        return mod_ref[idx]
    return mod_ref[idx, lo:lo + n, :]


def _ada_kernel(c_ref, w_ref, b_ref, o_ref):
    c = c_ref[...]
    s = c * _sigmoid(c)
    o_ref[...] = _dotx(s, w_ref[...]) + b_ref[...]


def _ada(c_all, w_ada, b_ada):
    n, d = c_all.shape
    n6 = w_ada.shape[1]
    tn = _pick(n6, 512, LANES)
    return pl.pallas_call(
        _ada_kernel,
        grid=(n6 // tn,),
        in_specs=[pl.BlockSpec((n, d), lambda j: (0, 0)),
                  pl.BlockSpec((d, tn), lambda j: (0, j)),
                  pl.BlockSpec((1, tn), lambda j: (0, j))],
        out_specs=pl.BlockSpec((n, tn), lambda j: (0, j)),
        out_shape=jax.ShapeDtypeStruct((n, n6), F32),
        compiler_params=_params(("arbitrary",)),
        name="ada",
    )(c_all, w_ada, b_ada.reshape(1, n6))


class _Rows:
    def __init__(self, n_rows, seq_len, tm, mod):
        self.n_rows, self.seq_len, self.tm, self.mod = n_rows, seq_len, tm, mod
        self.per_row = mod.shape[2] != 1
        if not self.per_row:
            assert seq_len % tm == 0
        else:
            assert tm == n_rows
        self.n_tiles = n_rows // tm
        self.tiles_per_seq = max(seq_len // tm, 1)

    def mod_spec(self, width, col_of):
        r = self.tm if self.per_row else 1
        tps = self.tiles_per_seq
        if self.per_row:
            return pl.BlockSpec((None, 6, r, width), lambda i, j: (0, 0, i, col_of(i, j)))
        return pl.BlockSpec((None, 6, r, width), lambda i, j: (i // tps, 0, 0, col_of(i, j)))


def _in_kernel(x_ref, mod_ref, g_ref, w_ref, cos_ref, sin_ref, o_ref, h_scr, *, sub, rope_lo, rope_hi):
    j = pl.program_id(1)
    tm = x_ref.shape[0]

    @pl.when(j == 0)
    def _():
        for s in range(tm // sub):
            lo = s * sub
            h = _norm_mod(x_ref[lo:lo + sub, :], g_ref[...],
                          _mod_rows(mod_ref, 0, lo, sub), _mod_rows(mod_ref, 1, lo, sub))
            h_scr[lo:lo + sub, :] = h.astype(BF16)

    acc = jnp.dot(h_scr[...], w_ref[...], preferred_element_type=F32)
    is_rope = jnp.logical_and(j >= rope_lo, j < rope_hi)

    @pl.when(is_rope)
    def _():
        cos = cos_ref[...]
        sin = sin_ref[...]
        for c in range(acc.shape[1] // HEAD_B):
            a = acc[:, c * HEAD_B:(c + 1) * HEAD_B]
            o_ref[:, c * HEAD_B:(c + 1) * HEAD_B] = a * cos + pltpu.roll(a, HEAD_B // 2, 1) * sin

    @pl.when(jnp.logical_not(is_rope))
    def _():
        o_ref[...] = acc


def _in_proj(x2, rows, norm_g, w_bf, cos_t, sin_t, tn, rope_lo, rope_hi):
    n, d = x2.shape
    zp = w_bf.shape[1]
    tm = rows.tm
    n_tab = cos_t.shape[0] // tm
    sub = _pick(tm, 128, SUBLANES)
    return pl.pallas_call(
        functools.partial(_in_kernel, sub=sub, rope_lo=rope_lo, rope_hi=rope_hi),
        grid=(n // tm, zp // tn),
        in_specs=[pl.BlockSpec((tm, d), lambda i, j: (i, 0)),
                  rows.mod_spec(d, lambda i, j: 0),
                  pl.BlockSpec((1, d), lambda i, j: (0, 0)),
                  pl.BlockSpec((d, tn), lambda i, j: (0, j)),
                  pl.BlockSpec((tm, HEAD_B), lambda i, j: (i % n_tab, 0)),
                  pl.BlockSpec((tm, HEAD_B), lambda i, j: (i % n_tab, 0))],
        out_specs=pl.BlockSpec((tm, tn), lambda i, j: (i, j)),
        out_shape=jax.ShapeDtypeStruct((n, zp), F32),
        scratch_shapes=[pltpu.VMEM((tm, d), BF16)],
        compiler_params=_params(("arbitrary", "arbitrary")),
        name="in_proj",
    )(x2, rows.mod, norm_g.reshape(1, d), w_bf, cos_t, sin_t)


def _unit_lower_inverse(low, n, c, blk):
    rr = lax.broadcasted_iota(jnp.int32, (n, n), 0)
    cc = lax.broadcasted_iota(jnp.int32, (n, n), 1)
    eye = (rr == cc).astype(F32)
    same = (rr // blk) == (cc // blk)
    dpart = jnp.where(same, low, 0.0)
    inv = eye - dpart
    p = dpart
    k = 2
    while k < blk:
        p = _dotx(p, p)
        inv = inv + _dotx(inv, p)
        k *= 2
    if c > blk:
        m = _dotx(inv, low - dpart)
        minv = eye - m
        p = m
        k = 2
        while k < c // blk:
            p = _dotx(p, p)
            minv = minv + _dotx(minv, p)
            k *= 2
        inv = _dotx(minv, inv)
    return inv


def _rwkv_kernel(z_ref, shift0_ref, s0_ref, mu_ref, vec_ref, w2_ref, a2_ref, g2_ref,
                 yag_ref, sout_ref, state_scr, prev_scr, *, c_len, t_valid, da, n_chunks):
    ci = pl.program_id(1)
    n_pairs = da // LANES
    c2 = 2 * c_len

    @pl.when(ci == 0)
    def _():
        state_scr[...] = s0_ref[...]
        prev_scr[0:1, :] = shift0_ref[...]

    za = z_ref[...]
    row = lax.broadcasted_iota(jnp.int32, (c_len, 1), 0)
    zprev = jnp.where(row == 0, prev_scr[0:1, :], pltpu.roll(za, 1, 0))
    prev_scr[0:1, :] = za[c_len - 1:c_len, :]
    zmix = za + (zprev - za) * mu_ref[...]

    x_wa = zmix[:, 3 * da:3 * da + LANES]
    lane = lax.broadcasted_iota(jnp.int32, (c_len, LANES), 1)
    act_wa = jnp.where(lane < LORA_W, jnp.tanh(x_wa), x_wa)
    sig_g = _sigmoid(zmix[:, 3 * da + LANES:])
    lw = _dotx(act_wa, w2_ref[...])
    la = _dotx(act_wa, a2_ref[...])
    g = _dotx(sig_g, g2_ref[...])

    valid = row < t_valid
    head_lo = lane < HEAD_A
    ones_seg = ((lax.broadcasted_iota(jnp.int32, (LANES, LANES), 0) // HEAD_A)
                == (lax.broadcasted_iota(jnp.int32, (LANES, LANES), 1) // HEAD_A)).astype(BF16)
    tr = lax.broadcasted_iota(jnp.int32, (c_len, c_len), 0)
    tc = lax.broadcasted_iota(jnp.int32, (c_len, c_len), 1)
    tri = (tc <= tr).astype(BF16)
    rr = lax.broadcasted_iota(jnp.int32, (c2, c2), 0)
    cc = lax.broadcasted_iota(jnp.int32, (c2, c2), 1)
    same_head = (rr // c_len) == (cc // c_len)
    strict = jnp.logical_and(same_head, (cc % c_len) < (rr % c_len))
    incl = jnp.logical_and(same_head, (cc % c_len) <= (rr % c_len))
    er = lax.broadcasted_iota(jnp.int32, (LANES, LANES), 0)
    ec = lax.broadcasted_iota(jnp.int32, (LANES, LANES), 1)

    def stack(x):
        return jnp.concatenate([jnp.where(head_lo, x, 0.0), jnp.where(head_lo, 0.0, x)], axis=0)

    for p in range(n_pairs):
        sl = slice(p * LANES, (p + 1) * LANES)
        r_p = zmix[:, sl]
        ka_p = zmix[:, da + p * LANES:da + (p + 1) * LANES]
        v_p = zmix[:, 2 * da + p * LANES:2 * da + (p + 1) * LANES]
        w0, a0, kkw, kaw, rkw, lng, lnb = (vec_ref[i:i + 1, sl] for i in range(7))

        w_log = -_softplus(-(w0 + lw[:, sl])) - 0.5
        logw = jnp.where(valid, -jnp.exp(w_log), 0.0)
        a_p = _sigmoid(a0 + la[:, sl])
        kk = ka_p * kkw
        ss = _dotr01(kk * kk, ones_seg)
        kk = kk * lax.rsqrt(jnp.maximum(ss, 1e-24))
        k_p = ka_p * (1.0 + (a_p - 1.0) * kaw)
        b_p = jnp.where(valid, kk * a_p, 0.0)
        k_s = jnp.where(valid, k_p, 0.0)

        cum = _dot01(tri, logw)
        cum_end = cum[c_len - 1:c_len, :]
        e_neg = jnp.exp(-cum)
        e_end = jnp.exp(cum_end - cum)
        kt = stack(kk * jnp.exp(cum - logw))
        rt = stack(r_p * jnp.exp(cum))
        kd = stack(k_s * e_neg)
        bd = stack(b_p * e_neg)
        ke = stack(k_s * e_end)
        be = stack(b_p * e_end)
        vs = stack(v_p)

        s0t = state_scr[p]
        l_k = jnp.where(strict, _dotx(kt, kd, _NT), 0.0)
        l_b = jnp.where(strict, _dotx(kt, bd, _NT), 0.0)
        a_k = jnp.where(incl, _dotx(rt, kd, _NT), 0.0)
        a_b = jnp.where(incl, _dotx(rt, bd, _NT), 0.0)
        tinv = _unit_lower_inverse(l_b, c2, c_len, min(16, c_len))
        u = _dotx(tinv, _dotx(kt, s0t) + _dotx(l_k, vs))
        y2 = _dotx(rt, s0t) + _dotx(a_k, vs) - _dotx(a_b, u)
        y = y2[:c_len, :] + y2[c_len:, :]

        dgam = jnp.where(er == ec, jnp.exp(jnp.broadcast_to(cum_end, (LANES, LANES))), 0.0)
        state_scr[p] = _dotx(dgam, s0t) + _dotx(ke, vs, _TN) - _dotx(be, u, _TN)

        mu_h = _dotr01(y, ones_seg) * (1.0 / HEAD_A)
        yc = y - mu_h
        var = _dotr01(yc * yc, ones_seg) * (1.0 / HEAD_A)
        yn = yc * lax.rsqrt(var + LNX_EPS)
        bonus = _dotr01(r_p * k_p * rkw, ones_seg) * v_p
        ya = yn * lng + lnb + bonus
        yag_ref[:, sl] = (ya * g[:, sl]).astype(BF16)

    @pl.when(ci == n_chunks - 1)
    def _():
        sout_ref[...] = state_scr[...]


def _rwkv(z3, zap, shift0, s0, mu_p, vecs, w2p, a2p, g2p, c_len, t_valid, da):
    bsz, tp, _ = z3.shape
    n_chunks = tp // c_len
    n_pairs = da // LANES
    gw = zap - 3 * da - LANES
    kern = functools.partial(_rwkv_kernel, c_len=c_len, t_valid=t_valid, da=da, n_chunks=n_chunks)
    return pl.pallas_call(
        kern,
        grid=(bsz, n_chunks),
        in_specs=[pl.BlockSpec((None, c_len, zap), lambda b, c: (b, c, 0)),
                  pl.BlockSpec((None, 1, zap), lambda b, c: (b, 0, 0)),
                  pl.BlockSpec((None, n_pairs, LANES, LANES), lambda b, c: (b, 0, 0, 0)),
                  pl.BlockSpec((1, zap), lambda b, c: (0, 0)),
                  pl.BlockSpec((SUBLANES, da), lambda b, c: (0, 0)),
                  pl.BlockSpec((LANES, da), lambda b, c: (0, 0)),
                  pl.BlockSpec((LANES, da), lambda b, c: (0, 0)),
                  pl.BlockSpec((gw, da), lambda b, c: (0, 0))],
        out_specs=[pl.BlockSpec((None, c_len, da), lambda b, c: (b, c, 0)),
                   pl.BlockSpec((None, n_pairs, LANES, LANES), lambda b, c: (b, 0, 0, 0))],
        out_shape=[jax.ShapeDtypeStruct((bsz, tp, da), BF16),
                   jax.ShapeDtypeStruct((bsz, n_pairs, LANES, LANES), F32)],
        scratch_shapes=[pltpu.VMEM((n_pairs, LANES, LANES), F32),
                        pltpu.VMEM((SUBLANES, zap), F32)],
        compiler_params=_params(("arbitrary", "arbitrary")),
        name="rwkv",
    )(z3, shift0, s0, mu_p, vecs, w2p, a2p, g2p)


def _state_to_pairs(s):
    bsz, h = s.shape[:2]
    st = jnp.swapaxes(s, 2, 3).reshape(bsz, h // 2, 2, HEAD_A, HEAD_A)
    eye2 = jnp.eye(2, dtype=s.dtype)
    full = st[:, :, :, :, None, :] * eye2[None, None, :, None, :, None]
    return full.reshape(bsz, h // 2, LANES, LANES)


def _pairs_to_state(sp):
    bsz, n_pairs = sp.shape[:2]
    s6 = sp.reshape(bsz, n_pairs, 2, HEAD_A, 2, HEAD_A)
    diag = jnp.stack([s6[:, :, 0, :, 0, :], s6[:, :, 1, :, 1, :]], axis=2)
    return jnp.swapaxes(diag.reshape(bsz, 2 * n_pairs, HEAD_A, HEAD_A), 2, 3)


def _topk_mask(gate, valid, n_cand, axis):
    gm = jnp.where(valid, gate, NEG_INF)
    idx = lax.broadcasted_iota(jnp.int32, gate.shape, axis)
    cnt = jnp.zeros(gate.shape, jnp.int32)
    for m in range(n_cand):
        g_m = gm[:, m:m + 1] if axis == 1 else gm[m:m + 1, :]
        ahead = jnp.logical_or(g_m > gm, jnp.logical_and(g_m == gm, m < idx))
        cnt = cnt + ahead.astype(jnp.int32)
    return jnp.logical_and(valid, cnt < MOBA_TOPK)


def _attn_prompt_kernel(q_ref, k_ref, v_ref, o_ref, kmean_scr, *, n_blk):
    i = pl.program_id(2)
    blk = MOBA_BLOCK
    scale = HEAD_B ** -0.5

    @pl.when(i == 0)
    def _():
        kmean_scr[...] = jnp.zeros_like(kmean_scr)
        for n in range(n_blk):
            kmean_scr[n:n + 1, :] = jnp.sum(k_ref[n * blk:(n + 1) * blk, :], axis=0, keepdims=True) * (1.0 / blk)

    q = q_ref[...]
    qb = q.astype(BF16)
    gate = _dotx(q, kmean_scr[...], _NT)
    lane = lax.broadcasted_iota(jnp.int32, gate.shape, 1)
    sel = _topk_mask(gate, lane < i, n_blk, 1).astype(F32)

    def past(n, carry):
        m, l, acc = carry
        off = pl.multiple_of(n * blk, blk)
        kb = k_ref[pl.ds(off, blk), :].astype(BF16)
        vb = v_ref[pl.ds(off, blk), :].astype(BF16)
        s = _dg(qb, kb, _NT) * scale
        ok = jnp.max(jnp.where(lane == n, sel, 0.0), axis=1, keepdims=True) > 0.0
        s = jnp.where(ok, s, NEG_INF)
        m_new = jnp.maximum(m, jnp.max(s, axis=1, keepdims=True))
        pr = jnp.where(ok, jnp.exp(s - m_new), 0.0)
        alpha = jnp.exp(m - m_new)
        l = alpha * l + jnp.sum(pr, axis=1, keepdims=True)
        acc = alpha * acc + _dg(pr.astype(BF16), vb, _NN)
        return m_new, l, acc

    init = (jnp.full((blk, 1), NEG_INF, F32), jnp.zeros((blk, 1), F32), jnp.zeros((blk, HEAD_B), F32))
    m, l, acc = lax.fori_loop(0, i, past, init)

    off = pl.multiple_of(i * blk, blk)
    kb = k_ref[pl.ds(off, blk), :].astype(BF16)
    vb = v_ref[pl.ds(off, blk), :].astype(BF16)
    s = _dg(qb, kb, _NT) * scale
    qi = lax.broadcasted_iota(jnp.int32, (blk, blk), 0)
    ki = lax.broadcasted_iota(jnp.int32, (blk, blk), 1)
    ok = ki <= qi
    s = jnp.where(ok, s, NEG_INF)
    m_new = jnp.maximum(m, jnp.max(s, axis=1, keepdims=True))
    pr = jnp.where(ok, jnp.exp(s - m_new), 0.0)
    alpha = jnp.exp(m - m_new)
    l = alpha * l + jnp.sum(pr, axis=1, keepdims=True)
    acc = alpha * acc + _dg(pr.astype(BF16), vb, _NN)
    o_ref[...] = (acc / l).astype(BF16)


def _attn_prompt(z3, q_col, k_col, v_col, n_heads):
    bsz, t_len, _ = z3.shape
    n_blk = t_len // MOBA_BLOCK
    kern = functools.partial(_attn_prompt_kernel, n_blk=n_blk)
    return pl.pallas_call(
        kern,
        grid=(bsz, n_heads, n_blk),
        in_specs=[pl.BlockSpec((None, MOBA_BLOCK, HEAD_B), lambda b, h, i: (b, i, q_col + h)),
                  pl.BlockSpec((None, t_len, HEAD_B), lambda b, h, i: (b, 0, k_col + h)),
                  pl.BlockSpec((None, t_len, HEAD_B), lambda b, h, i: (b, 0, v_col + h))],
        out_specs=pl.BlockSpec((None, MOBA_BLOCK, HEAD_B), lambda b, h, i: (b, i, h)),
        out_shape=jax.ShapeDtypeStruct((bsz, t_len, n_heads * HEAD_B), BF16),
        scratch_shapes=[pltpu.VMEM((LANES, HEAD_B), F32)],
        compiler_params=_params(("arbitrary", "arbitrary", "arbitrary")),
        name="attn_prompt",
    )(z3, z3, z3)


def _attn_sample_kernel(pt_ref, qrep_ref, knew_ref, vnew_ref, ck_ref, cv_ref, o_ref,
                        qrows_scr, newk_scr, newv_scr, ksum_scr, sc_scr, sel_scr, acc_scr, l_scr,
                        *, n_pages, t_new, n_heads):
    del pt_ref
    j = pl.program_id(1)
    n_past_blk = n_pages * PAGE_SIZE // MOBA_BLOCK
    pages_per_blk = MOBA_BLOCK // PAGE_SIZE
    n_col = t_new * n_heads
    scale = HEAD_B ** -0.5
    dh = qrows_scr.shape[1]
    col = lax.broadcasted_iota(jnp.int32, (1, LANES), 1)

    @pl.when(j == 0)
    def _():
        rowi = lax.broadcasted_iota(jnp.int32, (n_col, dh), 0)
        lanei = lax.broadcasted_iota(jnp.int32, (n_col, dh), 1)
        qrows_scr[...] = jnp.zeros_like(qrows_scr)
        qrows_scr[0:n_col, :] = jnp.where(lanei // HEAD_B == rowi % n_heads, qrep_ref[...], 0.0)
        newk_scr[...] = jnp.zeros_like(newk_scr)
        newv_scr[...] = jnp.zeros_like(newv_scr)
        newk_scr[0:t_new, :] = knew_ref[...]
        newv_scr[0:t_new, :] = vnew_ref[...]
        ksum_scr[...] = jnp.zeros_like(ksum_scr)
        acc_scr[...] = jnp.zeros_like(acc_scr)

    def score_page(kpage, page):
        off = pl.multiple_of(page * PAGE_SIZE, PAGE_SIZE)
        sc_scr[pl.ds(off, PAGE_SIZE), :] = _dg(kpage.astype(BF16), qrows_scr[...].astype(BF16), _NT) * scale

    @pl.when(j < n_pages)
    def _():
        kpage = ck_ref[...]
        score_page(kpage, j)
        blk = j // pages_per_blk
        ksum_scr[pl.ds(blk, 1), :] = ksum_scr[pl.ds(blk, 1), :] + jnp.sum(kpage, axis=0, keepdims=True)

    @pl.when(j == n_pages)
    def _():
        score_page(newk_scr[...], n_pages)
        kmean = ksum_scr[...] * (1.0 / MOBA_BLOCK)
        gate = _dotx(kmean, qrows_scr[...], _NT)
        rowb = lax.broadcasted_iota(jnp.int32, gate.shape, 0)
        sel = _topk_mask(gate, rowb < n_past_blk, n_past_blk, 0)
        sel_scr[...] = sel.astype(F32)

        rown = lax.broadcasted_iota(jnp.int32, (PAGE_SIZE, LANES), 0)
        new_ok = jnp.logical_and(jnp.logical_and(rown < t_new, rown <= col // n_heads), col < n_col)
        new_off = n_pages * PAGE_SIZE
        s_new = sc_scr[new_off:new_off + PAGE_SIZE, :]
        m0 = jnp.max(jnp.where(new_ok, s_new, NEG_INF), axis=0, keepdims=True)

        def blk_max(n, m):
            off = pl.multiple_of(n * MOBA_BLOCK, MOBA_BLOCK)
            ok = sel_scr[pl.ds(n, 1), :] > 0.0
            bm = jnp.max(sc_scr[pl.ds(off, MOBA_BLOCK), :], axis=0, keepdims=True)
            return jnp.maximum(m, jnp.where(ok, bm, NEG_INF))

        m = lax.fori_loop(0, n_past_blk, blk_max, m0)
        p_new = jnp.where(new_ok, jnp.exp(s_new - m), 0.0)
        sc_scr[new_off:new_off + PAGE_SIZE, :] = p_new
        l0 = jnp.sum(p_new, axis=0, keepdims=True)

        def blk_exp(n, l):
            off = pl.multiple_of(n * MOBA_BLOCK, MOBA_BLOCK)
            ok = sel_scr[pl.ds(n, 1), :] > 0.0
            pr = jnp.where(ok, jnp.exp(sc_scr[pl.ds(off, MOBA_BLOCK), :] - m), 0.0)
            sc_scr[pl.ds(off, MOBA_BLOCK), :] = pr
            return l + jnp.sum(pr, axis=0, keepdims=True)

        l = lax.fori_loop(0, n_past_blk, blk_exp, l0)
        l_scr[...] = jnp.broadcast_to(jnp.where(col < n_col, l, 1.0), l_scr.shape)

    def pv_page(vpage, page):
        off = pl.multiple_of(page * PAGE_SIZE, PAGE_SIZE)
        pt = jnp.transpose(sc_scr[pl.ds(off, PAGE_SIZE), :]).astype(BF16)
        acc_scr[...] = acc_scr[...] + _dg(pt, vpage.astype(BF16), _NN)

    jv = j - (n_pages + 1)

    @pl.when(jnp.logical_and(jv >= 0, jv < n_pages))
    def _():
        pv_page(cv_ref[...], jv)

    @pl.when(jv == n_pages)
    def _():
        pv_page(newv_scr[...], n_pages)
        l_col = jnp.transpose(l_scr[...])
        out = acc_scr[...] / l_col[:, 0:1]
        subl = lax.broadcasted_iota(jnp.int32, (n_heads, dh), 0)
        lanei = lax.broadcasted_iota(jnp.int32, (n_heads, dh), 1)
        for qi in range(t_new):
            rows_q = out[qi * n_heads:(qi + 1) * n_heads, :]
            o_ref[qi:qi + 1, :] = jnp.sum(jnp.where(lanei // HEAD_B == subl, rows_q, 0.0),
                                          axis=0, keepdims=True).astype(BF16)


def _attn_sample(q3, k3, v3, cache_k, cache_v, page_table):
    n_seq, t_new, dh = q3.shape
    n_heads = dh // HEAD_B
    n_pages = page_table.shape[1]
    n_col = t_new * n_heads
    assert n_heads == SUBLANES and n_col <= LANES and (n_pages * PAGE_SIZE) % MOBA_BLOCK == 0
    assert n_pages * PAGE_SIZE // MOBA_BLOCK <= LANES
    qrep = jnp.broadcast_to(q3[:, :, None, :], (n_seq, t_new, n_heads, dh)).reshape(n_seq, n_col, dh)
    n_steps = 2 * (n_pages + 1)
    kern = functools.partial(_attn_sample_kernel, n_pages=n_pages, t_new=t_new, n_heads=n_heads)

    def k_map(b, j, pt):
        return (pt[b, jnp.minimum(j, n_pages - 1)], 0, 0)

    def v_map(b, j, pt):
        return (pt[b, jnp.clip(j - (n_pages + 1), 0, n_pages - 1)], 0, 0)

    grid_spec = pltpu.PrefetchScalarGridSpec(
        num_scalar_prefetch=1,
        grid=(n_seq, n_steps),
        in_specs=[pl.BlockSpec((None, n_col, dh), lambda b, j, pt: (b, 0, 0)),
                  pl.BlockSpec((None, t_new, dh), lambda b, j, pt: (b, 0, 0)),
                  pl.BlockSpec((None, t_new, dh), lambda b, j, pt: (b, 0, 0)),
                  pl.BlockSpec((None, PAGE_SIZE, dh), k_map),
                  pl.BlockSpec((None, PAGE_SIZE, dh), v_map)],
        out_specs=pl.BlockSpec((None, t_new, dh), lambda b, j, pt: (b, 0, 0)),
        scratch_shapes=[pltpu.VMEM((LANES, dh), F32),
                        pltpu.VMEM((PAGE_SIZE, dh), F32),
                        pltpu.VMEM((PAGE_SIZE, dh), F32),
                        pltpu.VMEM((LANES, dh), F32),
                        pltpu.VMEM(((n_pages + 1) * PAGE_SIZE, LANES), F32),
                        pltpu.VMEM((LANES, LANES), F32),
                        pltpu.VMEM((LANES, dh), F32),
                        pltpu.VMEM((LANES, LANES), F32)])
    return pl.pallas_call(
        kern,
        grid_spec=grid_spec,
        out_shape=jax.ShapeDtypeStruct((n_seq, t_new, dh), BF16),
        compiler_params=_params(("arbitrary", "arbitrary")),
        name="attn_sample",
    )(page_table, qrep, k3, v3, cache_k, cache_v)


def _mix_kernel(ya_ref, ob_ref, wa_ref, wb_ref, ga_ref, gb_ref, o_ref):
    oa = jnp.dot(ya_ref[...], wa_ref[...], preferred_element_type=F32)
    ob = jnp.dot(ob_ref[...], wb_ref[...], preferred_element_type=F32)
    o_ref[...] = (_sigmoid(ga_ref[...]) * oa + _sigmoid(gb_ref[...]) * ob).astype(BF16)


def _mix(yag, attn, wa_bf, wb_bf, z, ga_col, gb_col, tm, tn):
    n, da = yag.shape
    db = attn.shape[1]
    d = wa_bf.shape[1]
    return pl.pallas_call(
        _mix_kernel,
        grid=(n // tm, d // tn),
        in_specs=[pl.BlockSpec((tm, da), lambda i, j: (i, 0)),
                  pl.BlockSpec((tm, db), lambda i, j: (i, 0)),
                  pl.BlockSpec((da, tn), lambda i, j: (0, j)),
                  pl.BlockSpec((db, tn), lambda i, j: (0, j)),
                  pl.BlockSpec((tm, tn), lambda i, j: (i, ga_col + j)),
                  pl.BlockSpec((tm, tn), lambda i, j: (i, gb_col + j))],
        out_specs=pl.BlockSpec((tm, tn), lambda i, j: (i, j)),
        out_shape=jax.ShapeDtypeStruct((n, d), BF16),
        compiler_params=_params(("arbitrary", "arbitrary")),
        name="mix",
    )(yag, attn, wa_bf, wb_bf, z, z)


def _out_kernel(x_ref, mix_ref, w_ref, mod_ref, o_ref):
    acc = jnp.dot(mix_ref[...], w_ref[...], preferred_element_type=F32)
    o_ref[...] = x_ref[...] + _mod_rows(mod_ref, 2, 0, x_ref.shape[0]) * acc


def _out_proj(x2, mix, w_bf, rows, tn):
    n, d = x2.shape
    tm = rows.tm
    return pl.pallas_call(
        _out_kernel,
        grid=(n // tm, d // tn),
        in_specs=[pl.BlockSpec((tm, tn), lambda i, j: (i, j)),
                  pl.BlockSpec((tm, d), lambda i, j: (i, 0)),
                  pl.BlockSpec((d, tn), lambda i, j: (0, j)),
                  rows.mod_spec(tn, lambda i, j: j)],
        out_specs=pl.BlockSpec((tm, tn), lambda i, j: (i, j)),
        out_shape=jax.ShapeDtypeStruct((n, d), F32),
        compiler_params=_params(("arbitrary", "arbitrary")),
        name="out_proj",
    )(x2, mix, w_bf, rows.mod)


def _gelu_tanh(x):
    return 0.5 * x * (1.0 + jnp.tanh(0.7978845608028654 * (x + 0.044715 * (x * x * x))))


def _up_kernel(x_ref, mod_ref, g_ref, wg_ref, wv_ref, cw_ref, cb_ref, p1_ref, p2_ref,
               f_ref, tail_ref, h_scr, carry_scr, *, sub, seq_len):
    i = pl.program_id(0)
    j = pl.program_id(1)
    tm = x_ref.shape[0]

    @pl.when(j == 0)
    def _():
        for s in range(tm // sub):
            lo = s * sub
            h = _norm_mod(x_ref[lo:lo + sub, :], g_ref[...],
                          _mod_rows(mod_ref, 3, lo, sub), _mod_rows(mod_ref, 4, lo, sub))
            h_scr[lo:lo + sub, :] = h.astype(BF16)

    @pl.when(i == 0)
    def _():
        carry_scr[j] = jnp.zeros(carry_scr.shape[1:], F32)

    hb = h_scr[...]
    ug = jnp.dot(hb, wg_ref[...], preferred_element_type=F32)
    uv = jnp.dot(hb, wv_ref[...], preferred_element_type=F32)
    row = lax.broadcasted_iota(jnp.int32, (tm, 1), 0)
    pos = (i * tm + row) % seq_len
    c0 = carry_scr[j, 0:1, :]
    c1 = carry_scr[j, 1:2, :]
    s1 = jnp.where(row == 0, c1, pltpu.roll(ug, 1, 0))
    s2 = jnp.where(row == 0, c0, jnp.where(row == 1, c1, pltpu.roll(ug, 2, 0)))
    s1 = jnp.where(pos < 1, p1_ref[...], s1)
    s2 = jnp.where(pos < 2, p2_ref[...], s2)
    carry_scr[j, 0:2, :] = ug[tm - 2:tm, :]
    conv = cb_ref[...] + s2 * cw_ref[0:1, :] + s1 * cw_ref[1:2, :] + ug * cw_ref[2:3, :]
    f_ref[...] = (_gelu_tanh(conv) * uv).astype(BF16)
    if tail_ref.shape[0] == tm:
        tail_ref[...] = ug
    else:
        tail_ref[...] = ug[tm - SUBLANES:tm, :]


def _up(x1, rows, norm_g, wup_bf, conv_w, conv_b, p1, p2, tn, full_tail):
    n, d = x1.shape
    dff = conv_w.shape[1]
    tm = rows.tm
    ncol = dff // tn
    sub = _pick(tm, 128, SUBLANES)
    pr = p1.shape[0]
    tail_rows = tm if full_tail else SUBLANES
    prev_spec = (pl.BlockSpec((tm, tn), lambda i, j: (i, j)) if pr != 1
                 else pl.BlockSpec((1, tn), lambda i, j: (0, j)))
    kern = functools.partial(_up_kernel, sub=sub, seq_len=rows.seq_len)
    return pl.pallas_call(
        kern,
        grid=(n // tm, ncol),
        in_specs=[pl.BlockSpec((tm, d), lambda i, j: (i, 0)),
                  rows.mod_spec(d, lambda i, j: 0),
                  pl.BlockSpec((1, d), lambda i, j: (0, 0)),
                  pl.BlockSpec((d, tn), lambda i, j: (0, j)),
                  pl.BlockSpec((d, tn), lambda i, j: (0, ncol + j)),
                  pl.BlockSpec((CONV_W, tn), lambda i, j: (0, j)),
                  pl.BlockSpec((1, tn), lambda i, j: (0, j)),
                  prev_spec, prev_spec],
        out_specs=[pl.BlockSpec((tm, tn), lambda i, j: (i, j)),
                   pl.BlockSpec((tail_rows, tn), lambda i, j: (i, j))],
        out_shape=[jax.ShapeDtypeStruct((n, dff), BF16),
                   jax.ShapeDtypeStruct((n // tm * tail_rows, dff), F32)],
        scratch_shapes=[pltpu.VMEM((tm, d), BF16),
                        pltpu.VMEM((ncol, SUBLANES, tn), F32)],
        compiler_params=_params(("arbitrary", "arbitrary")),
        name="up_proj",
    )(x1, rows.mod, norm_g.reshape(1, d), wup_bf, wup_bf, conv_w, conv_b.reshape(1, dff), p1, p2)


def _down_kernel(f_ref, w_ref, x_ref, mod_ref, g_ref, o_ref, acc_scr, *, n_k):
    k = pl.program_id(1)

    @pl.when(k == 0)
    def _():
        acc_scr[...] = jnp.zeros_like(acc_scr)

    acc_scr[...] += jnp.dot(f_ref[...], w_ref[...], preferred_element_type=F32)

    @pl.when(k == n_k - 1)
    def _():
        x2 = x_ref[...] + _mod_rows(mod_ref, 5, 0, x_ref.shape[0]) * acc_scr[...]
        ms = jnp.mean(x2 * x2, axis=-1, keepdims=True)
        o_ref[...] = (x2 * lax.rsqrt(ms + RMS_EPS)) * g_ref[...]


def _down(f, wd_bf, x1, rows, normf_g, tk):
    n, dff = f.shape
    d = x1.shape[1]
    tm = rows.tm
    n_k = dff // tk
    return pl.pallas_call(
        functools.partial(_down_kernel, n_k=n_k),
        grid=(n // tm, n_k),
        in_specs=[pl.BlockSpec((tm, tk), lambda i, k: (i, k)),
                  pl.BlockSpec((tk, d), lambda i, k: (k, 0)),
                  pl.BlockSpec((tm, d), lambda i, k: (i, 0)),
                  rows.mod_spec(d, lambda i, k: 0),
                  pl.BlockSpec((1, d), lambda i, k: (0, 0))],
        out_specs=pl.BlockSpec((tm, d), lambda i, k: (i, 0)),
        out_shape=jax.ShapeDtypeStruct((n, d), F32),
        scratch_shapes=[pltpu.VMEM((tm, d), F32)],
        compiler_params=_params(("arbitrary", "arbitrary")),
        name="down_proj",
    )(f, wd_bf, x1, rows.mod, normf_g.reshape(1, d))


def _rope_tables(pos):
    half = HEAD_B // 2
    inv = ROPE_THETA ** (-jnp.arange(half, dtype=F32) / half)
    ang = pos.astype(F32)[:, None] * inv[None, :]
    cos, sin = jnp.cos(ang), jnp.sin(ang)
    return jnp.concatenate([cos, cos], axis=1), jnp.concatenate([-sin, sin], axis=1)


def _group(x, mod, pos, wts, shift0, wkv0, conv_prev, cache, tm_target):
    bsz, t_len, d = x.shape
    n = bsz * t_len
    lay = wts["layout"]
    tn, za, zap, da, db, dff = lay["tn"], lay["za"], lay["zap"], lay["da"], lay["db"], lay["dff"]
    x2 = x.reshape(n, d)
    per_row = cache is not None
    if per_row:
        tm = n
        mod4 = jnp.repeat(mod.reshape(bsz, 6, d), t_len, axis=0).swapaxes(0, 1)[None]
        cos_t, sin_t = _rope_tables(jnp.tile(pos, bsz))
    else:
        tm = _pick(t_len, tm_target, SUBLANES)
        mod4 = mod.reshape(bsz, 6, 1, d)
        cos_t, sin_t = _rope_tables(pos)
    rows = _Rows(n, t_len, tm, mod4)

    q_off = zap
    z = _in_proj(x2, rows, wts["norm1_g"], wts["w_in"], cos_t, sin_t, tn, q_off // tn, (q_off + 2 * db) // tn)
    zp = z.shape[1]
    z3 = z.reshape(bsz, t_len, zp)

    c_len = _pick(t_len, 64, SUBLANES) if t_len >= SUBLANES else SUBLANES
    t_pad = _round_up(t_len, c_len)
    z3a = z3 if t_pad == t_len else jnp.pad(z3[:, :, :zap], ((0, 0), (0, t_pad - t_len), (0, 0)))
    shift_p = jnp.pad(shift0, ((0, 0), (0, zap - za)))[:, None, :]
    yag, s_new = _rwkv(z3a, zap, shift_p, _state_to_pairs(wkv0), wts["mu"], wts["vecs"],
                       wts["w2p"], wts["a2p"], wts["g2p"], c_len, min(t_len, c_len), da)
    yag = yag[:, :t_len].reshape(n, da)
    wkv = _pairs_to_state(s_new)
    new_shift = z3[:, t_len - 1, :za]

    n_heads = db // HEAD_B
    k_new = z3[:, :, q_off + db:q_off + 2 * db]
    v_new = z3[:, :, q_off + 2 * db:q_off + 3 * db]
    if cache is None:
        attn = _attn_prompt(z3, q_off // HEAD_B, (q_off + db) // HEAD_B, (q_off + 2 * db) // HEAD_B, n_heads)
    else:
        cache_k, cache_v, page_table = cache
        n_pool = cache_k.shape[0]
        attn = _attn_sample(z3[:, :, q_off:q_off + db], k_new, v_new,
                            cache_k.reshape(n_pool, PAGE_SIZE, db), cache_v.reshape(n_pool, PAGE_SIZE, db),
                            page_table)
    attn = attn.reshape(n, db)

    tm_mix = _pick(tm, 512, SUBLANES)
    mix = _mix(yag, attn, wts["w_proj_a"], wts["w_proj_b"], z, (q_off + 3 * db) // tn,
               (q_off + 3 * db + d) // tn, tm_mix, tn)
    x1 = _out_proj(x2, mix, wts["w_out"], rows, tn)

    if per_row:
        zeros = jnp.zeros((bsz, t_len - 1, dff), F32)
        p1 = jnp.concatenate([conv_prev[:, 1:2], zeros], axis=1).reshape(n, dff)
        p2 = jnp.concatenate([conv_prev, zeros[:, :t_len - 2]], axis=1).reshape(n, dff)
    else:
        p1 = p2 = jnp.zeros((1, dff), F32)
    f, tail = _up(x1, rows, wts["norm2_g"], wts["w_up"], wts["conv_w"], wts["conv_b"], p1, p2, tn, per_row)
    if per_row:
        new_conv = tail.reshape(bsz, t_len, dff)[:, t_len - (CONV_W - 1):]
    else:
        tps = t_len // tm
        new_conv = tail.reshape(bsz, tps, SUBLANES, dff)[:, -1, SUBLANES - (CONV_W - 1):]
    y = _down(f, wts["w_down"], x1, rows, wts["normf_g"], tn).reshape(bsz, t_len, d)
    k_out = k_new.reshape(bsz, t_len, n_heads, HEAD_B)
    v_out = v_new.reshape(bsz, t_len, n_heads, HEAD_B)
    return y, k_out, v_out, wkv, new_shift, new_conv


def kernel(x_prompt, x_sample, cache_k, cache_v, state_wkv, state_shift, state_conv, page_table, c_prompt, c_sample, w_ada, b_ada, norm1_g, w_in, mu_shift, rwkv_w0, rwkv_w2, rwkv_a0, rwkv_a2, rwkv_g2, rwkv_kk, rwkv_ka, rwkv_rk, lnx_g, lnx_b, w_proj_a, w_proj_b, w_out, norm2_g, w_up, conv_w, conv_b, w_down, normf_g):
    d = x_prompt.shape[-1]
    da, db = w_proj_a.shape[0], w_proj_b.shape[0]
    za = mu_shift.shape[0]
    dff = conv_w.shape[1]
    lora_g = rwkv_g2.shape[0]
    assert rwkv_w2.shape[0] == LORA_W and rwkv_a2.shape[0] == LORA_A and LORA_W + LORA_A == LANES
    assert za == 3 * da + LORA_W + LORA_A + lora_g and da % LANES == 0 and db % HEAD_B == 0
    tn = 512 if all(v % 512 == 0 for v in (db, d, dff)) else LANES
    zap = _round_up(za, tn)
    gw = zap - 3 * da - LANES
    assert gw >= lora_g

    w_in_p = jnp.concatenate([w_in[:, :za].astype(BF16), jnp.zeros((d, zap - za), BF16),
                              w_in[:, za:].astype(BF16)], axis=1)
    zeros_l = jnp.zeros((LORA_W, da), F32)
    vecs = jnp.stack([rwkv_w0, rwkv_a0, rwkv_kk, rwkv_ka, rwkv_rk.reshape(da), lnx_g, lnx_b,
                      jnp.zeros((da,), F32)])
    wts = {
        "layout": dict(tn=tn, za=za, zap=zap, da=da, db=db, dff=dff),
        "norm1_g": norm1_g, "norm2_g": norm2_g, "normf_g": normf_g,
        "w_in": w_in_p,
        "mu": jnp.pad(mu_shift, (0, zap - za)).reshape(1, zap),
        "vecs": vecs,
        "w2p": jnp.concatenate([rwkv_w2, zeros_l], axis=0),
        "a2p": jnp.concatenate([zeros_l, rwkv_a2], axis=0),
        "g2p": jnp.pad(rwkv_g2, ((0, gw - lora_g), (0, 0))),
        "w_proj_a": w_proj_a.astype(BF16), "w_proj_b": w_proj_b.astype(BF16),
        "w_out": w_out.astype(BF16), "w_up": w_up.astype(BF16), "w_down": w_down.astype(BF16),
        "conv_w": conv_w, "conv_b": conv_b,
    }

    n_p, t_p = x_prompt.shape[:2]
    n_s, t_s = x_sample.shape[:2]
    n_c = _round_up(n_p + n_s, SUBLANES)
    c_all = jnp.concatenate([c_prompt, c_sample, jnp.zeros((n_c - n_p - n_s, d), F32)], axis=0)
    mod = _ada(c_all, w_ada, b_ada)

    n_heads_a = da // HEAD_A
    out_p = _group(x_prompt, mod[:n_p], jnp.arange(t_p, dtype=jnp.int32), wts,
                   jnp.zeros((n_p, za), F32), jnp.zeros((n_p, n_heads_a, HEAD_A, HEAD_A), F32),
                   None, None, 512)
    past_len = page_table.shape[1] * PAGE_SIZE
    out_s = _group(x_sample, mod[n_p:n_p + n_s], past_len + jnp.arange(t_s, dtype=jnp.int32), wts,
                   state_shift, state_wkv, state_conv, (cache_k, cache_v, page_table), 512)
    y_p, k_p, v_p, wkv_p, sh_p, cv_p = out_p
    y_s, k_s, v_s, wkv_s, sh_s, cv_s = out_s
    return (y_p, y_s, k_p, v_p, wkv_p, sh_p, cv_p, k_s, v_s, wkv_s, sh_s, cv_s)
```

```python
import functools

import jax
import jax.numpy as jnp
from jax import lax
from jax.experimental import pallas as pl
from jax.experimental.pallas import tpu as pltpu

F32 = jnp.float32
BF16 = jnp.bfloat16

HEAD_A = 64
HEAD_B = 128
MOBA_BLOCK = 256
MOBA_TOPK = 3
PAGE_SIZE = 128
ROPE_THETA = 10000.0
LNX_EPS = 64e-5
RMS_EPS = 1e-6
NEG_INF = -1e30
CONV_W = 3
LORA_W = 64
LORA_A = 64

LANES = 128
SUBLANES = 8
VMEM_LIMIT = 52 * 1024 * 1024


def _round_up(x, m):
    return (x + m - 1) // m * m


def _pick(n, target, align):
    if n <= target:
        return n
    t = target - target % align
    while t >= align:
        if n % t == 0:
            return t
        t -= align
    return n


def _params(sem):
    return pltpu.CompilerParams(dimension_semantics=sem, vmem_limit_bytes=VMEM_LIMIT)


def _split3(x):
    hi = x.astype(BF16)
    r1 = x - hi.astype(F32)
    mid = r1.astype(BF16)
    lo = (r1 - mid.astype(F32)).astype(BF16)
    return hi, mid, lo


def _dg(a, b, dims):
    return lax.dot_general(a, b, (dims, ((), ())), preferred_element_type=F32)


_NN = ((1,), (0,))
_NT = ((1,), (1,))
_TN = ((0,), (0,))


def _dot_pieces(a_pieces, b_pieces, dims):
    ca, cb = dims[0][0], dims[1][0]
    if a_pieces[0].shape[ca] % LANES == 0:
        return _dg(jnp.concatenate(a_pieces, axis=ca), jnp.concatenate(b_pieces, axis=cb), dims)
    out = _dg(a_pieces[0], b_pieces[0], dims)
    for x, y in zip(a_pieces[1:], b_pieces[1:]):
        out = out + _dg(x, y, dims)
    return out


def _dotx(a, b, dims=_NN):
    ah, am, _ = _split3(a)
    bh, bm, _ = _split3(b)
    return _dot_pieces([ah, ah, am], [bh, bm, bh], dims)


def _dot01(a01, b, dims=_NN):
    return _dot_pieces([a01, a01, a01], list(_split3(b)), dims)


def _dotr01(a, b01, dims=_NN):
    return _dot_pieces(list(_split3(a)), [b01, b01, b01], dims)


def _sigmoid(x):
    return 1.0 / (1.0 + jnp.exp(-x))


def _softplus(x):
    return jnp.maximum(x, 0.0) + jnp.log(1.0 + jnp.exp(-jnp.abs(x)))


def _norm_mod(x, g, shift, scale):
    ms = jnp.mean(x * x, axis=-1, keepdims=True)
    y = x * lax.rsqrt(ms + RMS_EPS)
    return (y * g) * (1.0 + scale) + shift


def _mod_rows(mod_ref, idx, lo, n):
    if mod_ref.shape[1] == 1:
        return mod_ref[idx]
    return mod_ref[idx, lo:lo + n, :]


def _ada_kernel(c_ref, w_ref, b_ref, o_ref):
    c = c_ref[...]
    s = c * _sigmoid(c)
    o_ref[...] = _dotx(s, w_ref[...]) + b_ref[...]


def _ada(c_all, w_ada, b_ada):
    n, d = c_all.shape
    n6 = w_ada.shape[1]
    tn = _pick(n6, 512, LANES)
    return pl.pallas_call(
        _ada_kernel,
        grid=(n6 // tn,),
        in_specs=[pl.BlockSpec((n, d), lambda j: (0, 0)),
                  pl.BlockSpec((d, tn), lambda j: (0, j)),
                  pl.BlockSpec((1, tn), lambda j: (0, j))],
        out_specs=pl.BlockSpec((n, tn), lambda j: (0, j)),
        out_shape=jax.ShapeDtypeStruct((n, n6), F32),
        compiler_params=_params(("arbitrary",)),
        name="ada",
    )(c_all, w_ada, b_ada.reshape(1, n6))


class _Rows:
    def __init__(self, n_rows, seq_len, tm, mod):
        self.n_rows, self.seq_len, self.tm, self.mod = n_rows, seq_len, tm, mod
        self.per_row = mod.shape[2] != 1
        if not self.per_row:
            assert seq_len % tm == 0
        else:
            assert tm == n_rows
        self.n_tiles = n_rows // tm
        self.tiles_per_seq = max(seq_len // tm, 1)

    def mod_spec(self, width, col_of):
        r = self.tm if self.per_row else 1
        tps = self.tiles_per_seq
        if self.per_row:
            return pl.BlockSpec((None, 6, r, width), lambda i, j: (0, 0, i, col_of(i, j)))
        return pl.BlockSpec((None, 6, r, width), lambda i, j: (i // tps, 0, 0, col_of(i, j)))


def _in_kernel(x_ref, mod_ref, g_ref, w_ref, cos_ref, sin_ref, o_ref, h_scr, *, sub, rope_lo, rope_hi):
    j = pl.program_id(1)
    tm = x_ref.shape[0]

    @pl.when(j == 0)
    def _():
        for s in range(tm // sub):
            lo = s * sub
            h = _norm_mod(x_ref[lo:lo + sub, :], g_ref[...],
                          _mod_rows(mod_ref, 0, lo, sub), _mod_rows(mod_ref, 1, lo, sub))
            h_scr[lo:lo + sub, :] = h.astype(BF16)

    acc = jnp.dot(h_scr[...], w_ref[...], preferred_element_type=F32)
    is_rope = jnp.logical_and(j >= rope_lo, j < rope_hi)

    @pl.when(is_rope)
    def _():
        cos = cos_ref[...]
        sin = sin_ref[...]
        for c in range(acc.shape[1] // HEAD_B):
            a = acc[:, c * HEAD_B:(c + 1) * HEAD_B]
            o_ref[:, c * HEAD_B:(c + 1) * HEAD_B] = a * cos + pltpu.roll(a, HEAD_B // 2, 1) * sin

    @pl.when(jnp.logical_not(is_rope))
    def _():
        o_ref[...] = acc


def _in_proj(x2, rows, norm_g, w_bf, cos_t, sin_t, tn, rope_lo, rope_hi):
    n, d = x2.shape
    zp = w_bf.shape[1]
    tm = rows.tm
    n_tab = cos_t.shape[0] // tm
    sub = _pick(tm, 128, SUBLANES)
    return pl.pallas_call(
        functools.partial(_in_kernel, sub=sub, rope_lo=rope_lo, rope_hi=rope_hi),
        grid=(n // tm, zp // tn),
        in_specs=[pl.BlockSpec((tm, d), lambda i, j: (i, 0)),
                  rows.mod_spec(d, lambda i, j: 0),
                  pl.BlockSpec((1, d), lambda i, j: (0, 0)),
                  pl.BlockSpec((d, tn), lambda i, j: (0, j)),
                  pl.BlockSpec((tm, HEAD_B), lambda i, j: (i % n_tab, 0)),
                  pl.BlockSpec((tm, HEAD_B), lambda i, j: (i % n_tab, 0))],
        out_specs=pl.BlockSpec((tm, tn), lambda i, j: (i, j)),
        out_shape=jax.ShapeDtypeStruct((n, zp), F32),
        scratch_shapes=[pltpu.VMEM((tm, d), BF16)],
        compiler_params=_params(("arbitrary", "arbitrary")),
        name="in_proj",
    )(x2, rows.mod, norm_g.reshape(1, d), w_bf, cos_t, sin_t)


def _unit_lower_inverse(low, n, c, blk):
    rr = lax.broadcasted_iota(jnp.int32, (n, n), 0)
    cc = lax.broadcasted_iota(jnp.int32, (n, n), 1)
    eye = (rr == cc).astype(F32)
    same = (rr // blk) == (cc // blk)
    dpart = [jnp.where(same, x, 0.0) for x in low]
    inv = [eye - d for d in dpart]
    p = dpart
    k = 2
    while k < blk:
        p = [_dotx(x, x) for x in p]
        inv = [i + _dotx(i, x) for i, x in zip(inv, p)]
        k *= 2
    if c > blk:
        m = [_dotx(i, x - d) for i, x, d in zip(inv, low, dpart)]
        minv = [eye - x for x in m]
        p = m
        k = 2
        while k < c // blk:
            p = [_dotx(x, x) for x in p]
            minv = [i + _dotx(i, x) for i, x in zip(minv, p)]
            k *= 2
        inv = [_dotx(mi, i) for mi, i in zip(minv, inv)]
    return inv


def _rwkv_kernel(z_ref, shift0_ref, s0_ref, mu_ref, vec_ref, w2_ref, a2_ref, g2_ref,
                 yag_ref, sout_ref, state_scr, prev_scr, *, c_len, t_valid, da, n_chunks):
    ci = pl.program_id(1)
    n_pairs = da // LANES
    c2 = 2 * c_len

    @pl.when(ci == 0)
    def _():
        state_scr[...] = s0_ref[...]
        prev_scr[0:1, :] = shift0_ref[...]

    za = z_ref[...]
    row = lax.broadcasted_iota(jnp.int32, (c_len, 1), 0)
    zprev = jnp.where(row == 0, prev_scr[0:1, :], pltpu.roll(za, 1, 0))
    prev_scr[0:1, :] = za[c_len - 1:c_len, :]
    zmix = za + (zprev - za) * mu_ref[...]

    x_wa = zmix[:, 3 * da:3 * da + LANES]
    lane = lax.broadcasted_iota(jnp.int32, (c_len, LANES), 1)
    act_wa = jnp.where(lane < LORA_W, jnp.tanh(x_wa), x_wa)
    sig_g = _sigmoid(zmix[:, 3 * da + LANES:])
    lw = _dotx(act_wa, w2_ref[...])
    la = _dotx(act_wa, a2_ref[...])
    g = _dotx(sig_g, g2_ref[...])

    valid = row < t_valid
    head_lo = lane < HEAD_A
    ones_seg = ((lax.broadcasted_iota(jnp.int32, (LANES, LANES), 0) // HEAD_A)
                == (lax.broadcasted_iota(jnp.int32, (LANES, LANES), 1) // HEAD_A)).astype(BF16)
    tr = lax.broadcasted_iota(jnp.int32, (c_len, c_len), 0)
    tc = lax.broadcasted_iota(jnp.int32, (c_len, c_len), 1)
    tri = (tc <= tr).astype(BF16)
    rr = lax.broadcasted_iota(jnp.int32, (c2, c2), 0)
    cc = lax.broadcasted_iota(jnp.int32, (c2, c2), 1)
    same_head = (rr // c_len) == (cc // c_len)
    strict = jnp.logical_and(same_head, (cc % c_len) < (rr % c_len))
    incl = jnp.logical_and(same_head, (cc % c_len) <= (rr % c_len))
    pairs = range(n_pairs)

    def lanes(x, p):
        return x[:, p * LANES:(p + 1) * LANES]

    def head_sums(x):
        xs = jnp.concatenate([lanes(x, p) for p in pairs], axis=0)
        s = _dotr01(xs, ones_seg)
        return jnp.concatenate([s[p * c_len:(p + 1) * c_len] for p in pairs], axis=1)

    def stack(x, p):
        xp = lanes(x, p)
        return jnp.concatenate([jnp.where(head_lo, xp, 0.0), jnp.where(head_lo, 0.0, xp)], axis=0)

    r = zmix[:, 0:da]
    ka = zmix[:, da:2 * da]
    v = zmix[:, 2 * da:3 * da]
    w0, a0, kkw, kaw, rkw, lng, lnb = (vec_ref[i:i + 1, :] for i in range(7))
    w_log = -_softplus(-(w0 + lw)) - 0.5
    logw = jnp.where(valid, -jnp.exp(w_log), 0.0)
    a = _sigmoid(a0 + la)
    kk = ka * kkw
    kk = kk * lax.rsqrt(jnp.maximum(head_sums(kk * kk), 1e-24))
    k_mod = ka * (1.0 + (a - 1.0) * kaw)
    bonus = head_sums(r * k_mod * rkw) * v
    b = jnp.where(valid, kk * a, 0.0)
    k_s = jnp.where(valid, k_mod, 0.0)

    cum = _dot01(tri, logw)
    cum_end = cum[c_len - 1:c_len, :]
    e_neg = jnp.exp(-cum)
    e_end = jnp.exp(cum_end - cum)
    gam_end = jnp.exp(cum_end)
    kt = kk * jnp.exp(cum - logw)
    rt = r * jnp.exp(cum)
    kd = k_s * e_neg
    bd = b * e_neg
    ke = k_s * e_end
    be = b * e_end

    xs = [jnp.concatenate([stack(kt, p), stack(rt, p)], axis=0) for p in pairs]
    ws = [jnp.concatenate([stack(kd, p), stack(bd, p)], axis=0) for p in pairs]
    es = [jnp.concatenate([stack(ke, p), stack(be, p)], axis=0) for p in pairs]
    vs = [stack(v, p) for p in pairs]
    s0 = [state_scr[p] for p in pairs]
    gm = [_dotx(x, w, _NT) for x, w in zip(xs, ws)]
    xs0 = [_dotx(x, s, _NT) for x, s in zip(xs, s0)]
    l_k = [jnp.where(strict, g_[:c2, :c2], 0.0) for g_ in gm]
    l_b = [jnp.where(strict, g_[:c2, c2:], 0.0) for g_ in gm]
    a_kb = [jnp.concatenate([jnp.where(incl, g_[c2:, :c2], 0.0), jnp.where(incl, -g_[c2:, c2:], 0.0)], axis=1)
            for g_ in gm]
    tinv = _unit_lower_inverse(l_b, c2, c_len, min(16, c_len))
    rhs = [x0[:c2] + _dotx(lk, v_) for x0, lk, v_ in zip(xs0, l_k, vs)]
    u = [_dotx(t, r_) for t, r_ in zip(tinv, rhs)]
    y2 = [x0[c2:] + _dotx(ab, jnp.concatenate([v_, u_], axis=0)) for x0, ab, v_, u_ in zip(xs0, a_kb, vs, u)]
    for p in pairs:
        upd = _dotx(jnp.concatenate([vs[p], -u[p]], axis=0), es[p], _TN)
        state_scr[p] = s0[p] * lanes(gam_end, p) + upd
    y = jnp.concatenate([y_[:c_len] + y_[c_len:] for y_ in y2], axis=1)

    yc = y - head_sums(y) * (1.0 / HEAD_A)
    var = head_sums(yc * yc) * (1.0 / HEAD_A)
    ya = yc * lax.rsqrt(var + LNX_EPS) * lng + lnb + bonus
    yag_ref[...] = (ya * g).astype(BF16)

    @pl.when(ci == n_chunks - 1)
    def _():
        sout_ref[...] = state_scr[...]


def _rwkv(z3, zap, shift0, s0, mu_p, vecs, w2p, a2p, g2p, c_len, t_valid, da):
    bsz, tp, _ = z3.shape
    n_chunks = tp // c_len
    n_pairs = da // LANES
    gw = zap - 3 * da - LANES
    kern = functools.partial(_rwkv_kernel, c_len=c_len, t_valid=t_valid, da=da, n_chunks=n_chunks)
    return pl.pallas_call(
        kern,
        grid=(bsz, n_chunks),
        in_specs=[pl.BlockSpec((None, c_len, zap), lambda b, c: (b, c, 0)),
                  pl.BlockSpec((None, 1, zap), lambda b, c: (b, 0, 0)),
                  pl.BlockSpec((None, n_pairs, LANES, LANES), lambda b, c: (b, 0, 0, 0)),
                  pl.BlockSpec((1, zap), lambda b, c: (0, 0)),
                  pl.BlockSpec((SUBLANES, da), lambda b, c: (0, 0)),
                  pl.BlockSpec((LANES, da), lambda b, c: (0, 0)),
                  pl.BlockSpec((LANES, da), lambda b, c: (0, 0)),
                  pl.BlockSpec((gw, da), lambda b, c: (0, 0))],
        out_specs=[pl.BlockSpec((None, c_len, da), lambda b, c: (b, c, 0)),
                   pl.BlockSpec((None, n_pairs, LANES, LANES), lambda b, c: (b, 0, 0, 0))],
        out_shape=[jax.ShapeDtypeStruct((bsz, tp, da), BF16),
                   jax.ShapeDtypeStruct((bsz, n_pairs, LANES, LANES), F32)],
        scratch_shapes=[pltpu.VMEM((n_pairs, LANES, LANES), F32),
                        pltpu.VMEM((SUBLANES, zap), F32)],
        compiler_params=_params(("arbitrary", "arbitrary")),
        name="rwkv",
    )(z3, shift0, s0, mu_p, vecs, w2p, a2p, g2p)


def _state_to_pairs(s):
    bsz, h = s.shape[:2]
    st = s.reshape(bsz, h // 2, 2, HEAD_A, HEAD_A)
    eye2 = jnp.eye(2, dtype=s.dtype)
    full = st[:, :, :, :, None, :] * eye2[None, None, :, None, :, None]
    return full.reshape(bsz, h // 2, LANES, LANES)


def _pairs_to_state(sp):
    bsz, n_pairs = sp.shape[:2]
    s6 = sp.reshape(bsz, n_pairs, 2, HEAD_A, 2, HEAD_A)
    diag = jnp.stack([s6[:, :, 0, :, 0, :], s6[:, :, 1, :, 1, :]], axis=2)
    return diag.reshape(bsz, 2 * n_pairs, HEAD_A, HEAD_A)


def _topk_mask(gate, valid, n_cand, axis):
    gm = jnp.where(valid, gate, NEG_INF)
    idx = lax.broadcasted_iota(jnp.int32, gate.shape, axis)
    cnt = jnp.zeros(gate.shape, jnp.int32)
    for m in range(n_cand):
        g_m = gm[:, m:m + 1] if axis == 1 else gm[m:m + 1, :]
        ahead = jnp.logical_or(g_m > gm, jnp.logical_and(g_m == gm, m < idx))
        cnt = cnt + ahead.astype(jnp.int32)
    return jnp.logical_and(valid, cnt < MOBA_TOPK)


def _attn_prompt_kernel(q_ref, k_ref, v_ref, o_ref, kmean_scr, *, n_blk):
    i = pl.program_id(2)
    blk = MOBA_BLOCK
    scale = HEAD_B ** -0.5

    @pl.when(i == 0)
    def _():
        kmean_scr[...] = jnp.zeros_like(kmean_scr)
        for n in range(n_blk):
            kmean_scr[n:n + 1, :] = jnp.sum(k_ref[n * blk:(n + 1) * blk, :], axis=0, keepdims=True) * (1.0 / blk)

    q = q_ref[...]
    qb = q.astype(BF16)
    gate = _dotx(q, kmean_scr[...], _NT)
    lane = lax.broadcasted_iota(jnp.int32, gate.shape, 1)
    sel = _topk_mask(gate, lane < i, n_blk, 1).astype(F32)

    def past(n, carry):
        m, l, acc = carry
        off = pl.multiple_of(n * blk, blk)
        kb = k_ref[pl.ds(off, blk), :].astype(BF16)
        vb = v_ref[pl.ds(off, blk), :].astype(BF16)
        s = _dg(qb, kb, _NT) * scale
        ok = jnp.max(jnp.where(lane == n, sel, 0.0), axis=1, keepdims=True) > 0.0
        s = jnp.where(ok, s, NEG_INF)
        m_new = jnp.maximum(m, jnp.max(s, axis=1, keepdims=True))
        pr = jnp.where(ok, jnp.exp(s - m_new), 0.0)
        alpha = jnp.exp(m - m_new)
        l = alpha * l + jnp.sum(pr, axis=1, keepdims=True)
        acc = alpha * acc + _dg(pr.astype(BF16), vb, _NN)
        return m_new, l, acc

    init = (jnp.full((blk, 1), NEG_INF, F32), jnp.zeros((blk, 1), F32), jnp.zeros((blk, HEAD_B), F32))
    m, l, acc = lax.fori_loop(0, i, past, init)

    off = pl.multiple_of(i * blk, blk)
    kb = k_ref[pl.ds(off, blk), :].astype(BF16)
    vb = v_ref[pl.ds(off, blk), :].astype(BF16)
    s = _dg(qb, kb, _NT) * scale
    qi = lax.broadcasted_iota(jnp.int32, (blk, blk), 0)
    ki = lax.broadcasted_iota(jnp.int32, (blk, blk), 1)
    ok = ki <= qi
    s = jnp.where(ok, s, NEG_INF)
    m_new = jnp.maximum(m, jnp.max(s, axis=1, keepdims=True))
    pr = jnp.where(ok, jnp.exp(s - m_new), 0.0)
    alpha = jnp.exp(m - m_new)
    l = alpha * l + jnp.sum(pr, axis=1, keepdims=True)
    acc = alpha * acc + _dg(pr.astype(BF16), vb, _NN)
    o_ref[...] = (acc / l).astype(BF16)


def _attn_prompt(z3, q_col, k_col, v_col, n_heads):
    bsz, t_len, _ = z3.shape
    n_blk = t_len // MOBA_BLOCK
    kern = functools.partial(_attn_prompt_kernel, n_blk=n_blk)
    return pl.pallas_call(
        kern,
        grid=(bsz, n_heads, n_blk),
        in_specs=[pl.BlockSpec((None, MOBA_BLOCK, HEAD_B), lambda b, h, i: (b, i, q_col + h)),
                  pl.BlockSpec((None, t_len, HEAD_B), lambda b, h, i: (b, 0, k_col + h)),
                  pl.BlockSpec((None, t_len, HEAD_B), lambda b, h, i: (b, 0, v_col + h))],
        out_specs=pl.BlockSpec((None, MOBA_BLOCK, HEAD_B), lambda b, h, i: (b, i, h)),
        out_shape=jax.ShapeDtypeStruct((bsz, t_len, n_heads * HEAD_B), BF16),
        scratch_shapes=[pltpu.VMEM((LANES, HEAD_B), F32)],
        compiler_params=_params(("arbitrary", "arbitrary", "arbitrary")),
        name="attn_prompt",
    )(z3, z3, z3)


def _attn_sample_kernel(pt_ref, q_ref, knew_ref, vnew_ref, ck_ref, cv_ref, o_ref,
                        newk_scr, newv_scr, ksum_scr, sc_scr, sel_scr, acc_scr, l_scr,
                        *, n_pages, t_new, n_heads):
    del pt_ref
    j = pl.program_id(1)
    n_past_blk = n_pages * PAGE_SIZE // MOBA_BLOCK
    pages_per_blk = MOBA_BLOCK // PAGE_SIZE
    n_col = t_new * n_heads
    n_key = PAGE_SIZE * n_heads
    scale = HEAD_B ** -0.5
    ci = lax.broadcasted_iota(jnp.int32, (n_col, n_key), 0)
    li = lax.broadcasted_iota(jnp.int32, (n_col, n_key), 1)
    diag = (li % n_heads) == (ci % n_heads)
    lane = lax.broadcasted_iota(jnp.int32, (n_col, LANES), 1)

    @pl.when(j == 0)
    def _():
        newk_scr[...] = jnp.zeros_like(newk_scr)
        newv_scr[...] = jnp.zeros_like(newv_scr)
        newk_scr[0:t_new] = knew_ref[...]
        newv_scr[0:t_new] = vnew_ref[...]
        ksum_scr[...] = jnp.zeros_like(ksum_scr)
        acc_scr[...] = jnp.zeros_like(acc_scr)

    def score_page(kpage, page):
        k2 = kpage.reshape(n_key, HEAD_B).astype(BF16)
        sc_scr[page] = _dg(q_ref[...].astype(BF16), k2, _NT) * scale

    @pl.when(j < n_pages)
    def _():
        kpage = ck_ref[...]
        score_page(kpage, j)
        blk = j // pages_per_blk
        ksum_scr[blk] = ksum_scr[blk] + jnp.sum(kpage, axis=0)

    def block_selected(pg):
        blk = pg // pages_per_blk
        return jnp.max(jnp.where(lane == blk, sel_scr[...], 0.0), axis=1, keepdims=True) > 0.0

    @pl.when(j == n_pages)
    def _():
        score_page(newk_scr[...], n_pages)
        n_bh = n_past_blk * n_heads
        kmean = ksum_scr[...].reshape(n_bh, HEAD_B) * (1.0 / MOBA_BLOCK)
        gt = _dotx(q_ref[...], kmean, _NT)
        gci = lax.broadcasted_iota(jnp.int32, (n_col, n_bh), 0)
        gli = lax.broadcasted_iota(jnp.int32, (n_col, n_bh), 1)
        gm = jnp.where((gli % n_heads) == (gci % n_heads), gt, 0.0)
        pick = ((lax.broadcasted_iota(jnp.int32, (n_bh, LANES), 0) // n_heads)
                == lax.broadcasted_iota(jnp.int32, (n_bh, LANES), 1)).astype(BF16)
        gate = _dotr01(gm, pick)
        sel_scr[...] = _topk_mask(gate, lane < n_past_blk, n_past_blk, 1).astype(F32)

        key_row = li // n_heads
        new_ok = jnp.logical_and(diag, jnp.logical_and(key_row < t_new, key_row <= ci // n_heads))
        s_new = sc_scr[n_pages]

        def pg_max(pg, mm):
            ok = jnp.logical_and(block_selected(pg), diag)
            return jnp.maximum(mm, jnp.where(ok, sc_scr[pg], NEG_INF))

        mm = lax.fori_loop(0, n_pages, pg_max, jnp.where(new_ok, s_new, NEG_INF))
        m = jnp.max(mm, axis=1, keepdims=True)
        p_new = jnp.where(new_ok, jnp.exp(s_new - m), 0.0)
        sc_scr[n_pages] = p_new

        def pg_exp(pg, ll):
            ok = jnp.logical_and(block_selected(pg), diag)
            pr = jnp.where(ok, jnp.exp(sc_scr[pg] - m), 0.0)
            sc_scr[pg] = pr
            return ll + pr

        ll = lax.fori_loop(0, n_pages, pg_exp, p_new)
        l_scr[...] = jnp.broadcast_to(jnp.sum(ll, axis=1, keepdims=True), l_scr.shape)

    def pv_page(vpage, page):
        v2 = vpage.reshape(n_key, HEAD_B).astype(BF16)
        acc_scr[...] = acc_scr[...] + _dg(sc_scr[page].astype(BF16), v2, _NN)

    jv = j - (n_pages + 1)

    @pl.when(jnp.logical_and(jv >= 0, jv < n_pages))
    def _():
        pv_page(cv_ref[...], jv)

    @pl.when(jv == n_pages)
    def _():
        pv_page(newv_scr[...], n_pages)
        o_ref[...] = (acc_scr[...] / l_scr[...]).astype(BF16)


def _attn_sample(q4, k4, v4, cache_k, cache_v, page_table):
    n_seq, t_new, n_heads, _ = q4.shape
    n_pages = page_table.shape[1]
    n_col = t_new * n_heads
    n_key = PAGE_SIZE * n_heads
    assert (n_pages * PAGE_SIZE) % MOBA_BLOCK == 0 and n_pages * PAGE_SIZE // MOBA_BLOCK <= LANES
    assert n_heads == SUBLANES and t_new <= PAGE_SIZE
    n_past_blk = n_pages * PAGE_SIZE // MOBA_BLOCK
    n_steps = 2 * (n_pages + 1)
    kern = functools.partial(_attn_sample_kernel, n_pages=n_pages, t_new=t_new, n_heads=n_heads)

    def k_map(b, j, pt):
        return (pt[b, jnp.minimum(j, n_pages - 1)], 0, 0, 0)

    def v_map(b, j, pt):
        return (pt[b, jnp.clip(j - (n_pages + 1), 0, n_pages - 1)], 0, 0, 0)

    grid_spec = pltpu.PrefetchScalarGridSpec(
        num_scalar_prefetch=1,
        grid=(n_seq, n_steps),
        in_specs=[pl.BlockSpec((None, n_col, HEAD_B), lambda b, j, pt: (b, 0, 0)),
                  pl.BlockSpec((None, t_new, n_heads, HEAD_B), lambda b, j, pt: (b, 0, 0, 0)),
                  pl.BlockSpec((None, t_new, n_heads, HEAD_B), lambda b, j, pt: (b, 0, 0, 0)),
                  pl.BlockSpec((None, PAGE_SIZE, n_heads, HEAD_B), k_map),
                  pl.BlockSpec((None, PAGE_SIZE, n_heads, HEAD_B), v_map)],
        out_specs=pl.BlockSpec((None, n_col, HEAD_B), lambda b, j, pt: (b, 0, 0)),
        scratch_shapes=[pltpu.VMEM((PAGE_SIZE, n_heads, HEAD_B), F32),
                        pltpu.VMEM((PAGE_SIZE, n_heads, HEAD_B), F32),
                        pltpu.VMEM((n_past_blk, n_heads, HEAD_B), F32),
                        pltpu.VMEM((n_pages + 1, n_col, n_key), F32),
                        pltpu.VMEM((n_col, LANES), F32),
                        pltpu.VMEM((n_col, HEAD_B), F32),
                        pltpu.VMEM((n_col, HEAD_B), F32)])
    return pl.pallas_call(
        kern,
        grid_spec=grid_spec,
        out_shape=jax.ShapeDtypeStruct((n_seq, n_col, HEAD_B), BF16),
        compiler_params=_params(("arbitrary", "arbitrary")),
        name="attn_sample",
    )(page_table, q4.reshape(n_seq, n_col, HEAD_B), k4, v4, cache_k, cache_v)


def _mix_kernel(ya_ref, ob_ref, wa_ref, wb_ref, ga_ref, gb_ref, o_ref):
    oa = jnp.dot(ya_ref[...], wa_ref[...], preferred_element_type=F32)
    ob = jnp.dot(ob_ref[...], wb_ref[...], preferred_element_type=F32)
    o_ref[...] = (_sigmoid(ga_ref[...]) * oa + _sigmoid(gb_ref[...]) * ob).astype(BF16)


def _mix(yag, attn, wa_bf, wb_bf, z, ga_col, gb_col, tm, tn):
    n, da = yag.shape
    db = attn.shape[1]
    d = wa_bf.shape[1]
    return pl.pallas_call(
        _mix_kernel,
        grid=(n // tm, d // tn),
        in_specs=[pl.BlockSpec((tm, da), lambda i, j: (i, 0)),
                  pl.BlockSpec((tm, db), lambda i, j: (i, 0)),
                  pl.BlockSpec((da, tn), lambda i, j: (0, j)),
                  pl.BlockSpec((db, tn), lambda i, j: (0, j)),
                  pl.BlockSpec((tm, tn), lambda i, j: (i, ga_col + j)),
                  pl.BlockSpec((tm, tn), lambda i, j: (i, gb_col + j))],
        out_specs=pl.BlockSpec((tm, tn), lambda i, j: (i, j)),
        out_shape=jax.ShapeDtypeStruct((n, d), BF16),
        compiler_params=_params(("arbitrary", "arbitrary")),
        name="mix",
    )(yag, attn, wa_bf, wb_bf, z, z)


def _out_kernel(x_ref, mix_ref, w_ref, mod_ref, o_ref):
    acc = jnp.dot(mix_ref[...], w_ref[...], preferred_element_type=F32)
    o_ref[...] = x_ref[...] + _mod_rows(mod_ref, 2, 0, x_ref.shape[0]) * acc


def _out_proj(x2, mix, w_bf, rows, tn):
    n, d = x2.shape
    tm = rows.tm
    return pl.pallas_call(
        _out_kernel,
        grid=(n // tm, d // tn),
        in_specs=[pl.BlockSpec((tm, tn), lambda i, j: (i, j)),
                  pl.BlockSpec((tm, d), lambda i, j: (i, 0)),
                  pl.BlockSpec((d, tn), lambda i, j: (0, j)),
                  rows.mod_spec(tn, lambda i, j: j)],
        out_specs=pl.BlockSpec((tm, tn), lambda i, j: (i, j)),
        out_shape=jax.ShapeDtypeStruct((n, d), F32),
        compiler_params=_params(("arbitrary", "arbitrary")),
        name="out_proj",
    )(x2, mix, w_bf, rows.mod)


def _gelu_tanh(x):
    return 0.5 * x * (1.0 + jnp.tanh(0.7978845608028654 * (x + 0.044715 * (x * x * x))))


def _up_kernel(x_ref, mod_ref, g_ref, wg_ref, wv_ref, cw_ref, cb_ref, p1_ref, p2_ref,
               f_ref, tail_ref, h_scr, carry_scr, *, sub, seq_len):
    i = pl.program_id(0)
    j = pl.program_id(1)
    tm = x_ref.shape[0]

    @pl.when(j == 0)
    def _():
        for s in range(tm // sub):
            lo = s * sub
            h = _norm_mod(x_ref[lo:lo + sub, :], g_ref[...],
                          _mod_rows(mod_ref, 3, lo, sub), _mod_rows(mod_ref, 4, lo, sub))
            h_scr[lo:lo + sub, :] = h.astype(BF16)

    @pl.when(i == 0)
    def _():
        carry_scr[j] = jnp.zeros(carry_scr.shape[1:], F32)

    hb = h_scr[...]
    ug = jnp.dot(hb, wg_ref[...], preferred_element_type=F32)
    uv = jnp.dot(hb, wv_ref[...], preferred_element_type=F32)
    row = lax.broadcasted_iota(jnp.int32, (tm, 1), 0)
    pos = (i * tm + row) % seq_len
    c0 = carry_scr[j, 0:1, :]
    c1 = carry_scr[j, 1:2, :]
    s1 = jnp.where(row == 0, c1, pltpu.roll(ug, 1, 0))
    s2 = jnp.where(row == 0, c0, jnp.where(row == 1, c1, pltpu.roll(ug, 2, 0)))
    s1 = jnp.where(pos < 1, p1_ref[...], s1)
    s2 = jnp.where(pos < 2, p2_ref[...], s2)
    carry_scr[j, 0:2, :] = ug[tm - 2:tm, :]
    conv = cb_ref[...] + s2 * cw_ref[0:1, :] + s1 * cw_ref[1:2, :] + ug * cw_ref[2:3, :]
    f_ref[...] = (_gelu_tanh(conv) * uv).astype(BF16)
    if tail_ref.shape[0] == tm:
        tail_ref[...] = ug
    else:
        tail_ref[...] = ug[tm - SUBLANES:tm, :]


def _up(x1, rows, norm_g, wup_bf, conv_w, conv_b, p1, p2, tn, full_tail):
    n, d = x1.shape
    dff = conv_w.shape[1]
    tm = rows.tm
    ncol = dff // tn
    sub = _pick(tm, 128, SUBLANES)
    pr = p1.shape[0]
    tail_rows = tm if full_tail else SUBLANES
    prev_spec = (pl.BlockSpec((tm, tn), lambda i, j: (i, j)) if pr != 1
                 else pl.BlockSpec((1, tn), lambda i, j: (0, j)))
    kern = functools.partial(_up_kernel, sub=sub, seq_len=rows.seq_len)
    return pl.pallas_call(
        kern,
        grid=(n // tm, ncol),
        in_specs=[pl.BlockSpec((tm, d), lambda i, j: (i, 0)),
                  rows.mod_spec(d, lambda i, j: 0),
                  pl.BlockSpec((1, d), lambda i, j: (0, 0)),
                  pl.BlockSpec((d, tn), lambda i, j: (0, j)),
                  pl.BlockSpec((d, tn), lambda i, j: (0, ncol + j)),
                  pl.BlockSpec((CONV_W, tn), lambda i, j: (0, j)),
                  pl.BlockSpec((1, tn), lambda i, j: (0, j)),
                  prev_spec, prev_spec],
        out_specs=[pl.BlockSpec((tm, tn), lambda i, j: (i, j)),
                   pl.BlockSpec((tail_rows, tn), lambda i, j: (i, j))],
        out_shape=[jax.ShapeDtypeStruct((n, dff), BF16),
                   jax.ShapeDtypeStruct((n // tm * tail_rows, dff), F32)],
        scratch_shapes=[pltpu.VMEM((tm, d), BF16),
                        pltpu.VMEM((ncol, SUBLANES, tn), F32)],
        compiler_params=_params(("arbitrary", "arbitrary")),
        name="up_proj",
    )(x1, rows.mod, norm_g.reshape(1, d), wup_bf, wup_bf, conv_w, conv_b.reshape(1, dff), p1, p2)


def _down_kernel(f_ref, w_ref, x_ref, mod_ref, g_ref, o_ref, acc_scr, *, n_k):
    k = pl.program_id(1)

    @pl.when(k == 0)
    def _():
        acc_scr[...] = jnp.zeros_like(acc_scr)

    acc_scr[...] += jnp.dot(f_ref[...], w_ref[...], preferred_element_type=F32)

    @pl.when(k == n_k - 1)
    def _():
        x2 = x_ref[...] + _mod_rows(mod_ref, 5, 0, x_ref.shape[0]) * acc_scr[...]
        ms = jnp.mean(x2 * x2, axis=-1, keepdims=True)
        o_ref[...] = (x2 * lax.rsqrt(ms + RMS_EPS)) * g_ref[...]


def _down(f, wd_bf, x1, rows, normf_g, tk):
    n, dff = f.shape
    d = x1.shape[1]
    tm = rows.tm
    n_k = dff // tk
    return pl.pallas_call(
        functools.partial(_down_kernel, n_k=n_k),
        grid=(n // tm, n_k),
        in_specs=[pl.BlockSpec((tm, tk), lambda i, k: (i, k)),
                  pl.BlockSpec((tk, d), lambda i, k: (k, 0)),
                  pl.BlockSpec((tm, d), lambda i, k: (i, 0)),
                  rows.mod_spec(d, lambda i, k: 0),
                  pl.BlockSpec((1, d), lambda i, k: (0, 0))],
        out_specs=pl.BlockSpec((tm, d), lambda i, k: (i, 0)),
        out_shape=jax.ShapeDtypeStruct((n, d), F32),
        scratch_shapes=[pltpu.VMEM((tm, d), F32)],
        compiler_params=_params(("arbitrary", "arbitrary")),
        name="down_proj",
    )(f, wd_bf, x1, rows.mod, normf_g.reshape(1, d))


def _rope_tables(pos):
    half = HEAD_B // 2
    inv = ROPE_THETA ** (-jnp.arange(half, dtype=F32) / half)
    ang = pos.astype(F32)[:, None] * inv[None, :]
    cos, sin = jnp.cos(ang), jnp.sin(ang)
    return jnp.concatenate([cos, cos], axis=1), jnp.concatenate([-sin, sin], axis=1)


def _group(x, mod, pos, wts, shift0, wkv0, conv_prev, cache, tm_target):
    bsz, t_len, d = x.shape
    n = bsz * t_len
    lay = wts["layout"]
    tn, za, zap, da, db, dff = lay["tn"], lay["za"], lay["zap"], lay["da"], lay["db"], lay["dff"]
    x2 = x.reshape(n, d)
    per_row = cache is not None
    if per_row:
        tm = n
        mod4 = jnp.repeat(mod.reshape(bsz, 6, d), t_len, axis=0).swapaxes(0, 1)[None]
        cos_t, sin_t = _rope_tables(jnp.tile(pos, bsz))
    else:
        tm = _pick(t_len, tm_target, SUBLANES)
        mod4 = mod.reshape(bsz, 6, 1, d)
        cos_t, sin_t = _rope_tables(pos)
    rows = _Rows(n, t_len, tm, mod4)

    q_off = zap
    z = _in_proj(x2, rows, wts["norm1_g"], wts["w_in"], cos_t, sin_t, tn, q_off // tn, (q_off + 2 * db) // tn)
    zp = z.shape[1]
    z3 = z.reshape(bsz, t_len, zp)

    c_len = _pick(t_len, 64, SUBLANES) if t_len >= SUBLANES else SUBLANES
    t_pad = _round_up(t_len, c_len)
    z3a = z3 if t_pad == t_len else jnp.pad(z3[:, :, :zap], ((0, 0), (0, t_pad - t_len), (0, 0)))
    shift_p = jnp.pad(shift0, ((0, 0), (0, zap - za)))[:, None, :]
    yag, s_new = _rwkv(z3a, zap, shift_p, _state_to_pairs(wkv0), wts["mu"], wts["vecs"],
                       wts["w2p"], wts["a2p"], wts["g2p"], c_len, min(t_len, c_len), da)
    yag = yag[:, :t_len].reshape(n, da)
    wkv = _pairs_to_state(s_new)
    new_shift = z3[:, t_len - 1, :za]

    n_heads = db // HEAD_B
    k_new = z3[:, :, q_off + db:q_off + 2 * db]
    v_new = z3[:, :, q_off + 2 * db:q_off + 3 * db]
    if cache is None:
        attn = _attn_prompt(z3, q_off // HEAD_B, (q_off + db) // HEAD_B, (q_off + 2 * db) // HEAD_B, n_heads)
    else:
        cache_k, cache_v, page_table = cache
        heads = lambda t: t.reshape(bsz, t_len, n_heads, HEAD_B)
        attn = _attn_sample(heads(z3[:, :, q_off:q_off + db]), heads(k_new), heads(v_new),
                            cache_k, cache_v, page_table)
    attn = attn.reshape(n, db)

    tm_mix = _pick(tm, 512, SUBLANES)
    mix = _mix(yag, attn, wts["w_proj_a"], wts["w_proj_b"], z, (q_off + 3 * db) // tn,
               (q_off + 3 * db + d) // tn, tm_mix, tn)
    x1 = _out_proj(x2, mix, wts["w_out"], rows, tn)

    if per_row:
        zeros = jnp.zeros((bsz, t_len - 1, dff), F32)
        p1 = jnp.concatenate([conv_prev[:, 1:2], zeros], axis=1).reshape(n, dff)
        p2 = jnp.concatenate([conv_prev, zeros[:, :t_len - 2]], axis=1).reshape(n, dff)
    else:
        p1 = p2 = jnp.zeros((1, dff), F32)
    f, tail = _up(x1, rows, wts["norm2_g"], wts["w_up"], wts["conv_w"], wts["conv_b"], p1, p2, tn, per_row)
    if per_row:
        new_conv = tail.reshape(bsz, t_len, dff)[:, t_len - (CONV_W - 1):]
    else:
        tps = t_len // tm
        new_conv = tail.reshape(bsz, tps, SUBLANES, dff)[:, -1, SUBLANES - (CONV_W - 1):]
    y = _down(f, wts["w_down"], x1, rows, wts["normf_g"], tn).reshape(bsz, t_len, d)
    k_out = k_new.reshape(bsz, t_len, n_heads, HEAD_B)
    v_out = v_new.reshape(bsz, t_len, n_heads, HEAD_B)
    return y, k_out, v_out, wkv, new_shift, new_conv


def kernel(x_prompt, x_sample, cache_k, cache_v, state_wkv, state_shift, state_conv, page_table, c_prompt, c_sample, w_ada, b_ada, norm1_g, w_in, mu_shift, rwkv_w0, rwkv_w2, rwkv_a0, rwkv_a2, rwkv_g2, rwkv_kk, rwkv_ka, rwkv_rk, lnx_g, lnx_b, w_proj_a, w_proj_b, w_out, norm2_g, w_up, conv_w, conv_b, w_down, normf_g):
    d = x_prompt.shape[-1]
    da, db = w_proj_a.shape[0], w_proj_b.shape[0]
    za = mu_shift.shape[0]
    dff = conv_w.shape[1]
    lora_g = rwkv_g2.shape[0]
    assert rwkv_w2.shape[0] == LORA_W and rwkv_a2.shape[0] == LORA_A and LORA_W + LORA_A == LANES
    assert za == 3 * da + LORA_W + LORA_A + lora_g and da % LANES == 0 and db % HEAD_B == 0
    tn = 512 if all(v % 512 == 0 for v in (db, d, dff)) else LANES
    zap = _round_up(za, tn)
    gw = zap - 3 * da - LANES
    assert gw >= lora_g

    w_in_p = jnp.concatenate([w_in[:, :za].astype(BF16), jnp.zeros((d, zap - za), BF16),
                              w_in[:, za:].astype(BF16)], axis=1)
    zeros_l = jnp.zeros((LORA_W, da), F32)
    vecs = jnp.stack([rwkv_w0, rwkv_a0, rwkv_kk, rwkv_ka, rwkv_rk.reshape(da), lnx_g, lnx_b,
                      jnp.zeros((da,), F32)])
    wts = {
        "layout": dict(tn=tn, za=za, zap=zap, da=da, db=db, dff=dff),
        "norm1_g": norm1_g, "norm2_g": norm2_g, "normf_g": normf_g,
        "w_in": w_in_p,
        "mu": jnp.pad(mu_shift, (0, zap - za)).reshape(1, zap),
        "vecs": vecs,
        "w2p": jnp.concatenate([rwkv_w2, zeros_l], axis=0),
        "a2p": jnp.concatenate([zeros_l, rwkv_a2], axis=0),
        "g2p": jnp.pad(rwkv_g2, ((0, gw - lora_g), (0, 0))),
        "w_proj_a": w_proj_a.astype(BF16), "w_proj_b": w_proj_b.astype(BF16),
        "w_out": w_out.astype(BF16), "w_up": w_up.astype(BF16), "w_down": w_down.astype(BF16),
        "conv_w": conv_w, "conv_b": conv_b,
    }

    n_p, t_p = x_prompt.shape[:2]
    n_s, t_s = x_sample.shape[:2]
    n_c = _round_up(n_p + n_s, SUBLANES)
    c_all = jnp.concatenate([c_prompt, c_sample, jnp.zeros((n_c - n_p - n_s, d), F32)], axis=0)
    mod = _ada(c_all, w_ada, b_ada)

    n_heads_a = da // HEAD_A
    out_p = _group(x_prompt, mod[:n_p], jnp.arange(t_p, dtype=jnp.int32), wts,
                   jnp.zeros((n_p, za), F32), jnp.zeros((n_p, n_heads_a, HEAD_A, HEAD_A), F32),
                   None, None, 512)
    past_len = page_table.shape[1] * PAGE_SIZE
    out_s = _group(x_sample, mod[n_p:n_p + n_s], past_len + jnp.arange(t_s, dtype=jnp.int32), wts,
                   state_shift, state_wkv, state_conv, (cache_k, cache_v, page_table), 512)
    y_p, k_p, v_p, wkv_p, sh_p, cv_p = out_p
    y_s, k_s, v_s, wkv_s, sh_s, cv_s = out_s
    return (y_p, y_s, k_p, v_p, wkv_p, sh_p, cv_p, k_s, v_s, wkv_s, sh_s, cv_s)
```

```python
import functools

import jax
import jax.numpy as jnp
from jax import lax
from jax.experimental import pallas as pl
from jax.experimental.pallas import tpu as pltpu

F32 = jnp.float32
BF16 = jnp.bfloat16

HEAD_A = 64
HEAD_B = 128
MOBA_BLOCK = 256
MOBA_TOPK = 3
PAGE_SIZE = 128
ROPE_THETA = 10000.0
LNX_EPS = 64e-5
RMS_EPS = 1e-6
NEG_INF = -1e30
CONV_W = 3
LORA_W = 64
LORA_A = 64

LANES = 128
SUBLANES = 8
VMEM_LIMIT = 52 * 1024 * 1024


def _round_up(x, m):
    return (x + m - 1) // m * m


def _pick(n, target, align):
    if n <= target:
        return n
    t = target - target % align
    while t >= align:
        if n % t == 0:
            return t
        t -= align
    return n


def _params(sem):
    return pltpu.CompilerParams(dimension_semantics=sem, vmem_limit_bytes=VMEM_LIMIT)


def _split3(x):
    hi = x.astype(BF16)
    r1 = x - hi.astype(F32)
    mid = r1.astype(BF16)
    lo = (r1 - mid.astype(F32)).astype(BF16)
    return hi, mid, lo


def _dg(a, b, dims):
    return lax.dot_general(a, b, (dims, ((), ())), preferred_element_type=F32)


_NN = ((1,), (0,))
_NT = ((1,), (1,))
_TN = ((0,), (0,))


def _dot_pieces(a_pieces, b_pieces, dims):
    ca, cb = dims[0][0], dims[1][0]
    if a_pieces[0].shape[ca] % LANES == 0:
        return _dg(jnp.concatenate(a_pieces, axis=ca), jnp.concatenate(b_pieces, axis=cb), dims)
    out = _dg(a_pieces[0], b_pieces[0], dims)
    for x, y in zip(a_pieces[1:], b_pieces[1:]):
        out = out + _dg(x, y, dims)
    return out


def _dotx(a, b, dims=_NN):
    ah, am, _ = _split3(a)
    bh, bm, _ = _split3(b)
    return _dot_pieces([ah, ah, am], [bh, bm, bh], dims)


def _dot01(a01, b, dims=_NN):
    return _dot_pieces([a01, a01, a01], list(_split3(b)), dims)


def _dotr01(a, b01, dims=_NN):
    return _dot_pieces(list(_split3(a)), [b01, b01, b01], dims)


def _sigmoid(x):
    return 1.0 / (1.0 + jnp.exp(-x))


def _softplus(x):
    return jnp.maximum(x, 0.0) + jnp.log(1.0 + jnp.exp(-jnp.abs(x)))


def _norm_mod(x, g, shift, scale):
    ms = jnp.mean(x * x, axis=-1, keepdims=True)
    y = x * lax.rsqrt(ms + RMS_EPS)
    return (y * g) * (1.0 + scale) + shift


def _mod_rows(mod_ref, idx, lo, n):
    if mod_ref.shape[1] == 1:
        return mod_ref[idx]
    return mod_ref[idx, lo:lo + n, :]


def _ada_kernel(c_ref, w_ref, b_ref, o_ref):
    c = c_ref[...]
    s = c * _sigmoid(c)
    o_ref[...] = _dotx(s, w_ref[...]) + b_ref[...]


def _ada(c_all, w_ada, b_ada):
    n, d = c_all.shape
    n6 = w_ada.shape[1]
    tn = _pick(n6, 512, LANES)
    return pl.pallas_call(
        _ada_kernel,
        grid=(n6 // tn,),
        in_specs=[pl.BlockSpec((n, d), lambda j: (0, 0)),
                  pl.BlockSpec((d, tn), lambda j: (0, j)),
                  pl.BlockSpec((1, tn), lambda j: (0, j))],
        out_specs=pl.BlockSpec((n, tn), lambda j: (0, j)),
        out_shape=jax.ShapeDtypeStruct((n, n6), F32),
        compiler_params=_params(("arbitrary",)),
        name="ada",
    )(c_all, w_ada, b_ada.reshape(1, n6))


class _Rows:
    def __init__(self, n_rows, seq_len, tm, mod):
        self.n_rows, self.seq_len, self.tm, self.mod = n_rows, seq_len, tm, mod
        self.per_row = mod.shape[2] != 1
        if not self.per_row:
            assert seq_len % tm == 0
        else:
            assert tm == n_rows
        self.n_tiles = n_rows // tm
        self.tiles_per_seq = max(seq_len // tm, 1)

    def mod_spec(self, width, col_of):
        r = self.tm if self.per_row else 1
        tps = self.tiles_per_seq
        if self.per_row:
            return pl.BlockSpec((None, 6, r, width), lambda i, j: (0, 0, i, col_of(i, j)))
        return pl.BlockSpec((None, 6, r, width), lambda i, j: (i // tps, 0, 0, col_of(i, j)))


def _in_kernel(x_ref, mod_ref, g_ref, w_ref, cos_ref, sin_ref, o_ref, k4_ref, v4_ref, h_scr,
               *, sub, q_lo, n_qkv):
    j = pl.program_id(1)
    tm = x_ref.shape[0]

    @pl.when(j == 0)
    def _():
        for s in range(tm // sub):
            lo = s * sub
            h = _norm_mod(x_ref[lo:lo + sub, :], g_ref[...],
                          _mod_rows(mod_ref, 0, lo, sub), _mod_rows(mod_ref, 1, lo, sub))
            h_scr[lo:lo + sub, :] = h.astype(BF16)

    acc = jnp.dot(h_scr[...], w_ref[...], preferred_element_type=F32)
    heads_per_tile = acc.shape[1] // HEAD_B

    def emit(rope, head_ref, tile):
        for c in range(heads_per_tile):
            a = acc[:, c * HEAD_B:(c + 1) * HEAD_B]
            if rope:
                a = a * cos_ref[...] + pltpu.roll(a, HEAD_B // 2, 1) * sin_ref[...]
            o_ref[:, c * HEAD_B:(c + 1) * HEAD_B] = a
            if head_ref is not None:
                head_ref[:, tile * heads_per_tile + c, :] = a

    for t in range(n_qkv):
        pl.when(j == q_lo + t)(functools.partial(emit, True, None, t))
        pl.when(j == q_lo + n_qkv + t)(functools.partial(emit, True, k4_ref, t))
        pl.when(j == q_lo + 2 * n_qkv + t)(functools.partial(emit, False, v4_ref, t))

    @pl.when(jnp.logical_or(j < q_lo, j >= q_lo + 3 * n_qkv))
    def _():
        o_ref[...] = acc


def _in_proj(x2, rows, norm_g, w_bf, cos_t, sin_t, tn, q_lo, n_qkv):
    n, d = x2.shape
    zp = w_bf.shape[1]
    tm = rows.tm
    n_tab = cos_t.shape[0] // tm
    sub = _pick(tm, 128, SUBLANES)
    n_heads = n_qkv * tn // HEAD_B
    head_spec = pl.BlockSpec((tm, n_heads, HEAD_B), lambda i, j: (i, 0, 0))
    head_shape = jax.ShapeDtypeStruct((n, n_heads, HEAD_B), F32)
    return pl.pallas_call(
        functools.partial(_in_kernel, sub=sub, q_lo=q_lo, n_qkv=n_qkv),
        grid=(n // tm, zp // tn),
        in_specs=[pl.BlockSpec((tm, d), lambda i, j: (i, 0)),
                  rows.mod_spec(d, lambda i, j: 0),
                  pl.BlockSpec((1, d), lambda i, j: (0, 0)),
                  pl.BlockSpec((d, tn), lambda i, j: (0, j)),
                  pl.BlockSpec((tm, HEAD_B), lambda i, j: (i % n_tab, 0)),
                  pl.BlockSpec((tm, HEAD_B), lambda i, j: (i % n_tab, 0))],
        out_specs=[pl.BlockSpec((tm, tn), lambda i, j: (i, j)), head_spec, head_spec],
        out_shape=[jax.ShapeDtypeStruct((n, zp), F32), head_shape, head_shape],
        scratch_shapes=[pltpu.VMEM((tm, d), BF16)],
        compiler_params=_params(("arbitrary", "arbitrary")),
        name="in_proj",
    )(x2, rows.mod, norm_g.reshape(1, d), w_bf, cos_t, sin_t)


def _unit_lower_inverse(low, n, c, blk):
    rr = lax.broadcasted_iota(jnp.int32, (n, n), 0)
    cc = lax.broadcasted_iota(jnp.int32, (n, n), 1)
    eye = (rr == cc).astype(F32)
    same = (rr // blk) == (cc // blk)
    dpart = [jnp.where(same, x, 0.0) for x in low]
    inv = [eye - d for d in dpart]
    p = dpart
    k = 2
    while k < blk:
        p = [_dotx(x, x) for x in p]
        inv = [i + _dotx(i, x) for i, x in zip(inv, p)]
        k *= 2
    if c > blk:
        m = [_dotx(i, x - d) for i, x, d in zip(inv, low, dpart)]
        minv = [eye - x for x in m]
        p = m
        k = 2
        while k < c // blk:
            p = [_dotx(x, x) for x in p]
            minv = [i + _dotx(i, x) for i, x in zip(minv, p)]
            k *= 2
        inv = [_dotx(mi, i) for mi, i in zip(minv, inv)]
    return inv


def _rwkv_kernel(z_ref, shift0_ref, s0_ref, mu_ref, vec_ref, w2_ref, a2_ref, g2_ref,
                 yag_ref, sout_ref, state_scr, prev_scr, *, c_len, t_valid, da, n_chunks):
    ci = pl.program_id(1)
    n_pairs = da // LANES
    c2 = 2 * c_len

    @pl.when(ci == 0)
    def _():
        state_scr[...] = s0_ref[...]
        prev_scr[0:1, :] = shift0_ref[...]

    za = z_ref[...]
    row = lax.broadcasted_iota(jnp.int32, (c_len, 1), 0)
    zprev = jnp.where(row == 0, prev_scr[0:1, :], pltpu.roll(za, 1, 0))
    prev_scr[0:1, :] = za[c_len - 1:c_len, :]
    zmix = za + (zprev - za) * mu_ref[...]

    x_wa = zmix[:, 3 * da:3 * da + LANES]
    lane = lax.broadcasted_iota(jnp.int32, (c_len, LANES), 1)
    act_wa = jnp.where(lane < LORA_W, jnp.tanh(x_wa), x_wa)
    sig_g = _sigmoid(zmix[:, 3 * da + LANES:])
    lw = _dotx(act_wa, w2_ref[...])
    la = _dotx(act_wa, a2_ref[...])
    g = _dotx(sig_g, g2_ref[...])

    valid = row < t_valid
    head_lo = lane < HEAD_A
    ones_seg = ((lax.broadcasted_iota(jnp.int32, (LANES, LANES), 0) // HEAD_A)
                == (lax.broadcasted_iota(jnp.int32, (LANES, LANES), 1) // HEAD_A)).astype(BF16)
    tr = lax.broadcasted_iota(jnp.int32, (c_len, c_len), 0)
    tc = lax.broadcasted_iota(jnp.int32, (c_len, c_len), 1)
    tri = (tc <= tr).astype(BF16)
    rr = lax.broadcasted_iota(jnp.int32, (c2, c2), 0)
    cc = lax.broadcasted_iota(jnp.int32, (c2, c2), 1)
    same_head = (rr // c_len) == (cc // c_len)
    strict = jnp.logical_and(same_head, (cc % c_len) < (rr % c_len))
    incl = jnp.logical_and(same_head, (cc % c_len) <= (rr % c_len))
    pairs = range(n_pairs)

    def lanes(x, p):
        return x[:, p * LANES:(p + 1) * LANES]

    def head_sums(x):
        xs = jnp.concatenate([lanes(x, p) for p in pairs], axis=0)
        s = _dotr01(xs, ones_seg)
        return jnp.concatenate([s[p * c_len:(p + 1) * c_len] for p in pairs], axis=1)

    def stack(x, p):
        xp = lanes(x, p)
        return jnp.concatenate([jnp.where(head_lo, xp, 0.0), jnp.where(head_lo, 0.0, xp)], axis=0)

    r = zmix[:, 0:da]
    ka = zmix[:, da:2 * da]
    v = zmix[:, 2 * da:3 * da]
    w0, a0, kkw, kaw, rkw, lng, lnb = (vec_ref[i:i + 1, :] for i in range(7))
    w_log = -_softplus(-(w0 + lw)) - 0.5
    logw = jnp.where(valid, -jnp.exp(w_log), 0.0)
    a = _sigmoid(a0 + la)
    kk = ka * kkw
    kk = kk * lax.rsqrt(jnp.maximum(head_sums(kk * kk), 1e-24))
    k_mod = ka * (1.0 + (a - 1.0) * kaw)
    bonus = head_sums(r * k_mod * rkw) * v
    b = jnp.where(valid, kk * a, 0.0)
    k_s = jnp.where(valid, k_mod, 0.0)

    cum = _dot01(tri, logw)
    cum_end = cum[c_len - 1:c_len, :]
    e_neg = jnp.exp(-cum)
    e_end = jnp.exp(cum_end - cum)
    gam_end = jnp.exp(cum_end)
    kt = kk * jnp.exp(cum - logw)
    rt = r * jnp.exp(cum)
    kd = k_s * e_neg
    bd = b * e_neg
    ke = k_s * e_end
    be = b * e_end

    xs = [jnp.concatenate([stack(kt, p), stack(rt, p)], axis=0) for p in pairs]
    ws = [jnp.concatenate([stack(kd, p), stack(bd, p)], axis=0) for p in pairs]
    es = [jnp.concatenate([stack(ke, p), stack(be, p)], axis=0) for p in pairs]
    vs = [stack(v, p) for p in pairs]
    s0 = [state_scr[p] for p in pairs]
    gm = [_dotx(x, w, _NT) for x, w in zip(xs, ws)]
    xs0 = [_dotx(x, s, _NT) for x, s in zip(xs, s0)]
    l_k = [jnp.where(strict, g_[:c2, :c2], 0.0) for g_ in gm]
    l_b = [jnp.where(strict, g_[:c2, c2:], 0.0) for g_ in gm]
    a_kb = [jnp.concatenate([jnp.where(incl, g_[c2:, :c2], 0.0), jnp.where(incl, -g_[c2:, c2:], 0.0)], axis=1)
            for g_ in gm]
    tinv = _unit_lower_inverse(l_b, c2, c_len, min(16, c_len))
    rhs = [x0[:c2] + _dotx(lk, v_) for x0, lk, v_ in zip(xs0, l_k, vs)]
    u = [_dotx(t, r_) for t, r_ in zip(tinv, rhs)]
    y2 = [x0[c2:] + _dotx(ab, jnp.concatenate([v_, u_], axis=0)) for x0, ab, v_, u_ in zip(xs0, a_kb, vs, u)]
    for p in pairs:
        upd = _dotx(jnp.concatenate([vs[p], -u[p]], axis=0), es[p], _TN)
        state_scr[p] = s0[p] * lanes(gam_end, p) + upd
    y = jnp.concatenate([y_[:c_len] + y_[c_len:] for y_ in y2], axis=1)

    yc = y - head_sums(y) * (1.0 / HEAD_A)
    var = head_sums(yc * yc) * (1.0 / HEAD_A)
    ya = yc * lax.rsqrt(var + LNX_EPS) * lng + lnb + bonus
    yag_ref[...] = (ya * g).astype(BF16)

    @pl.when(ci == n_chunks - 1)
    def _():
        sout_ref[...] = state_scr[...]


def _rwkv(z3, zap, shift0, s0, mu_p, vecs, w2p, a2p, g2p, c_len, t_valid, da):
    bsz, tp, _ = z3.shape
    n_chunks = tp // c_len
    n_pairs = da // LANES
    gw = zap - 3 * da - LANES
    kern = functools.partial(_rwkv_kernel, c_len=c_len, t_valid=t_valid, da=da, n_chunks=n_chunks)
    return pl.pallas_call(
        kern,
        grid=(bsz, n_chunks),
        in_specs=[pl.BlockSpec((None, c_len, zap), lambda b, c: (b, c, 0)),
                  pl.BlockSpec((None, 1, zap), lambda b, c: (b, 0, 0)),
                  pl.BlockSpec((None, n_pairs, LANES, LANES), lambda b, c: (b, 0, 0, 0)),
                  pl.BlockSpec((1, zap), lambda b, c: (0, 0)),
                  pl.BlockSpec((SUBLANES, da), lambda b, c: (0, 0)),
                  pl.BlockSpec((LANES, da), lambda b, c: (0, 0)),
                  pl.BlockSpec((LANES, da), lambda b, c: (0, 0)),
                  pl.BlockSpec((gw, da), lambda b, c: (0, 0))],
        out_specs=[pl.BlockSpec((None, c_len, da), lambda b, c: (b, c, 0)),
                   pl.BlockSpec((None, n_pairs, LANES, LANES), lambda b, c: (b, 0, 0, 0))],
        out_shape=[jax.ShapeDtypeStruct((bsz, tp, da), BF16),
                   jax.ShapeDtypeStruct((bsz, n_pairs, LANES, LANES), F32)],
        scratch_shapes=[pltpu.VMEM((n_pairs, LANES, LANES), F32),
                        pltpu.VMEM((SUBLANES, zap), F32)],
        compiler_params=_params(("arbitrary", "arbitrary")),
        name="rwkv",
    )(z3, shift0, s0, mu_p, vecs, w2p, a2p, g2p)


def _state_to_pairs(s):
    bsz, h = s.shape[:2]
    st = s.reshape(bsz, h // 2, 2, HEAD_A, HEAD_A)
    eye2 = jnp.eye(2, dtype=s.dtype)
    full = st[:, :, :, :, None, :] * eye2[None, None, :, None, :, None]
    return full.reshape(bsz, h // 2, LANES, LANES)


def _pairs_to_state(sp):
    bsz, n_pairs = sp.shape[:2]
    s6 = sp.reshape(bsz, n_pairs, 2, HEAD_A, 2, HEAD_A)
    diag = jnp.stack([s6[:, :, 0, :, 0, :], s6[:, :, 1, :, 1, :]], axis=2)
    return diag.reshape(bsz, 2 * n_pairs, HEAD_A, HEAD_A)


def _topk_mask(gate, valid, n_cand, axis):
    gm = jnp.where(valid, gate, NEG_INF)
    idx = lax.broadcasted_iota(jnp.int32, gate.shape, axis)
    cnt = jnp.zeros(gate.shape, jnp.int32)
    for m in range(n_cand):
        g_m = gm[:, m:m + 1] if axis == 1 else gm[m:m + 1, :]
        ahead = jnp.logical_or(g_m > gm, jnp.logical_and(g_m == gm, m < idx))
        cnt = cnt + ahead.astype(jnp.int32)
    return jnp.logical_and(valid, cnt < MOBA_TOPK)


def _attn_prompt_kernel(q_ref, k_ref, v_ref, o_ref, kmean_scr, *, n_blk):
    i = pl.program_id(2)
    blk = MOBA_BLOCK
    scale = HEAD_B ** -0.5

    @pl.when(i == 0)
    def _():
        kmean_scr[...] = jnp.zeros_like(kmean_scr)
        for n in range(n_blk):
            kmean_scr[n:n + 1, :] = jnp.sum(k_ref[n * blk:(n + 1) * blk, :], axis=0, keepdims=True) * (1.0 / blk)

    q = q_ref[...]
    qb = q.astype(BF16)
    gate = _dotx(q, kmean_scr[...], _NT)
    lane = lax.broadcasted_iota(jnp.int32, gate.shape, 1)
    sel = _topk_mask(gate, lane < i, n_blk, 1).astype(F32)
    qi = lax.broadcasted_iota(jnp.int32, (blk, blk), 0)
    ki = lax.broadcasted_iota(jnp.int32, (blk, blk), 1)

    def attend(own):
        width = (own + 1) * blk
        s = _dg(qb, k_ref[0:width, :].astype(BF16), _NT) * scale
        parts = [jnp.where(sel[:, n:n + 1] > 0.0, s[:, n * blk:(n + 1) * blk], NEG_INF) for n in range(own)]
        parts.append(jnp.where(ki <= qi, s[:, own * blk:], NEG_INF))
        m = functools.reduce(jnp.maximum, [jnp.max(x, axis=1, keepdims=True) for x in parts])
        probs = [jnp.exp(x - m) for x in parts]
        l = functools.reduce(lambda a, b: a + b, [jnp.sum(x, axis=1, keepdims=True) for x in probs])
        pcat = jnp.concatenate([x.astype(BF16) for x in probs], axis=1)
        acc = _dg(pcat, v_ref[0:width, :].astype(BF16), _NN)
        o_ref[...] = (acc / l).astype(BF16)

    for own in range(n_blk):
        pl.when(i == own)(functools.partial(attend, own))


def _attn_prompt(z3, q_col, k_col, v_col, n_heads):
    bsz, t_len, _ = z3.shape
    n_blk = t_len // MOBA_BLOCK
    kern = functools.partial(_attn_prompt_kernel, n_blk=n_blk)
    return pl.pallas_call(
        kern,
        grid=(bsz, n_heads, n_blk),
        in_specs=[pl.BlockSpec((None, MOBA_BLOCK, HEAD_B), lambda b, h, i: (b, i, q_col + h)),
                  pl.BlockSpec((None, t_len, HEAD_B), lambda b, h, i: (b, 0, k_col + h)),
                  pl.BlockSpec((None, t_len, HEAD_B), lambda b, h, i: (b, 0, v_col + h))],
        out_specs=pl.BlockSpec((None, MOBA_BLOCK, HEAD_B), lambda b, h, i: (b, i, h)),
        out_shape=jax.ShapeDtypeStruct((bsz, t_len, n_heads * HEAD_B), BF16),
        scratch_shapes=[pltpu.VMEM((LANES, HEAD_B), F32)],
        compiler_params=_params(("arbitrary", "arbitrary", "arbitrary")),
        name="attn_prompt",
    )(z3, z3, z3)


def _attn_sample_kernel(pt_ref, q_ref, knew_ref, vnew_ref, *refs, n_pages, t_new, n_heads, grp):
    del pt_ref
    ck_refs, cv_refs = refs[:grp], refs[grp:2 * grp]
    o_ref, newk_scr, newv_scr, ksum_scr, sc_scr, sel_scr, acc_scr, l_scr = refs[2 * grp:]
    n_grp = n_pages // grp
    j = pl.program_id(1)
    n_past_blk = n_pages * PAGE_SIZE // MOBA_BLOCK
    pages_per_blk = MOBA_BLOCK // PAGE_SIZE
    n_col = t_new * n_heads
    n_key = PAGE_SIZE * n_heads
    scale = HEAD_B ** -0.5
    ci = lax.broadcasted_iota(jnp.int32, (n_col, n_key), 0)
    li = lax.broadcasted_iota(jnp.int32, (n_col, n_key), 1)
    diag = (li % n_heads) == (ci % n_heads)
    lane = lax.broadcasted_iota(jnp.int32, (n_col, LANES), 1)

    @pl.when(j == 0)
    def _():
        newk_scr[...] = jnp.zeros_like(newk_scr)
        newv_scr[...] = jnp.zeros_like(newv_scr)
        newk_scr[0:t_new] = knew_ref[...]
        newv_scr[0:t_new] = vnew_ref[...]
        ksum_scr[...] = jnp.zeros_like(ksum_scr)
        acc_scr[...] = jnp.zeros_like(acc_scr)

    def score_page(kpage, page):
        k2 = kpage.reshape(n_key, HEAD_B).astype(BF16)
        sc_scr[page] = _dg(q_ref[...].astype(BF16), k2, _NT) * scale

    @pl.when(j < n_grp)
    def _():
        for g in range(grp):
            kpage = ck_refs[g][...]
            page = j * grp + g
            score_page(kpage, page)
            blk = page // pages_per_blk
            ksum_scr[blk] = ksum_scr[blk] + jnp.sum(kpage, axis=0)

    def block_selected(pg):
        blk = pg // pages_per_blk
        return jnp.max(jnp.where(lane == blk, sel_scr[...], 0.0), axis=1, keepdims=True) > 0.0

    @pl.when(j == n_grp - 1)
    def _():
        score_page(newk_scr[...], n_pages)
        n_bh = n_past_blk * n_heads
        kmean = ksum_scr[...].reshape(n_bh, HEAD_B) * (1.0 / MOBA_BLOCK)
        gt = _dotx(q_ref[...], kmean, _NT)
        gci = lax.broadcasted_iota(jnp.int32, (n_col, n_bh), 0)
        gli = lax.broadcasted_iota(jnp.int32, (n_col, n_bh), 1)
        gm = jnp.where((gli % n_heads) == (gci % n_heads), gt, 0.0)
        pick = ((lax.broadcasted_iota(jnp.int32, (n_bh, LANES), 0) // n_heads)
                == lax.broadcasted_iota(jnp.int32, (n_bh, LANES), 1)).astype(BF16)
        gate = _dotr01(gm, pick)
        sel_scr[...] = _topk_mask(gate, lane < n_past_blk, n_past_blk, 1).astype(F32)

        key_row = li // n_heads
        new_ok = jnp.logical_and(diag, jnp.logical_and(key_row < t_new, key_row <= ci // n_heads))
        s_new = sc_scr[n_pages]

        def pg_max(pg, mm):
            ok = jnp.logical_and(block_selected(pg), diag)
            return jnp.maximum(mm, jnp.where(ok, sc_scr[pg], NEG_INF))

        mm = lax.fori_loop(0, n_pages, pg_max, jnp.where(new_ok, s_new, NEG_INF))
        m = jnp.max(mm, axis=1, keepdims=True)
        p_new = jnp.where(new_ok, jnp.exp(s_new - m), 0.0)
        sc_scr[n_pages] = p_new

        def pg_exp(pg, ll):
            ok = jnp.logical_and(block_selected(pg), diag)
            pr = jnp.where(ok, jnp.exp(sc_scr[pg] - m), 0.0)
            sc_scr[pg] = pr
            return ll + pr

        ll = lax.fori_loop(0, n_pages, pg_exp, p_new)
        l_scr[...] = jnp.broadcast_to(jnp.sum(ll, axis=1, keepdims=True), l_scr.shape)

    def pv_page(vpage, page):
        v2 = vpage.reshape(n_key, HEAD_B).astype(BF16)
        acc_scr[...] = acc_scr[...] + _dg(sc_scr[page].astype(BF16), v2, _NN)

    @pl.when(j >= n_grp)
    def _():
        for g in range(grp):
            pv_page(cv_refs[g][...], (j - n_grp) * grp + g)

    @pl.when(j == 2 * n_grp - 1)
    def _():
        pv_page(newv_scr[...], n_pages)
        o_ref[...] = (acc_scr[...] / l_scr[...]).astype(BF16)


def _attn_sample(q4, k4, v4, cache_k, cache_v, page_table):
    n_seq, t_new, n_heads, _ = q4.shape
    n_pages = page_table.shape[1]
    n_col = t_new * n_heads
    n_key = PAGE_SIZE * n_heads
    assert (n_pages * PAGE_SIZE) % MOBA_BLOCK == 0 and n_pages * PAGE_SIZE // MOBA_BLOCK <= LANES
    assert n_heads == SUBLANES and t_new <= PAGE_SIZE
    n_past_blk = n_pages * PAGE_SIZE // MOBA_BLOCK
    grp = _pick(n_pages, 8, 1)
    n_grp = n_pages // grp
    kern = functools.partial(_attn_sample_kernel, n_pages=n_pages, t_new=t_new, n_heads=n_heads, grp=grp)
    page_block = (None, PAGE_SIZE, n_heads, HEAD_B)

    def k_spec(g):
        return pl.BlockSpec(page_block, lambda b, j, pt: (pt[b, jnp.minimum(j, n_grp - 1) * grp + g], 0, 0, 0))

    def v_spec(g):
        return pl.BlockSpec(page_block, lambda b, j, pt: (pt[b, jnp.maximum(j - n_grp, 0) * grp + g], 0, 0, 0))

    grid_spec = pltpu.PrefetchScalarGridSpec(
        num_scalar_prefetch=1,
        grid=(n_seq, 2 * n_grp),
        in_specs=[pl.BlockSpec((None, n_col, HEAD_B), lambda b, j, pt: (b, 0, 0)),
                  pl.BlockSpec((None, t_new, n_heads, HEAD_B), lambda b, j, pt: (b, 0, 0, 0)),
                  pl.BlockSpec((None, t_new, n_heads, HEAD_B), lambda b, j, pt: (b, 0, 0, 0))]
                 + [k_spec(g) for g in range(grp)] + [v_spec(g) for g in range(grp)],
        out_specs=pl.BlockSpec((None, n_col, HEAD_B), lambda b, j, pt: (b, 0, 0)),
        scratch_shapes=[pltpu.VMEM((PAGE_SIZE, n_heads, HEAD_B), F32),
                        pltpu.VMEM((PAGE_SIZE, n_heads, HEAD_B), F32),
                        pltpu.VMEM((n_past_blk, n_heads, HEAD_B), F32),
                        pltpu.VMEM((n_pages + 1, n_col, n_key), F32),
                        pltpu.VMEM((n_col, LANES), F32),
                        pltpu.VMEM((n_col, HEAD_B), F32),
                        pltpu.VMEM((n_col, HEAD_B), F32)])
    return pl.pallas_call(
        kern,
        grid_spec=grid_spec,
        out_shape=jax.ShapeDtypeStruct((n_seq, n_col, HEAD_B), BF16),
        compiler_params=_params(("arbitrary", "arbitrary")),
        name="attn_sample",
    )(page_table, q4.reshape(n_seq, n_col, HEAD_B), k4, v4, *([cache_k] * grp), *([cache_v] * grp))


def _mix_kernel(ya_ref, ob_ref, wa_ref, wb_ref, ga_ref, gb_ref, o_ref):
    oa = jnp.dot(ya_ref[...], wa_ref[...], preferred_element_type=F32)
    ob = jnp.dot(ob_ref[...], wb_ref[...], preferred_element_type=F32)
    o_ref[...] = (_sigmoid(ga_ref[...]) * oa + _sigmoid(gb_ref[...]) * ob).astype(BF16)


def _mix(yag, attn, wa_bf, wb_bf, z, ga_col, gb_col, tm, tn):
    n, da = yag.shape
    db = attn.shape[1]
    d = wa_bf.shape[1]
    return pl.pallas_call(
        _mix_kernel,
        grid=(n // tm, d // tn),
        in_specs=[pl.BlockSpec((tm, da), lambda i, j: (i, 0)),
                  pl.BlockSpec((tm, db), lambda i, j: (i, 0)),
                  pl.BlockSpec((da, tn), lambda i, j: (0, j)),
                  pl.BlockSpec((db, tn), lambda i, j: (0, j)),
                  pl.BlockSpec((tm, tn), lambda i, j: (i, ga_col + j)),
                  pl.BlockSpec((tm, tn), lambda i, j: (i, gb_col + j))],
        out_specs=pl.BlockSpec((tm, tn), lambda i, j: (i, j)),
        out_shape=jax.ShapeDtypeStruct((n, d), BF16),
        compiler_params=_params(("arbitrary", "arbitrary")),
        name="mix",
    )(yag, attn, wa_bf, wb_bf, z, z)


def _out_kernel(x_ref, mix_ref, w_ref, mod_ref, o_ref):
    acc = jnp.dot(mix_ref[...], w_ref[...], preferred_element_type=F32)
    o_ref[...] = x_ref[...] + _mod_rows(mod_ref, 2, 0, x_ref.shape[0]) * acc


def _out_proj(x2, mix, w_bf, rows, tn):
    n, d = x2.shape
    tm = rows.tm
    return pl.pallas_call(
        _out_kernel,
        grid=(n // tm, d // tn),
        in_specs=[pl.BlockSpec((tm, tn), lambda i, j: (i, j)),
                  pl.BlockSpec((tm, d), lambda i, j: (i, 0)),
                  pl.BlockSpec((d, tn), lambda i, j: (0, j)),
                  rows.mod_spec(tn, lambda i, j: j)],
        out_specs=pl.BlockSpec((tm, tn), lambda i, j: (i, j)),
        out_shape=jax.ShapeDtypeStruct((n, d), F32),
        compiler_params=_params(("arbitrary", "arbitrary")),
        name="out_proj",
    )(x2, mix, w_bf, rows.mod)


def _gelu_tanh(x):
    return 0.5 * x * (1.0 + jnp.tanh(0.7978845608028654 * (x + 0.044715 * (x * x * x))))


def _up_kernel(x_ref, mod_ref, g_ref, wg_ref, wv_ref, cw_ref, cb_ref, p1_ref, p2_ref,
               f_ref, tail_ref, h_scr, carry_scr, *, sub, seq_len):
    i = pl.program_id(0)
    j = pl.program_id(1)
    tm = x_ref.shape[0]

    @pl.when(j == 0)
    def _():
        for s in range(tm // sub):
            lo = s * sub
            h = _norm_mod(x_ref[lo:lo + sub, :], g_ref[...],
                          _mod_rows(mod_ref, 3, lo, sub), _mod_rows(mod_ref, 4, lo, sub))
            h_scr[lo:lo + sub, :] = h.astype(BF16)

    @pl.when(i == 0)
    def _():
        carry_scr[j] = jnp.zeros(carry_scr.shape[1:], F32)

    hb = h_scr[...]
    ug = jnp.dot(hb, wg_ref[...], preferred_element_type=F32)
    uv = jnp.dot(hb, wv_ref[...], preferred_element_type=F32)
    row = lax.broadcasted_iota(jnp.int32, (tm, 1), 0)
    pos = (i * tm + row) % seq_len
    c0 = carry_scr[j, 0:1, :]
    c1 = carry_scr[j, 1:2, :]
    s1 = jnp.where(row == 0, c1, pltpu.roll(ug, 1, 0))
    s2 = jnp.where(row == 0, c0, jnp.where(row == 1, c1, pltpu.roll(ug, 2, 0)))
    s1 = jnp.where(pos < 1, p1_ref[...], s1)
    s2 = jnp.where(pos < 2, p2_ref[...], s2)
    carry_scr[j, 0:2, :] = ug[tm - 2:tm, :]
    conv = cb_ref[...] + s2 * cw_ref[0:1, :] + s1 * cw_ref[1:2, :] + ug * cw_ref[2:3, :]
    f_ref[...] = (_gelu_tanh(conv) * uv).astype(BF16)
    if tail_ref.shape[0] == tm:
        tail_ref[...] = ug
    else:
        tail_ref[...] = ug[tm - SUBLANES:tm, :]


def _up(x1, rows, norm_g, wup_bf, conv_w, conv_b, p1, p2, tn, full_tail):
    n, d = x1.shape
    dff = conv_w.shape[1]
    tm = rows.tm
    ncol = dff // tn
    sub = _pick(tm, 128, SUBLANES)
    pr = p1.shape[0]
    tail_rows = tm if full_tail else SUBLANES
    prev_spec = (pl.BlockSpec((tm, tn), lambda i, j: (i, j)) if pr != 1
                 else pl.BlockSpec((1, tn), lambda i, j: (0, j)))
    kern = functools.partial(_up_kernel, sub=sub, seq_len=rows.seq_len)
    return pl.pallas_call(
        kern,
        grid=(n // tm, ncol),
        in_specs=[pl.BlockSpec((tm, d), lambda i, j: (i, 0)),
                  rows.mod_spec(d, lambda i, j: 0),
                  pl.BlockSpec((1, d), lambda i, j: (0, 0)),
                  pl.BlockSpec((d, tn), lambda i, j: (0, j)),
                  pl.BlockSpec((d, tn), lambda i, j: (0, ncol + j)),
                  pl.BlockSpec((CONV_W, tn), lambda i, j: (0, j)),
                  pl.BlockSpec((1, tn), lambda i, j: (0, j)),
                  prev_spec, prev_spec],
        out_specs=[pl.BlockSpec((tm, tn), lambda i, j: (i, j)),
                   pl.BlockSpec((tail_rows, tn), lambda i, j: (i, j))],
        out_shape=[jax.ShapeDtypeStruct((n, dff), BF16),
                   jax.ShapeDtypeStruct((n // tm * tail_rows, dff), F32)],
        scratch_shapes=[pltpu.VMEM((tm, d), BF16),
                        pltpu.VMEM((ncol, SUBLANES, tn), F32)],
        compiler_params=_params(("arbitrary", "arbitrary")),
        name="up_proj",
    )(x1, rows.mod, norm_g.reshape(1, d), wup_bf, wup_bf, conv_w, conv_b.reshape(1, dff), p1, p2)


def _down_kernel(f_ref, w_ref, x_ref, mod_ref, g_ref, o_ref, acc_scr, *, n_k):
    k = pl.program_id(1)

    @pl.when(k == 0)
    def _():
        acc_scr[...] = jnp.zeros_like(acc_scr)

    acc_scr[...] += jnp.dot(f_ref[...], w_ref[...], preferred_element_type=F32)

    @pl.when(k == n_k - 1)
    def _():
        x2 = x_ref[...] + _mod_rows(mod_ref, 5, 0, x_ref.shape[0]) * acc_scr[...]
        ms = jnp.mean(x2 * x2, axis=-1, keepdims=True)
        o_ref[...] = (x2 * lax.rsqrt(ms + RMS_EPS)) * g_ref[...]


def _down(f, wd_bf, x1, rows, normf_g, tk):
    n, dff = f.shape
    d = x1.shape[1]
    tm = rows.tm
    n_k = dff // tk
    return pl.pallas_call(
        functools.partial(_down_kernel, n_k=n_k),
        grid=(n // tm, n_k),
        in_specs=[pl.BlockSpec((tm, tk), lambda i, k: (i, k)),
                  pl.BlockSpec((tk, d), lambda i, k: (k, 0)),
                  pl.BlockSpec((tm, d), lambda i, k: (i, 0)),
                  rows.mod_spec(d, lambda i, k: 0),
                  pl.BlockSpec((1, d), lambda i, k: (0, 0))],
        out_specs=pl.BlockSpec((tm, d), lambda i, k: (i, 0)),
        out_shape=jax.ShapeDtypeStruct((n, d), F32),
        scratch_shapes=[pltpu.VMEM((tm, d), F32)],
        compiler_params=_params(("arbitrary", "arbitrary")),
        name="down_proj",
    )(f, wd_bf, x1, rows.mod, normf_g.reshape(1, d))


def _rope_tables(pos):
    half = HEAD_B // 2
    inv = ROPE_THETA ** (-jnp.arange(half, dtype=F32) / half)
    ang = pos.astype(F32)[:, None] * inv[None, :]
    cos, sin = jnp.cos(ang), jnp.sin(ang)
    return jnp.concatenate([cos, cos], axis=1), jnp.concatenate([-sin, sin], axis=1)


def _group(x, mod, pos, wts, shift0, wkv0, conv_prev, cache, tm_target):
    bsz, t_len, d = x.shape
    n = bsz * t_len
    lay = wts["layout"]
    tn, za, zap, da, db, dff = lay["tn"], lay["za"], lay["zap"], lay["da"], lay["db"], lay["dff"]
    x2 = x.reshape(n, d)
    per_row = cache is not None
    if per_row:
        tm = tm_big = n
        mod4 = jnp.repeat(mod.reshape(bsz, 6, d), t_len, axis=0).swapaxes(0, 1)[None]
        cos_t, sin_t = _rope_tables(jnp.tile(pos, bsz))
    else:
        tm = _pick(t_len, tm_target, SUBLANES)
        tm_big = _pick(t_len, 2 * tm_target, SUBLANES)
        mod4 = mod.reshape(bsz, 6, 1, d)
        cos_t, sin_t = _rope_tables(pos)
    rows = _Rows(n, t_len, tm, mod4)
    rows_big = _Rows(n, t_len, tm_big, mod4)

    q_off = zap
    z, k4, v4 = _in_proj(x2, rows_big, wts["norm1_g"], wts["w_in"], cos_t, sin_t, tn, q_off // tn, db // tn)
    zp = z.shape[1]
    z3 = z.reshape(bsz, t_len, zp)

    c_len = _pick(t_len, 64, SUBLANES) if t_len >= SUBLANES else SUBLANES
    t_pad = _round_up(t_len, c_len)
    z3a = z3 if t_pad == t_len else jnp.pad(z3[:, :, :zap], ((0, 0), (0, t_pad - t_len), (0, 0)))
    shift_p = jnp.pad(shift0, ((0, 0), (0, zap - za)))[:, None, :]
    yag, s_new = _rwkv(z3a, zap, shift_p, _state_to_pairs(wkv0), wts["mu"], wts["vecs"],
                       wts["w2p"], wts["a2p"], wts["g2p"], c_len, min(t_len, c_len), da)
    yag = yag[:, :t_len].reshape(n, da)
    wkv = _pairs_to_state(s_new)
    new_shift = z3[:, t_len - 1, :za]

    n_heads = db // HEAD_B
    k_out = k4.reshape(bsz, t_len, n_heads, HEAD_B)
    v_out = v4.reshape(bsz, t_len, n_heads, HEAD_B)
    if cache is None:
        attn = _attn_prompt(z3, q_off // HEAD_B, (q_off + db) // HEAD_B, (q_off + 2 * db) // HEAD_B, n_heads)
    else:
        cache_k, cache_v, page_table = cache
        q4 = z3[:, :, q_off:q_off + db].reshape(bsz, t_len, n_heads, HEAD_B)
        attn = _attn_sample(q4, k_out, v_out, cache_k, cache_v, page_table)
    attn = attn.reshape(n, db)

    mix = _mix(yag, attn, wts["w_proj_a"], wts["w_proj_b"], z, (q_off + 3 * db) // tn,
               (q_off + 3 * db + d) // tn, tm_big, tn)
    x1 = _out_proj(x2, mix, wts["w_out"], rows_big, _pick(d, 1024, tn))

    if per_row:
        zeros = jnp.zeros((bsz, t_len - 1, dff), F32)
        p1 = jnp.concatenate([conv_prev[:, 1:2], zeros], axis=1).reshape(n, dff)
        p2 = jnp.concatenate([conv_prev, zeros[:, :t_len - 2]], axis=1).reshape(n, dff)
    else:
        p1 = p2 = jnp.zeros((1, dff), F32)
    f, tail = _up(x1, rows_big, wts["norm2_g"], wts["w_up"], wts["conv_w"], wts["conv_b"], p1, p2, tn, per_row)
    if per_row:
        new_conv = tail.reshape(bsz, t_len, dff)[:, t_len - (CONV_W - 1):]
    else:
        tps = t_len // tm_big
        new_conv = tail.reshape(bsz, tps, SUBLANES, dff)[:, -1, SUBLANES - (CONV_W - 1):]
    y = _down(f, wts["w_down"], x1, rows, wts["normf_g"], _pick(dff, 1408, LANES)).reshape(bsz, t_len, d)
    return y, k_out, v_out, wkv, new_shift, new_conv


def kernel(x_prompt, x_sample, cache_k, cache_v, state_wkv, state_shift, state_conv, page_table, c_prompt, c_sample, w_ada, b_ada, norm1_g, w_in, mu_shift, rwkv_w0, rwkv_w2, rwkv_a0, rwkv_a2, rwkv_g2, rwkv_kk, rwkv_ka, rwkv_rk, lnx_g, lnx_b, w_proj_a, w_proj_b, w_out, norm2_g, w_up, conv_w, conv_b, w_down, normf_g):
    d = x_prompt.shape[-1]
    da, db = w_proj_a.shape[0], w_proj_b.shape[0]
    za = mu_shift.shape[0]
    dff = conv_w.shape[1]
    lora_g = rwkv_g2.shape[0]
    assert rwkv_w2.shape[0] == LORA_W and rwkv_a2.shape[0] == LORA_A and LORA_W + LORA_A == LANES
    assert za == 3 * da + LORA_W + LORA_A + lora_g and da % LANES == 0 and db % HEAD_B == 0
    tn = 512 if all(v % 512 == 0 for v in (db, d, dff)) else LANES
    zap = _round_up(za, tn)
    gw = zap - 3 * da - LANES
    assert gw >= lora_g

    w_in_p = jnp.concatenate([w_in[:, :za].astype(BF16), jnp.zeros((d, zap - za), BF16),
                              w_in[:, za:].astype(BF16)], axis=1)
    zeros_l = jnp.zeros((LORA_W, da), F32)
    vecs = jnp.stack([rwkv_w0, rwkv_a0, rwkv_kk, rwkv_ka, rwkv_rk.reshape(da), lnx_g, lnx_b,
                      jnp.zeros((da,), F32)])
    wts = {
        "layout": dict(tn=tn, za=za, zap=zap, da=da, db=db, dff=dff),
        "norm1_g": norm1_g, "norm2_g": norm2_g, "normf_g": normf_g,
        "w_in": w_in_p,
        "mu": jnp.pad(mu_shift, (0, zap - za)).reshape(1, zap),
        "vecs": vecs,
        "w2p": jnp.concatenate([rwkv_w2, zeros_l], axis=0),
        "a2p": jnp.concatenate([zeros_l, rwkv_a2], axis=0),
        "g2p": jnp.pad(rwkv_g2, ((0, gw - lora_g), (0, 0))),
        "w_proj_a": w_proj_a.astype(BF16), "w_proj_b": w_proj_b.astype(BF16),
        "w_out": w_out.astype(BF16), "w_up": w_up.astype(BF16), "w_down": w_down.astype(BF16),
        "conv_w": conv_w, "conv_b": conv_b,
    }

    n_p, t_p = x_prompt.shape[:2]
    n_s, t_s = x_sample.shape[:2]
    n_c = _round_up(n_p + n_s, SUBLANES)
    c_all = jnp.concatenate([c_prompt, c_sample, jnp.zeros((n_c - n_p - n_s, d), F32)], axis=0)
    mod = _ada(c_all, w_ada, b_ada)

    n_heads_a = da // HEAD_A
    out_p = _group(x_prompt, mod[:n_p], jnp.arange(t_p, dtype=jnp.int32), wts,
                   jnp.zeros((n_p, za), F32), jnp.zeros((n_p, n_heads_a, HEAD_A, HEAD_A), F32),
                   None, None, 512)
    past_len = page_table.shape[1] * PAGE_SIZE
    out_s = _group(x_sample, mod[n_p:n_p + n_s], past_len + jnp.arange(t_s, dtype=jnp.int32), wts,
                   state_shift, state_wkv, state_conv, (cache_k, cache_v, page_table), 512)
    y_p, k_p, v_p, wkv_p, sh_p, cv_p = out_p
    y_s, k_s, v_s, wkv_s, sh_s, cv_s = out_s
    return (y_p, y_s, k_p, v_p, wkv_p, sh_p, cv_p, k_s, v_s, wkv_s, sh_s, cv_s)
```

```python
import functools

import jax
import jax.numpy as jnp
from jax import lax
from jax.experimental import pallas as pl
from jax.experimental.pallas import tpu as pltpu

F32 = jnp.float32
BF16 = jnp.bfloat16

HEAD_A = 64
HEAD_B = 128
MOBA_BLOCK = 256
MOBA_TOPK = 3
PAGE_SIZE = 128
ROPE_THETA = 10000.0
LNX_EPS = 64e-5
RMS_EPS = 1e-6
NEG_INF = -1e30
CONV_W = 3
LORA_W = 64
LORA_A = 64

LANES = 128
SUBLANES = 8
VMEM_LIMIT = 52 * 1024 * 1024


def _round_up(x, m):
    return (x + m - 1) // m * m


def _pick(n, target, align):
    if n <= target:
        return n
    t = target - target % align
    while t >= align:
        if n % t == 0:
            return t
        t -= align
    return n


def _params(sem):
    return pltpu.CompilerParams(dimension_semantics=sem, vmem_limit_bytes=VMEM_LIMIT)


def _split3(x):
    hi = x.astype(BF16)
    r1 = x - hi.astype(F32)
    mid = r1.astype(BF16)
    lo = (r1 - mid.astype(F32)).astype(BF16)
    return hi, mid, lo


def _dg(a, b, dims):
    return lax.dot_general(a, b, (dims, ((), ())), preferred_element_type=F32)


_NN = ((1,), (0,))
_NT = ((1,), (1,))
_TN = ((0,), (0,))


def _dot_pieces(a_pieces, b_pieces, dims):
    ca, cb = dims[0][0], dims[1][0]
    if a_pieces[0].shape[ca] % LANES == 0:
        return _dg(jnp.concatenate(a_pieces, axis=ca), jnp.concatenate(b_pieces, axis=cb), dims)
    out = _dg(a_pieces[0], b_pieces[0], dims)
    for x, y in zip(a_pieces[1:], b_pieces[1:]):
        out = out + _dg(x, y, dims)
    return out


def _dotx(a, b, dims=_NN):
    ah, am, _ = _split3(a)
    bh, bm, _ = _split3(b)
    return _dot_pieces([ah, ah, am], [bh, bm, bh], dims)


def _dot1(a, b, dims=_NN):
    return _dg(a.astype(BF16), b.astype(BF16), dims)


def _dot01(a01, b, dims=_NN):
    return _dot_pieces([a01, a01, a01], list(_split3(b)), dims)


def _dotr01(a, b01, dims=_NN):
    return _dot_pieces(list(_split3(a)), [b01, b01, b01], dims)


def _sigmoid(x):
    return 1.0 / (1.0 + jnp.exp(-x))


def _softplus(x):
    return jnp.maximum(x, 0.0) + jnp.log(1.0 + jnp.exp(-jnp.abs(x)))


def _norm_mod(x, g, shift, scale):
    ms = jnp.mean(x * x, axis=-1, keepdims=True)
    y = x * lax.rsqrt(ms + RMS_EPS)
    return (y * g) * (1.0 + scale) + shift


def _mod_rows(mod_ref, idx, lo, n):
    if mod_ref.shape[1] == 1:
        return mod_ref[idx]
    return mod_ref[idx, lo:lo + n, :]


def _ada_kernel(c_ref, w_ref, b_ref, o_ref):
    c = c_ref[...]
    s = c * _sigmoid(c)
    o_ref[...] = _dotx(s, w_ref[...]) + b_ref[...]


def _ada(c_all, w_ada, b_ada):
    n, d = c_all.shape
    n6 = w_ada.shape[1]
    tn = _pick(n6, 512, LANES)
    return pl.pallas_call(
        _ada_kernel,
        grid=(n6 // tn,),
        in_specs=[pl.BlockSpec((n, d), lambda j: (0, 0)),
                  pl.BlockSpec((d, tn), lambda j: (0, j)),
                  pl.BlockSpec((1, tn), lambda j: (0, j))],
        out_specs=pl.BlockSpec((n, tn), lambda j: (0, j)),
        out_shape=jax.ShapeDtypeStruct((n, n6), F32),
        compiler_params=_params(("arbitrary",)),
        name="ada",
    )(c_all, w_ada, b_ada.reshape(1, n6))


class _Rows:
    def __init__(self, n_rows, seq_len, tm, mod):
        self.n_rows, self.seq_len, self.tm, self.mod = n_rows, seq_len, tm, mod
        self.per_row = mod.shape[2] != 1
        if not self.per_row:
            assert seq_len % tm == 0
        else:
            assert tm == n_rows
        self.n_tiles = n_rows // tm
        self.tiles_per_seq = max(seq_len // tm, 1)

    def mod_spec(self, width, col_of):
        r = self.tm if self.per_row else 1
        tps = self.tiles_per_seq
        if self.per_row:
            return pl.BlockSpec((None, 6, r, width), lambda i, j: (0, 0, i, col_of(i, j)))
        return pl.BlockSpec((None, 6, r, width), lambda i, j: (i // tps, 0, 0, col_of(i, j)))


def _in_kernel(x_ref, mod_ref, g_ref, wa_ref, wb_ref, cos_ref, sin_ref, o_ref, k4_ref, v4_ref, h_scr,
               *, sub, q_lo, n_qkv):
    j = pl.program_id(1)
    tm = x_ref.shape[0]

    @pl.when(j == 0)
    def _():
        for s in range(tm // sub):
            lo = s * sub
            h = _norm_mod(x_ref[lo:lo + sub, :], g_ref[...],
                          _mod_rows(mod_ref, 0, lo, sub), _mod_rows(mod_ref, 1, lo, sub))
            h_scr[lo:lo + sub, :] = h.astype(BF16)

    @pl.when(j < q_lo)
    def _():
        o_ref[...] = jnp.dot(h_scr[...], wa_ref[...].astype(BF16), preferred_element_type=F32)

    heads_per_tile = wb_ref.shape[1] // HEAD_B

    def emit(rope, head_ref, tile):
        acc = jnp.dot(h_scr[...], wb_ref[...], preferred_element_type=F32)
        for c in range(heads_per_tile):
            a = acc[:, c * HEAD_B:(c + 1) * HEAD_B]
            if rope:
                a = a * cos_ref[...] + pltpu.roll(a, HEAD_B // 2, 1) * sin_ref[...]
            o_ref[:, c * HEAD_B:(c + 1) * HEAD_B] = a
            if head_ref is not None:
                head_ref[:, tile * heads_per_tile + c, :] = a

    for t in range(n_qkv):
        pl.when(j == q_lo + t)(functools.partial(emit, True, None, t))
        pl.when(j == q_lo + n_qkv + t)(functools.partial(emit, True, k4_ref, t))
        pl.when(j == q_lo + 2 * n_qkv + t)(functools.partial(emit, False, v4_ref, t))

    @pl.when(j >= q_lo + 3 * n_qkv)
    def _():
        o_ref[...] = jnp.dot(h_scr[...], wb_ref[...], preferred_element_type=F32)


def _in_proj(x2, rows, norm_g, w_f32, wb_bf, cos_t, sin_t, tn, q_lo, n_qkv):
    n, d = x2.shape
    zp = q_lo * tn + wb_bf.shape[1]
    tm = rows.tm
    n_tab = cos_t.shape[0] // tm
    sub = _pick(tm, 128, SUBLANES)
    n_heads = n_qkv * tn // HEAD_B
    head_spec = pl.BlockSpec((tm, n_heads, HEAD_B), lambda i, j: (i, 0, 0))
    head_shape = jax.ShapeDtypeStruct((n, n_heads, HEAD_B), F32)
    return pl.pallas_call(
        functools.partial(_in_kernel, sub=sub, q_lo=q_lo, n_qkv=n_qkv),
        grid=(n // tm, zp // tn),
        in_specs=[pl.BlockSpec((tm, d), lambda i, j: (i, 0), pipeline_mode=pl.Buffered(1)),
                  rows.mod_spec(d, lambda i, j: 0),
                  pl.BlockSpec((1, d), lambda i, j: (0, 0)),
                  pl.BlockSpec((d, tn), lambda i, j: (0, jnp.minimum(j, q_lo - 1))),
                  pl.BlockSpec((d, tn), lambda i, j: (0, jnp.maximum(j - q_lo, 0))),
                  pl.BlockSpec((tm, HEAD_B), lambda i, j: (i % n_tab, 0)),
                  pl.BlockSpec((tm, HEAD_B), lambda i, j: (i % n_tab, 0))],
        out_specs=[pl.BlockSpec((tm, tn), lambda i, j: (i, j)), head_spec, head_spec],
        out_shape=[jax.ShapeDtypeStruct((n, zp), F32), head_shape, head_shape],
        scratch_shapes=[pltpu.VMEM((tm, d), BF16)],
        compiler_params=_params(("arbitrary", "arbitrary")),
        name="in_proj",
    )(x2, rows.mod, norm_g.reshape(1, d), w_f32, wb_bf, cos_t, sin_t)


def _unit_lower_inverse(low, n, c, blk):
    rr = lax.broadcasted_iota(jnp.int32, (n, n), 0)
    cc = lax.broadcasted_iota(jnp.int32, (n, n), 1)
    eye = (rr == cc).astype(F32)
    same = (rr // blk) == (cc // blk)
    dpart = [jnp.where(same, x, 0.0) for x in low]
    inv = [eye - d for d in dpart]
    p = dpart
    k = 2
    while k < blk:
        p = [_dot1(x, x) for x in p]
        inv = [i + _dot1(i, x) for i, x in zip(inv, p)]
        k *= 2
    if c > blk:
        m = [_dot1(i, x - d) for i, x, d in zip(inv, low, dpart)]
        minv = [eye - x for x in m]
        p = m
        k = 2
        while k < c // blk:
            p = [_dot1(x, x) for x in p]
            minv = [i + _dot1(i, x) for i, x in zip(minv, p)]
            k *= 2
        inv = [_dot1(mi, i) for mi, i in zip(minv, inv)]
    return inv


def _rwkv_kernel(z_ref, shift0_ref, s0_ref, mu_ref, vec_ref, w2_ref, a2_ref, g2_ref,
                 yag_ref, sout_ref, state_scr, prev_scr, *, c_len, t_valid, da, n_chunks):
    ci = pl.program_id(1)
    n_pairs = da // LANES
    c2 = 2 * c_len

    @pl.when(ci == 0)
    def _():
        zero = jnp.zeros((HEAD_A, HEAD_A), F32)
        for p in range(n_pairs):
            top = jnp.concatenate([s0_ref[2 * p], zero], axis=1)
            bot = jnp.concatenate([zero, s0_ref[2 * p + 1]], axis=1)
            state_scr[p] = jnp.concatenate([top, bot], axis=0)
        prev_scr[0:1, :] = shift0_ref[...]

    za = z_ref[...]
    row = lax.broadcasted_iota(jnp.int32, (c_len, 1), 0)
    zprev = jnp.where(row == 0, prev_scr[0:1, :], pltpu.roll(za, 1, 0))
    prev_scr[0:1, :] = za[c_len - 1:c_len, :]
    zmix = za + (zprev - za) * mu_ref[...]

    x_wa = zmix[:, 3 * da:3 * da + LANES]
    lane = lax.broadcasted_iota(jnp.int32, (c_len, LANES), 1)
    act_wa = jnp.where(lane < LORA_W, jnp.tanh(x_wa), x_wa)
    sig_g = _sigmoid(zmix[:, 3 * da + LANES:])
    act_wa = act_wa.astype(BF16)
    lw = _dg(act_wa, w2_ref[...], _NN)
    la = _dg(act_wa, a2_ref[...], _NN)
    g = _dg(sig_g.astype(BF16), g2_ref[...], _NN)

    valid = row < t_valid
    head_lo = lane < HEAD_A
    ones_seg = ((lax.broadcasted_iota(jnp.int32, (LANES, LANES), 0) // HEAD_A)
                == (lax.broadcasted_iota(jnp.int32, (LANES, LANES), 1) // HEAD_A)).astype(BF16)
    tr = lax.broadcasted_iota(jnp.int32, (c_len, c_len), 0)
    tc = lax.broadcasted_iota(jnp.int32, (c_len, c_len), 1)
    tri = (tc <= tr).astype(BF16)
    rr = lax.broadcasted_iota(jnp.int32, (c2, c2), 0)
    cc = lax.broadcasted_iota(jnp.int32, (c2, c2), 1)
    same_head = (rr // c_len) == (cc // c_len)
    strict = jnp.logical_and(same_head, (cc % c_len) < (rr % c_len))
    incl = jnp.logical_and(same_head, (cc % c_len) <= (rr % c_len))
    pairs = range(n_pairs)

    def lanes(x, p):
        return x[:, p * LANES:(p + 1) * LANES]

    def head_sums(x):
        xs = jnp.concatenate([lanes(x, p) for p in pairs], axis=0)
        s = _dotr01(xs, ones_seg)
        return jnp.concatenate([s[p * c_len:(p + 1) * c_len] for p in pairs], axis=1)

    def stack(x, p):
        xp = lanes(x, p)
        return jnp.concatenate([jnp.where(head_lo, xp, 0.0), jnp.where(head_lo, 0.0, xp)], axis=0)

    r = zmix[:, 0:da]
    ka = zmix[:, da:2 * da]
    v = zmix[:, 2 * da:3 * da]
    w0, a0, kkw, kaw, rkw, lng, lnb = (vec_ref[i:i + 1, :] for i in range(7))
    w_log = -_softplus(-(w0 + lw)) - 0.5
    logw = jnp.where(valid, -jnp.exp(w_log), 0.0)
    a = _sigmoid(a0 + la)
    kk = ka * kkw
    kk = kk * lax.rsqrt(jnp.maximum(head_sums(kk * kk), 1e-24))
    k_mod = ka * (1.0 + (a - 1.0) * kaw)
    bonus = head_sums(r * k_mod * rkw) * v
    b = jnp.where(valid, kk * a, 0.0)
    k_s = jnp.where(valid, k_mod, 0.0)

    cum = _dot01(tri, logw)
    cum_end = cum[c_len - 1:c_len, :]
    e_neg = jnp.exp(-cum)
    e_end = jnp.exp(cum_end - cum)
    gam_end = jnp.exp(cum_end)
    kt = kk * jnp.exp(cum - logw)
    rt = r * jnp.exp(cum)
    kd = k_s * e_neg
    bd = b * e_neg
    ke = k_s * e_end
    be = b * e_end

    xs = [jnp.concatenate([stack(kt, p), stack(rt, p)], axis=0) for p in pairs]
    ws = [jnp.concatenate([stack(kd, p), stack(bd, p)], axis=0) for p in pairs]
    es = [jnp.concatenate([stack(ke, p), stack(be, p)], axis=0) for p in pairs]
    vs = [stack(v, p) for p in pairs]
    s0 = [state_scr[p] for p in pairs]
    gm = [_dot1(x, w, _NT) for x, w in zip(xs, ws)]
    xs0 = [_dot1(x, s, _NT) for x, s in zip(xs, s0)]
    l_k = [jnp.where(strict, g_[:c2, :c2], 0.0) for g_ in gm]
    l_b = [jnp.where(strict, g_[:c2, c2:], 0.0) for g_ in gm]
    a_kb = [jnp.concatenate([jnp.where(incl, g_[c2:, :c2], 0.0), jnp.where(incl, -g_[c2:, c2:], 0.0)], axis=1)
            for g_ in gm]
    tinv = _unit_lower_inverse(l_b, c2, c_len, min(16, c_len))
    rhs = [x0[:c2] + _dot1(lk, v_) for x0, lk, v_ in zip(xs0, l_k, vs)]
    u = [_dot1(t, r_) for t, r_ in zip(tinv, rhs)]
    res = [r_ - (u_ + _dotx(lb, u_)) for r_, u_, lb in zip(rhs, u, l_b)]
    u = [u_ + _dot1(t, e_) for u_, t, e_ in zip(u, tinv, res)]
    y2 = [x0[c2:] + _dot1(ab, jnp.concatenate([v_, u_], axis=0)) for x0, ab, v_, u_ in zip(xs0, a_kb, vs, u)]
    for p in pairs:
        upd = _dot1(jnp.concatenate([vs[p], -u[p]], axis=0), es[p], _TN)
        state_scr[p] = s0[p] * lanes(gam_end, p) + upd
    y = jnp.concatenate([y_[:c_len] + y_[c_len:] for y_ in y2], axis=1)

    yc = y - head_sums(y) * (1.0 / HEAD_A)
    var = head_sums(yc * yc) * (1.0 / HEAD_A)
    ya = yc * lax.rsqrt(var + LNX_EPS) * lng + lnb + bonus
    yag_ref[...] = (ya * g).astype(BF16)

    @pl.when(ci == n_chunks - 1)
    def _():
        for p in range(n_pairs):
            s_pair = state_scr[p]
            sout_ref[2 * p] = s_pair[:HEAD_A, :HEAD_A]
            sout_ref[2 * p + 1] = s_pair[HEAD_A:, HEAD_A:]


def _rwkv(z3, zap, shift0, s0, mu_p, vecs, w2p, a2p, g2p, c_len, t_valid, da):
    bsz, tp, _ = z3.shape
    n_chunks = tp // c_len
    n_pairs = da // LANES
    n_heads = da // HEAD_A
    gw = zap - 3 * da - LANES
    kern = functools.partial(_rwkv_kernel, c_len=c_len, t_valid=t_valid, da=da, n_chunks=n_chunks)
    state_spec = pl.BlockSpec((None, n_heads, HEAD_A, HEAD_A), lambda b, c: (b, 0, 0, 0))
    return pl.pallas_call(
        kern,
        grid=(bsz, n_chunks),
        in_specs=[pl.BlockSpec((None, c_len, zap), lambda b, c: (b, c, 0)),
                  pl.BlockSpec((None, 1, zap), lambda b, c: (b, 0, 0)),
                  state_spec,
                  pl.BlockSpec((1, zap), lambda b, c: (0, 0)),
                  pl.BlockSpec((SUBLANES, da), lambda b, c: (0, 0)),
                  pl.BlockSpec((LANES, da), lambda b, c: (0, 0)),
                  pl.BlockSpec((LANES, da), lambda b, c: (0, 0)),
                  pl.BlockSpec((gw, da), lambda b, c: (0, 0))],
        out_specs=[pl.BlockSpec((None, c_len, da), lambda b, c: (b, c, 0)), state_spec],
        out_shape=[jax.ShapeDtypeStruct((bsz, tp, da), BF16),
                   jax.ShapeDtypeStruct((bsz, n_heads, HEAD_A, HEAD_A), F32)],
        scratch_shapes=[pltpu.VMEM((n_pairs, LANES, LANES), F32),
                        pltpu.VMEM((SUBLANES, zap), F32)],
        compiler_params=_params(("arbitrary", "arbitrary")),
        name="rwkv",
    )(z3, shift0, s0, mu_p, vecs, w2p, a2p, g2p)


def _topk_mask(gate, valid, n_cand, axis):
    gm = jnp.where(valid, gate, NEG_INF)
    idx = lax.broadcasted_iota(jnp.int32, gate.shape, axis)
    cnt = jnp.zeros(gate.shape, jnp.int32)
    for m in range(n_cand):
        g_m = gm[:, m:m + 1] if axis == 1 else gm[m:m + 1, :]
        ahead = jnp.logical_or(g_m > gm, jnp.logical_and(g_m == gm, m < idx))
        cnt = cnt + ahead.astype(jnp.int32)
    return jnp.logical_and(valid, cnt < MOBA_TOPK)


def _attn_prompt_kernel(q_ref, k_ref, v_ref, o_ref, kmean_scr, *, n_blk):
    i = pl.program_id(2)
    blk = MOBA_BLOCK
    scale = HEAD_B ** -0.5

    @pl.when(i == 0)
    def _():
        kmean_scr[...] = jnp.zeros_like(kmean_scr)
        for n in range(n_blk):
            kmean_scr[n:n + 1, :] = jnp.sum(k_ref[n * blk:(n + 1) * blk, :], axis=0, keepdims=True) * (1.0 / blk)

    q = q_ref[...]
    qb = q.astype(BF16)
    gate = _dotx(q, kmean_scr[...], _NT)
    lane = lax.broadcasted_iota(jnp.int32, gate.shape, 1)
    sel = _topk_mask(gate, lane < i, n_blk, 1).astype(F32)
    qi = lax.broadcasted_iota(jnp.int32, (blk, blk), 0)
    ki = lax.broadcasted_iota(jnp.int32, (blk, blk), 1)

    def attend(own):
        width = (own + 1) * blk
        s = _dg(qb, k_ref[0:width, :].astype(BF16), _NT) * scale
        parts = [jnp.where(sel[:, n:n + 1] > 0.0, s[:, n * blk:(n + 1) * blk], NEG_INF) for n in range(own)]
        parts.append(jnp.where(ki <= qi, s[:, own * blk:], NEG_INF))
        m = functools.reduce(jnp.maximum, [jnp.max(x, axis=1, keepdims=True) for x in parts])
        probs = [jnp.exp(x - m) for x in parts]
        l = functools.reduce(lambda a, b: a + b, [jnp.sum(x, axis=1, keepdims=True) for x in probs])
        pcat = jnp.concatenate([x.astype(BF16) for x in probs], axis=1)
        acc = _dg(pcat, v_ref[0:width, :].astype(BF16), _NN)
        o_ref[...] = (acc / l).astype(BF16)

    for own in range(n_blk):
        pl.when(i == own)(functools.partial(attend, own))


def _attn_prompt(z3, q_col, k_col, v_col, n_heads):
    bsz, t_len, _ = z3.shape
    n_blk = t_len // MOBA_BLOCK
    kern = functools.partial(_attn_prompt_kernel, n_blk=n_blk)
    return pl.pallas_call(
        kern,
        grid=(bsz, n_heads, n_blk),
        in_specs=[pl.BlockSpec((None, MOBA_BLOCK, HEAD_B), lambda b, h, i: (b, i, q_col + h)),
                  pl.BlockSpec((None, t_len, HEAD_B), lambda b, h, i: (b, 0, k_col + h)),
                  pl.BlockSpec((None, t_len, HEAD_B), lambda b, h, i: (b, 0, v_col + h))],
        out_specs=pl.BlockSpec((None, MOBA_BLOCK, HEAD_B), lambda b, h, i: (b, i, h)),
        out_shape=jax.ShapeDtypeStruct((bsz, t_len, n_heads * HEAD_B), BF16),
        scratch_shapes=[pltpu.VMEM((LANES, HEAD_B), F32)],
        compiler_params=_params(("arbitrary", "arbitrary", "arbitrary")),
        name="attn_prompt",
    )(z3, z3, z3)


def _attn_sample_kernel(pt_ref, q_ref, knew_ref, vnew_ref, *refs, n_pages, t_new, n_heads, grp):
    del pt_ref
    ck_refs, cv_refs = refs[:grp], refs[grp:2 * grp]
    o_ref, newk_scr, newv_scr, ksum_scr, sc_scr, sel_scr, acc_scr, l_scr = refs[2 * grp:]
    n_grp = n_pages // grp
    j = pl.program_id(1)
    n_past_blk = n_pages * PAGE_SIZE // MOBA_BLOCK
    pages_per_blk = MOBA_BLOCK // PAGE_SIZE
    n_col = t_new * n_heads
    n_key = PAGE_SIZE * n_heads
    scale = HEAD_B ** -0.5
    ci = lax.broadcasted_iota(jnp.int32, (n_col, n_key), 0)
    li = lax.broadcasted_iota(jnp.int32, (n_col, n_key), 1)
    diag = (li % n_heads) == (ci % n_heads)
    lane = lax.broadcasted_iota(jnp.int32, (n_col, LANES), 1)

    @pl.when(j == 0)
    def _():
        newk_scr[...] = jnp.zeros_like(newk_scr)
        newv_scr[...] = jnp.zeros_like(newv_scr)
        newk_scr[0:t_new] = knew_ref[...]
        newv_scr[0:t_new] = vnew_ref[...]
        ksum_scr[...] = jnp.zeros_like(ksum_scr)
        acc_scr[...] = jnp.zeros_like(acc_scr)

    def score_page(kpage, page):
        k2 = kpage.reshape(n_key, HEAD_B).astype(BF16)
        sc_scr[page] = _dg(q_ref[...].astype(BF16), k2, _NT) * scale

    @pl.when(j < n_grp)
    def _():
        for g in range(grp):
            kpage = ck_refs[g][...]
            page = j * grp + g
            score_page(kpage, page)
            blk = page // pages_per_blk
            ksum_scr[blk] = ksum_scr[blk] + jnp.sum(kpage, axis=0)

    def block_selected(pg):
        blk = pg // pages_per_blk
        return jnp.max(jnp.where(lane == blk, sel_scr[...], 0.0), axis=1, keepdims=True) > 0.0

    @pl.when(j == n_grp - 1)
    def _():
        score_page(newk_scr[...], n_pages)
        n_bh = n_past_blk * n_heads
        kmean = ksum_scr[...].reshape(n_bh, HEAD_B) * (1.0 / MOBA_BLOCK)
        gt = _dotx(q_ref[...], kmean, _NT)
        gci = lax.broadcasted_iota(jnp.int32, (n_col, n_bh), 0)
        gli = lax.broadcasted_iota(jnp.int32, (n_col, n_bh), 1)
        gm = jnp.where((gli % n_heads) == (gci % n_heads), gt, 0.0)
        pick = ((lax.broadcasted_iota(jnp.int32, (n_bh, LANES), 0) // n_heads)
                == lax.broadcasted_iota(jnp.int32, (n_bh, LANES), 1)).astype(BF16)
        gate = _dotr01(gm, pick)
        sel_scr[...] = _topk_mask(gate, lane < n_past_blk, n_past_blk, 1).astype(F32)

        key_row = li // n_heads
        new_ok = jnp.logical_and(diag, jnp.logical_and(key_row < t_new, key_row <= ci // n_heads))
        s_new = sc_scr[n_pages]

        def pg_max(pg, mm):
            ok = jnp.logical_and(block_selected(pg), diag)
            return jnp.maximum(mm, jnp.where(ok, sc_scr[pg], NEG_INF))

        mm = lax.fori_loop(0, n_pages, pg_max, jnp.where(new_ok, s_new, NEG_INF), unroll=grp)
        m = jnp.max(mm, axis=1, keepdims=True)
        p_new = jnp.where(new_ok, jnp.exp(s_new - m), 0.0)
        sc_scr[n_pages] = p_new

        def pg_exp(pg, ll):
            ok = jnp.logical_and(block_selected(pg), diag)
            pr = jnp.where(ok, jnp.exp(sc_scr[pg] - m), 0.0)
            sc_scr[pg] = pr
            return ll + pr

        ll = lax.fori_loop(0, n_pages, pg_exp, p_new, unroll=grp)
        l_scr[...] = jnp.broadcast_to(jnp.sum(ll, axis=1, keepdims=True), l_scr.shape)

    def pv_page(vpage, page):
        v2 = vpage.reshape(n_key, HEAD_B).astype(BF16)
        acc_scr[...] = acc_scr[...] + _dg(sc_scr[page].astype(BF16), v2, _NN)

    @pl.when(j >= n_grp)
    def _():
        for g in range(grp):
            pv_page(cv_refs[g][...], (j - n_grp) * grp + g)

    @pl.when(j == 2 * n_grp - 1)
    def _():
        pv_page(newv_scr[...], n_pages)
        o_ref[...] = (acc_scr[...] / l_scr[...]).astype(BF16)


def _attn_sample(q4, k4, v4, cache_k, cache_v, page_table):
    n_seq, t_new, n_heads, _ = q4.shape
    n_pages = page_table.shape[1]
    n_col = t_new * n_heads
    n_key = PAGE_SIZE * n_heads
    assert (n_pages * PAGE_SIZE) % MOBA_BLOCK == 0 and n_pages * PAGE_SIZE // MOBA_BLOCK <= LANES
    assert n_heads == SUBLANES and t_new <= PAGE_SIZE
    n_past_blk = n_pages * PAGE_SIZE // MOBA_BLOCK
    grp = _pick(n_pages, 8, 1)
    n_grp = n_pages // grp
    kern = functools.partial(_attn_sample_kernel, n_pages=n_pages, t_new=t_new, n_heads=n_heads, grp=grp)
    page_block = (None, PAGE_SIZE, n_heads, HEAD_B)

    def k_spec(g):
        return pl.BlockSpec(page_block, lambda b, j, pt: (pt[b, jnp.minimum(j, n_grp - 1) * grp + g], 0, 0, 0))

    def v_spec(g):
        return pl.BlockSpec(page_block, lambda b, j, pt: (pt[b, jnp.maximum(j - n_grp, 0) * grp + g], 0, 0, 0))

    grid_spec = pltpu.PrefetchScalarGridSpec(
        num_scalar_prefetch=1,
        grid=(n_seq, 2 * n_grp),
        in_specs=[pl.BlockSpec((None, n_col, HEAD_B), lambda b, j, pt: (b, 0, 0)),
                  pl.BlockSpec((None, t_new, n_heads, HEAD_B), lambda b, j, pt: (b, 0, 0, 0)),
                  pl.BlockSpec((None, t_new, n_heads, HEAD_B), lambda b, j, pt: (b, 0, 0, 0))]
                 + [k_spec(g) for g in range(grp)] + [v_spec(g) for g in range(grp)],
        out_specs=pl.BlockSpec((None, n_col, HEAD_B), lambda b, j, pt: (b, 0, 0)),
        scratch_shapes=[pltpu.VMEM((PAGE_SIZE, n_heads, HEAD_B), F32),
                        pltpu.VMEM((PAGE_SIZE, n_heads, HEAD_B), F32),
                        pltpu.VMEM((n_past_blk, n_heads, HEAD_B), F32),
                        pltpu.VMEM((n_pages + 1, n_col, n_key), F32),
                        pltpu.VMEM((n_col, LANES), F32),
                        pltpu.VMEM((n_col, HEAD_B), F32),
                        pltpu.VMEM((n_col, HEAD_B), F32)])
    return pl.pallas_call(
        kern,
        grid_spec=grid_spec,
        out_shape=jax.ShapeDtypeStruct((n_seq, n_col, HEAD_B), BF16),
        compiler_params=_params(("arbitrary", "arbitrary")),
        name="attn_sample",
    )(page_table, q4.reshape(n_seq, n_col, HEAD_B), k4, v4, *([cache_k] * grp), *([cache_v] * grp))


def _mix_kernel(ya_ref, ob_ref, wa_ref, wb_ref, ga_ref, gb_ref, o_ref):
    oa = jnp.dot(ya_ref[...], wa_ref[...], preferred_element_type=F32)
    ob = jnp.dot(ob_ref[...], wb_ref[...], preferred_element_type=F32)
    o_ref[...] = (_sigmoid(ga_ref[...]) * oa + _sigmoid(gb_ref[...]) * ob).astype(BF16)


def _mix(yag, attn, wa_bf, wb_bf, z, ga_col, gb_col, tm, tn):
    n, da = yag.shape
    db = attn.shape[1]
    d = wa_bf.shape[1]
    return pl.pallas_call(
        _mix_kernel,
        grid=(n // tm, d // tn),
        in_specs=[pl.BlockSpec((tm, da), lambda i, j: (i, 0)),
                  pl.BlockSpec((tm, db), lambda i, j: (i, 0)),
                  pl.BlockSpec((da, tn), lambda i, j: (0, j)),
                  pl.BlockSpec((db, tn), lambda i, j: (0, j)),
                  pl.BlockSpec((tm, tn), lambda i, j: (i, ga_col + j)),
                  pl.BlockSpec((tm, tn), lambda i, j: (i, gb_col + j))],
        out_specs=pl.BlockSpec((tm, tn), lambda i, j: (i, j)),
        out_shape=jax.ShapeDtypeStruct((n, d), BF16),
        compiler_params=_params(("arbitrary", "arbitrary")),
        name="mix",
    )(yag, attn, wa_bf, wb_bf, z, z)


def _out_kernel(x_ref, mix_ref, w_ref, mod_ref, o_ref):
    acc = jnp.dot(mix_ref[...], w_ref[...], preferred_element_type=F32)
    o_ref[...] = x_ref[...] + _mod_rows(mod_ref, 2, 0, x_ref.shape[0]) * acc


def _out_proj(x2, mix, w_bf, rows, tn):
    n, d = x2.shape
    tm = rows.tm
    return pl.pallas_call(
        _out_kernel,
        grid=(n // tm, d // tn),
        in_specs=[pl.BlockSpec((tm, tn), lambda i, j: (i, j)),
                  pl.BlockSpec((tm, d), lambda i, j: (i, 0)),
                  pl.BlockSpec((d, tn), lambda i, j: (0, j)),
                  rows.mod_spec(tn, lambda i, j: j)],
        out_specs=pl.BlockSpec((tm, tn), lambda i, j: (i, j)),
        out_shape=jax.ShapeDtypeStruct((n, d), F32),
        compiler_params=_params(("arbitrary", "arbitrary")),
        name="out_proj",
    )(x2, mix, w_bf, rows.mod)


def _gelu_tanh(x):
    return 0.5 * x * (1.0 + jnp.tanh(0.7978845608028654 * (x + 0.044715 * (x * x * x))))


def _up_kernel(x_ref, mod_ref, g_ref, wg_ref, wv_ref, cw_ref, cb_ref, p1_ref, p2_ref,
               f_ref, tail_ref, h_scr, carry_scr, *, sub, seq_len):
    i = pl.program_id(0)
    j = pl.program_id(1)
    tm = x_ref.shape[0]

    @pl.when(j == 0)
    def _():
        for s in range(tm // sub):
            lo = s * sub
            h = _norm_mod(x_ref[lo:lo + sub, :], g_ref[...],
                          _mod_rows(mod_ref, 3, lo, sub), _mod_rows(mod_ref, 4, lo, sub))
            h_scr[lo:lo + sub, :] = h.astype(BF16)

    @pl.when(i == 0)
    def _():
        carry_scr[j] = jnp.zeros(carry_scr.shape[1:], F32)

    hb = h_scr[...]
    ug = jnp.dot(hb, wg_ref[...], preferred_element_type=F32)
    uv = jnp.dot(hb, wv_ref[...], preferred_element_type=F32)
    row = lax.broadcasted_iota(jnp.int32, (tm, 1), 0)
    pos = (i * tm + row) % seq_len
    c0 = carry_scr[j, 0:1, :]
    c1 = carry_scr[j, 1:2, :]
    s1 = jnp.where(row == 0, c1, pltpu.roll(ug, 1, 0))
    s2 = jnp.where(row == 0, c0, jnp.where(row == 1, c1, pltpu.roll(ug, 2, 0)))
    s1 = jnp.where(pos < 1, p1_ref[...], s1)
    s2 = jnp.where(pos < 2, p2_ref[...], s2)
    carry_scr[j, 0:2, :] = ug[tm - 2:tm, :]
    conv = cb_ref[...] + s2 * cw_ref[0:1, :] + s1 * cw_ref[1:2, :] + ug * cw_ref[2:3, :]
    f_ref[...] = (_gelu_tanh(conv) * uv).astype(BF16)
    if tail_ref.shape[0] == tm:
        tail_ref[...] = ug
    else:
        tail_ref[...] = ug[tm - SUBLANES:tm, :]


def _up(x1, rows, norm_g, wup_bf, conv_w, conv_b, p1, p2, tn, full_tail):
    n, d = x1.shape
    dff = conv_w.shape[1]
    tm = rows.tm
    ncol = dff // tn
    sub = _pick(tm, 128, SUBLANES)
    pr = p1.shape[0]
    tail_rows = tm if full_tail else SUBLANES
    prev_spec = (pl.BlockSpec((tm, tn), lambda i, j: (i, j)) if pr != 1
                 else pl.BlockSpec((1, tn), lambda i, j: (0, j)))
    kern = functools.partial(_up_kernel, sub=sub, seq_len=rows.seq_len)
    return pl.pallas_call(
        kern,
        grid=(n // tm, ncol),
        in_specs=[pl.BlockSpec((tm, d), lambda i, j: (i, 0), pipeline_mode=pl.Buffered(1)),
                  rows.mod_spec(d, lambda i, j: 0),
                  pl.BlockSpec((1, d), lambda i, j: (0, 0)),
                  pl.BlockSpec((d, tn), lambda i, j: (0, j)),
                  pl.BlockSpec((d, tn), lambda i, j: (0, ncol + j)),
                  pl.BlockSpec((CONV_W, tn), lambda i, j: (0, j)),
                  pl.BlockSpec((1, tn), lambda i, j: (0, j)),
                  prev_spec, prev_spec],
        out_specs=[pl.BlockSpec((tm, tn), lambda i, j: (i, j)),
                   pl.BlockSpec((tail_rows, tn), lambda i, j: (i, j))],
        out_shape=[jax.ShapeDtypeStruct((n, dff), BF16),
                   jax.ShapeDtypeStruct((n // tm * tail_rows, dff), F32)],
        scratch_shapes=[pltpu.VMEM((tm, d), BF16),
                        pltpu.VMEM((ncol, SUBLANES, tn), F32)],
        compiler_params=_params(("arbitrary", "arbitrary")),
        name="up_proj",
    )(x1, rows.mod, norm_g.reshape(1, d), wup_bf, wup_bf, conv_w, conv_b.reshape(1, dff), p1, p2)


def _down_kernel(f_ref, w_ref, x_ref, mod_ref, g_ref, o_ref, acc_scr, *, n_k):
    k = pl.program_id(1)

    @pl.when(k == 0)
    def _():
        acc_scr[...] = jnp.zeros_like(acc_scr)

    acc_scr[...] += jnp.dot(f_ref[...], w_ref[...], preferred_element_type=F32)

    @pl.when(k == n_k - 1)
    def _():
        x2 = x_ref[...] + _mod_rows(mod_ref, 5, 0, x_ref.shape[0]) * acc_scr[...]
        ms = jnp.mean(x2 * x2, axis=-1, keepdims=True)
        o_ref[...] = (x2 * lax.rsqrt(ms + RMS_EPS)) * g_ref[...]


def _down(f, wd_bf, x1, rows, normf_g, tk):
    n, dff = f.shape
    d = x1.shape[1]
    tm = rows.tm
    n_k = dff // tk
    return pl.pallas_call(
        functools.partial(_down_kernel, n_k=n_k),
        grid=(n // tm, n_k),
        in_specs=[pl.BlockSpec((tm, tk), lambda i, k: (i, k)),
                  pl.BlockSpec((tk, d), lambda i, k: (k, 0)),
                  pl.BlockSpec((tm, d), lambda i, k: (i, 0)),
                  rows.mod_spec(d, lambda i, k: 0),
                  pl.BlockSpec((1, d), lambda i, k: (0, 0))],
        out_specs=pl.BlockSpec((tm, d), lambda i, k: (i, 0)),
        out_shape=jax.ShapeDtypeStruct((n, d), F32),
        scratch_shapes=[pltpu.VMEM((tm, d), F32)],
        compiler_params=_params(("arbitrary", "arbitrary")),
        name="down_proj",
    )(f, wd_bf, x1, rows.mod, normf_g.reshape(1, d))


def _rope_tables(pos):
    half = HEAD_B // 2
    inv = ROPE_THETA ** (-jnp.arange(half, dtype=F32) / half)
    ang = pos.astype(F32)[:, None] * inv[None, :]
    cos, sin = jnp.cos(ang), jnp.sin(ang)
    return jnp.concatenate([cos, cos], axis=1), jnp.concatenate([-sin, sin], axis=1)


def _group(x, mod, pos, wts, shift0, wkv0, conv_prev, cache, tm_target):
    bsz, t_len, d = x.shape
    n = bsz * t_len
    lay = wts["layout"]
    tn, za, zap, da, db, dff = lay["tn"], lay["za"], lay["zap"], lay["da"], lay["db"], lay["dff"]
    x2 = x.reshape(n, d)
    per_row = cache is not None
    if per_row:
        tm = tm_big = n
        mod4 = jnp.repeat(mod.reshape(bsz, 6, d), t_len, axis=0).swapaxes(0, 1)[None]
        cos_t, sin_t = _rope_tables(jnp.tile(pos, bsz))
    else:
        tm = _pick(t_len, tm_target, SUBLANES)
        tm_big = _pick(t_len, 2 * tm_target, SUBLANES)
        mod4 = mod.reshape(bsz, 6, 1, d)
        cos_t, sin_t = _rope_tables(pos)
    rows = _Rows(n, t_len, tm, mod4)
    rows_big = _Rows(n, t_len, tm_big, mod4)

    q_off = zap
    z, k4, v4 = _in_proj(x2, rows_big, wts["norm1_g"], wts["w_in"], wts["w_in_b"], cos_t, sin_t, tn,
                         q_off // tn, db // tn)
    zp = z.shape[1]
    z3 = z.reshape(bsz, t_len, zp)

    c_len = _pick(t_len, 64, SUBLANES) if t_len >= SUBLANES else SUBLANES
    t_pad = _round_up(t_len, c_len)
    z3a = z3 if t_pad == t_len else jnp.pad(z3[:, :, :zap], ((0, 0), (0, t_pad - t_len), (0, 0)))
    shift_p = jnp.pad(shift0, ((0, 0), (0, zap - za)))[:, None, :]
    yag, wkv = _rwkv(z3a, zap, shift_p, wkv0, wts["mu"], wts["vecs"],
                     wts["w2p"], wts["a2p"], wts["g2p"], c_len, min(t_len, c_len), da)
    yag = yag[:, :t_len].reshape(n, da)
    new_shift = z3[:, t_len - 1, :za]

    n_heads = db // HEAD_B
    k_out = k4.reshape(bsz, t_len, n_heads, HEAD_B)
    v_out = v4.reshape(bsz, t_len, n_heads, HEAD_B)
    if cache is None:
        attn = _attn_prompt(z3, q_off // HEAD_B, (q_off + db) // HEAD_B, (q_off + 2 * db) // HEAD_B, n_heads)
    else:
        cache_k, cache_v, page_table = cache
        q4 = z3[:, :, q_off:q_off + db].reshape(bsz, t_len, n_heads, HEAD_B)
        attn = _attn_sample(q4, k_out, v_out, cache_k, cache_v, page_table)
    attn = attn.reshape(n, db)

    mix = _mix(yag, attn, wts["w_proj_a"], wts["w_proj_b"], z, (q_off + 3 * db) // tn,
               (q_off + 3 * db + d) // tn, tm_big, tn)
    x1 = _out_proj(x2, mix, wts["w_out"], rows_big, _pick(d, 1024, tn))

    if per_row:
        zeros = jnp.zeros((bsz, t_len - 1, dff), F32)
        p1 = jnp.concatenate([conv_prev[:, 1:2], zeros], axis=1).reshape(n, dff)
        p2 = jnp.concatenate([conv_prev, zeros[:, :t_len - 2]], axis=1).reshape(n, dff)
    else:
        p1 = p2 = jnp.zeros((1, dff), F32)
    f, tail = _up(x1, rows_big, wts["norm2_g"], wts["w_up"], wts["conv_w"], wts["conv_b"], p1, p2, tn, per_row)
    if per_row:
        new_conv = tail.reshape(bsz, t_len, dff)[:, t_len - (CONV_W - 1):]
    else:
        tps = t_len // tm_big
        new_conv = tail.reshape(bsz, tps, SUBLANES, dff)[:, -1, SUBLANES - (CONV_W - 1):]
    y = _down(f, wts["w_down"], x1, rows, wts["normf_g"], _pick(dff, 1408, LANES)).reshape(bsz, t_len, d)
    return y, k_out, v_out, wkv, new_shift, new_conv


def kernel(x_prompt, x_sample, cache_k, cache_v, state_wkv, state_shift, state_conv, page_table, c_prompt, c_sample, w_ada, b_ada, norm1_g, w_in, mu_shift, rwkv_w0, rwkv_w2, rwkv_a0, rwkv_a2, rwkv_g2, rwkv_kk, rwkv_ka, rwkv_rk, lnx_g, lnx_b, w_proj_a, w_proj_b, w_out, norm2_g, w_up, conv_w, conv_b, w_down, normf_g):
    d = x_prompt.shape[-1]
    da, db = w_proj_a.shape[0], w_proj_b.shape[0]
    za = mu_shift.shape[0]
    dff = conv_w.shape[1]
    lora_g = rwkv_g2.shape[0]
    assert rwkv_w2.shape[0] == LORA_W and rwkv_a2.shape[0] == LORA_A and LORA_W + LORA_A == LANES
    assert za == 3 * da + LORA_W + LORA_A + lora_g and da % LANES == 0 and db % HEAD_B == 0
    tn = 512 if all(v % 512 == 0 for v in (db, d, dff)) else LANES
    zap = _round_up(za, tn)
    gw = zap - 3 * da - LANES
    assert gw >= lora_g

    assert w_in.shape[1] >= zap
    zeros_l = jnp.zeros((LORA_W, da), F32)
    vecs = jnp.stack([rwkv_w0, rwkv_a0, rwkv_kk, rwkv_ka, rwkv_rk.reshape(da), lnx_g, lnx_b,
                      jnp.zeros((da,), F32)])
    wts = {
        "layout": dict(tn=tn, za=za, zap=zap, da=da, db=db, dff=dff),
        "norm1_g": norm1_g, "norm2_g": norm2_g, "normf_g": normf_g,
        "w_in": w_in, "w_in_b": w_in[:, za:].astype(BF16),
        "mu": jnp.pad(mu_shift, (0, zap - za)).reshape(1, zap),
        "vecs": vecs,
        "w2p": jnp.concatenate([rwkv_w2, zeros_l], axis=0).astype(BF16),
        "a2p": jnp.concatenate([zeros_l, rwkv_a2], axis=0).astype(BF16),
        "g2p": jnp.pad(rwkv_g2, ((0, gw - lora_g), (0, 0))).astype(BF16),
        "w_proj_a": w_proj_a.astype(BF16), "w_proj_b": w_proj_b.astype(BF16),
        "w_out": w_out.astype(BF16), "w_up": w_up.astype(BF16), "w_down": w_down.astype(BF16),
        "conv_w": conv_w, "conv_b": conv_b,
    }

    n_p, t_p = x_prompt.shape[:2]
    n_s, t_s = x_sample.shape[:2]
    n_c = _round_up(n_p + n_s, SUBLANES)
    c_all = jnp.concatenate([c_prompt, c_sample, jnp.zeros((n_c - n_p - n_s, d), F32)], axis=0)
    mod = _ada(c_all, w_ada, b_ada)

    n_heads_a = da // HEAD_A
    out_p = _group(x_prompt, mod[:n_p], jnp.arange(t_p, dtype=jnp.int32), wts,
                   jnp.zeros((n_p, za), F32), jnp.zeros((n_p, n_heads_a, HEAD_A, HEAD_A), F32),
                   None, None, 512)
    past_len = page_table.shape[1] * PAGE_SIZE
    out_s = _group(x_sample, mod[n_p:n_p + n_s], past_len + jnp.arange(t_s, dtype=jnp.int32), wts,
                   state_shift, state_wkv, state_conv, (cache_k, cache_v, page_table), 512)
    y_p, k_p, v_p, wkv_p, sh_p, cv_p = out_p
    y_s, k_s, v_s, wkv_s, sh_s, cv_s = out_s
    return (y_p, y_s, k_p, v_p, wkv_p, sh_p, cv_p, k_s, v_s, wkv_s, sh_s, cv_s)
```

```python
import functools

import jax
import jax.numpy as jnp
from jax import lax
from jax.experimental import pallas as pl
from jax.experimental.pallas import tpu as pltpu

F32 = jnp.float32
BF16 = jnp.bfloat16

HEAD_A = 64
HEAD_B = 128
MOBA_BLOCK = 256
MOBA_TOPK = 3
PAGE_SIZE = 128
ROPE_THETA = 10000.0
LNX_EPS = 64e-5
RMS_EPS = 1e-6
NEG_INF = -1e30
CONV_W = 3
LORA_W = 64
LORA_A = 64
LOG2_E = 1.4426950408889634

LANES = 128
SUBLANES = 8
VMEM_LIMIT = 52 * 1024 * 1024


def _round_up(x, m):
    return (x + m - 1) // m * m


def _pick(n, target, align):
    if n <= target:
        return n
    t = target - target % align
    while t >= align:
        if n % t == 0:
            return t
        t -= align
    return n


def _params(sem):
    return pltpu.CompilerParams(dimension_semantics=sem, vmem_limit_bytes=VMEM_LIMIT)


def _split3(x):
    hi = x.astype(BF16)
    r1 = x - hi.astype(F32)
    mid = r1.astype(BF16)
    lo = (r1 - mid.astype(F32)).astype(BF16)
    return hi, mid, lo


def _dg(a, b, dims):
    return lax.dot_general(a, b, (dims, ((), ())), preferred_element_type=F32)


_NN = ((1,), (0,))
_NT = ((1,), (1,))
_TN = ((0,), (0,))


def _dot_pieces(a_pieces, b_pieces, dims):
    ca, cb = dims[0][0], dims[1][0]
    if a_pieces[0].shape[ca] % LANES == 0:
        return _dg(jnp.concatenate(a_pieces, axis=ca), jnp.concatenate(b_pieces, axis=cb), dims)
    out = _dg(a_pieces[0], b_pieces[0], dims)
    for x, y in zip(a_pieces[1:], b_pieces[1:]):
        out = out + _dg(x, y, dims)
    return out


def _dotx(a, b, dims=_NN):
    ah, am, _ = _split3(a)
    bh, bm, _ = _split3(b)
    return _dot_pieces([ah, ah, am], [bh, bm, bh], dims)


def _dot1(a, b, dims=_NN):
    return _dg(a.astype(BF16), b.astype(BF16), dims)


def _dot01(a01, b, dims=_NN):
    return _dot_pieces([a01, a01, a01], list(_split3(b)), dims)


def _dotr01(a, b01, dims=_NN):
    return _dot_pieces(list(_split3(a)), [b01, b01, b01], dims)


def _sigmoid(x):
    return 1.0 / (1.0 + jnp.exp(-x))


def _softplus(x):
    return jnp.maximum(x, 0.0) + jnp.log(1.0 + jnp.exp(-jnp.abs(x)))


def _norm_mod(x, g, shift, scale):
    ms = jnp.mean(x * x, axis=-1, keepdims=True)
    y = x * lax.rsqrt(ms + RMS_EPS)
    return (y * g) * (1.0 + scale) + shift


def _mod_rows(mod_ref, idx, lo, n):
    if mod_ref.shape[1] == 1:
        return mod_ref[idx]
    return mod_ref[idx, lo:lo + n, :]


def _ada_kernel(c_ref, w_ref, b_ref, o_ref):
    c = c_ref[...]
    s = c * _sigmoid(c)
    o_ref[...] = _dotx(s, w_ref[...]) + b_ref[...]


def _ada(c_all, w_ada, b_ada):
    n, d = c_all.shape
    n6 = w_ada.shape[1]
    tn = _pick(n6, 512, LANES)
    return pl.pallas_call(
        _ada_kernel,
        grid=(n6 // tn,),
        in_specs=[pl.BlockSpec((n, d), lambda j: (0, 0)),
                  pl.BlockSpec((d, tn), lambda j: (0, j)),
                  pl.BlockSpec((1, tn), lambda j: (0, j))],
        out_specs=pl.BlockSpec((n, tn), lambda j: (0, j)),
        out_shape=jax.ShapeDtypeStruct((n, n6), F32),
        compiler_params=_params(("arbitrary",)),
        name="ada",
    )(c_all, w_ada, b_ada.reshape(1, n6))


class _Rows:
    def __init__(self, n_rows, seq_len, tm, mod):
        self.n_rows, self.seq_len, self.tm, self.mod = n_rows, seq_len, tm, mod
        self.per_row = mod.shape[2] != 1
        if not self.per_row:
            assert seq_len % tm == 0
        else:
            assert tm == n_rows
        self.n_tiles = n_rows // tm
        self.tiles_per_seq = max(seq_len // tm, 1)

    def mod_spec(self, width, col_of):
        r = self.tm if self.per_row else 1
        tps = self.tiles_per_seq
        if self.per_row:
            return pl.BlockSpec((None, 6, r, width), lambda i, j: (0, 0, i, col_of(i, j)))
        return pl.BlockSpec((None, 6, r, width), lambda i, j: (i // tps, 0, 0, col_of(i, j)))


def _in_kernel(x_ref, mod_ref, g_ref, wa_ref, wb_ref, cos_ref, sin_ref, o_ref, k4_ref, v4_ref, h_scr,
               *, sub, q_lo, n_qkv):
    j = pl.program_id(1)
    tm = x_ref.shape[0]

    @pl.when(j == 0)
    def _():
        for s in range(tm // sub):
            lo = s * sub
            h = _norm_mod(x_ref[lo:lo + sub, :], g_ref[...],
                          _mod_rows(mod_ref, 0, lo, sub), _mod_rows(mod_ref, 1, lo, sub))
            h_scr[lo:lo + sub, :] = h.astype(BF16)

    @pl.when(j < q_lo)
    def _():
        o_ref[...] = jnp.dot(h_scr[...], wa_ref[...].astype(BF16), preferred_element_type=F32)

    heads_per_tile = wb_ref.shape[1] // HEAD_B

    def emit(rope, head_ref, tile):
        acc = jnp.dot(h_scr[...], wb_ref[...].astype(BF16), preferred_element_type=F32)
        for c in range(heads_per_tile):
            a = acc[:, c * HEAD_B:(c + 1) * HEAD_B]
            if rope:
                a = a * cos_ref[...] + pltpu.roll(a, HEAD_B // 2, 1) * sin_ref[...]
            o_ref[:, c * HEAD_B:(c + 1) * HEAD_B] = a
            if head_ref is not None:
                head_ref[:, tile * heads_per_tile + c, :] = a

    for t in range(n_qkv):
        pl.when(j == q_lo + t)(functools.partial(emit, True, None, t))
        pl.when(j == q_lo + n_qkv + t)(functools.partial(emit, True, k4_ref, t))
        pl.when(j == q_lo + 2 * n_qkv + t)(functools.partial(emit, False, v4_ref, t))

    @pl.when(j >= q_lo + 3 * n_qkv)
    def _():
        o_ref[...] = jnp.dot(h_scr[...], wb_ref[...].astype(BF16), preferred_element_type=F32)


def _in_proj(x2, rows, norm_g, w_f32, wb_bf, cos_t, sin_t, tn, q_lo, n_qkv):
    n, d = x2.shape
    zp = q_lo * tn + wb_bf.shape[1]
    tm = rows.tm
    n_tab = cos_t.shape[0] // tm
    sub = _pick(tm, 128, SUBLANES)
    n_heads = n_qkv * tn // HEAD_B
    head_spec = pl.BlockSpec((tm, n_heads, HEAD_B), lambda i, j: (i, 0, 0))
    head_shape = jax.ShapeDtypeStruct((n, n_heads, HEAD_B), F32)
    return pl.pallas_call(
        functools.partial(_in_kernel, sub=sub, q_lo=q_lo, n_qkv=n_qkv),
        grid=(n // tm, zp // tn),
        in_specs=[pl.BlockSpec((tm, d), lambda i, j: (i, 0), pipeline_mode=pl.Buffered(1)),
                  rows.mod_spec(d, lambda i, j: 0),
                  pl.BlockSpec((1, d), lambda i, j: (0, 0)),
                  pl.BlockSpec((d, tn), lambda i, j: (0, jnp.minimum(j, q_lo - 1))),
                  pl.BlockSpec((d, tn), lambda i, j: (0, jnp.maximum(j - q_lo, 0))),
                  pl.BlockSpec((tm, HEAD_B), lambda i, j: (i % n_tab, 0)),
                  pl.BlockSpec((tm, HEAD_B), lambda i, j: (i % n_tab, 0))],
        out_specs=[pl.BlockSpec((tm, tn), lambda i, j: (i, j)), head_spec, head_spec],
        out_shape=[jax.ShapeDtypeStruct((n, zp), F32), head_shape, head_shape],
        scratch_shapes=[pltpu.VMEM((tm, d), BF16)],
        compiler_params=_params(("arbitrary", "arbitrary")),
        name="in_proj",
    )(x2, rows.mod, norm_g.reshape(1, d), w_f32, wb_bf, cos_t, sin_t)


def _unit_lower_inverse(low, n, c, blk):
    rr = lax.broadcasted_iota(jnp.int32, (n, n), 0)
    cc = lax.broadcasted_iota(jnp.int32, (n, n), 1)
    eye = (rr == cc).astype(F32)
    same = (rr // blk) == (cc // blk)
    dpart = [jnp.where(same, x, 0.0) for x in low]
    inv = [eye - d for d in dpart]
    p = dpart
    k = 2
    while k < blk:
        p = [_dot1(x, x) for x in p]
        inv = [i + _dot1(i, x) for i, x in zip(inv, p)]
        k *= 2
    if c > blk:
        m = [_dot1(i, x - d) for i, x, d in zip(inv, low, dpart)]
        minv = [eye - x for x in m]
        p = m
        k = 2
        while k < c // blk:
            p = [_dot1(x, x) for x in p]
            minv = [i + _dot1(i, x) for i, x in zip(minv, p)]
            k *= 2
        inv = [_dot1(mi, i) for mi, i in zip(minv, inv)]
    return inv


def _rwkv_kernel(z_ref, shift0_ref, s0_ref, mu_ref, vec_ref, w2_ref, a2_ref, g2_ref,
                 yag_ref, sout_ref, state_scr, prev_scr, *, c_len, t_valid, da, n_chunks):
    ci = pl.program_id(1)
    n_pairs = da // LANES
    c2 = 2 * c_len

    @pl.when(ci == 0)
    def _():
        zero = jnp.zeros((HEAD_A, HEAD_A), F32)
        for p in range(n_pairs):
            top = jnp.concatenate([s0_ref[2 * p], zero], axis=1)
            bot = jnp.concatenate([zero, s0_ref[2 * p + 1]], axis=1)
            state_scr[p] = jnp.concatenate([top, bot], axis=0)
        prev_scr[0:1, :] = shift0_ref[...]

    za = z_ref[...]
    row = lax.broadcasted_iota(jnp.int32, (c_len, 1), 0)
    zprev = jnp.where(row == 0, prev_scr[0:1, :], pltpu.roll(za, 1, 0))
    prev_scr[0:1, :] = za[c_len - 1:c_len, :]
    zmix = za + (zprev - za) * mu_ref[...]

    x_wa = zmix[:, 3 * da:3 * da + LANES]
    lane = lax.broadcasted_iota(jnp.int32, (c_len, LANES), 1)
    act_wa = jnp.where(lane < LORA_W, jnp.tanh(x_wa), x_wa)
    sig_g = _sigmoid(zmix[:, 3 * da + LANES:])
    act_wa = act_wa.astype(BF16)
    lw = _dg(act_wa, w2_ref[...], _NN)
    la = _dg(act_wa, a2_ref[...], _NN)
    g = _dg(sig_g.astype(BF16), g2_ref[...], _NN)

    valid = row < t_valid
    head_lo = lane < HEAD_A
    ones_seg = ((lax.broadcasted_iota(jnp.int32, (LANES, LANES), 0) // HEAD_A)
                == (lax.broadcasted_iota(jnp.int32, (LANES, LANES), 1) // HEAD_A)).astype(BF16)
    tr = lax.broadcasted_iota(jnp.int32, (c_len, c_len), 0)
    tc = lax.broadcasted_iota(jnp.int32, (c_len, c_len), 1)
    tri = (tc <= tr).astype(BF16)
    rr = lax.broadcasted_iota(jnp.int32, (c2, c2), 0)
    cc = lax.broadcasted_iota(jnp.int32, (c2, c2), 1)
    same_head = (rr // c_len) == (cc // c_len)
    strict = jnp.logical_and(same_head, (cc % c_len) < (rr % c_len))
    incl = jnp.logical_and(same_head, (cc % c_len) <= (rr % c_len))
    pairs = range(n_pairs)

    def lanes(x, p):
        return x[:, p * LANES:(p + 1) * LANES]

    def head_sums(x):
        xs = jnp.concatenate([lanes(x, p) for p in pairs], axis=0)
        s = _dotr01(xs, ones_seg)
        return jnp.concatenate([s[p * c_len:(p + 1) * c_len] for p in pairs], axis=1)

    def stack(x, p):
        xp = lanes(x, p)
        return jnp.concatenate([jnp.where(head_lo, xp, 0.0), jnp.where(head_lo, 0.0, xp)], axis=0)

    r = zmix[:, 0:da]
    ka = zmix[:, da:2 * da]
    v = zmix[:, 2 * da:3 * da]
    w0, a0, kkw, kaw, rkw, lng, lnb = (vec_ref[i:i + 1, :] for i in range(7))
    w_log = -_softplus(-(w0 + lw)) - 0.5
    logw = jnp.where(valid, -jnp.exp(w_log), 0.0)
    a = _sigmoid(a0 + la)
    kk = ka * kkw
    kk = kk * lax.rsqrt(jnp.maximum(head_sums(kk * kk), 1e-24))
    k_mod = ka * (1.0 + (a - 1.0) * kaw)
    bonus = head_sums(r * k_mod * rkw) * v
    b = jnp.where(valid, kk * a, 0.0)
    k_s = jnp.where(valid, k_mod, 0.0)

    cum = _dot01(tri, logw)
    cum_end = cum[c_len - 1:c_len, :]
    e_neg = jnp.exp(-cum)
    e_end = jnp.exp(cum_end - cum)
    gam_end = jnp.exp(cum_end)
    kt = kk * jnp.exp(cum - logw)
    rt = r * jnp.exp(cum)
    kd = k_s * e_neg
    bd = b * e_neg
    ke = k_s * e_end
    be = b * e_end

    xs = [jnp.concatenate([stack(kt, p), stack(rt, p)], axis=0) for p in pairs]
    ws = [jnp.concatenate([stack(kd, p), stack(bd, p)], axis=0) for p in pairs]
    es = [jnp.concatenate([stack(ke, p), stack(be, p)], axis=0) for p in pairs]
    vs = [stack(v, p) for p in pairs]
    s0 = [state_scr[p] for p in pairs]
    gm = [_dot1(x, w, _NT) for x, w in zip(xs, ws)]
    xs0 = [_dot1(x, s, _NT) for x, s in zip(xs, s0)]
    l_k = [jnp.where(strict, g_[:c2, :c2], 0.0) for g_ in gm]
    l_b = [jnp.where(strict, g_[:c2, c2:], 0.0) for g_ in gm]
    a_kb = [jnp.concatenate([jnp.where(incl, g_[c2:, :c2], 0.0), jnp.where(incl, -g_[c2:, c2:], 0.0)], axis=1)
            for g_ in gm]
    tinv = _unit_lower_inverse(l_b, c2, c_len, min(16, c_len))
    rhs = [x0[:c2] + _dot1(lk, v_) for x0, lk, v_ in zip(xs0, l_k, vs)]
    u = [_dot1(t, r_) for t, r_ in zip(tinv, rhs)]
    res = [r_ - (u_ + _dotx(lb, u_)) for r_, u_, lb in zip(rhs, u, l_b)]
    u = [u_ + _dot1(t, e_) for u_, t, e_ in zip(u, tinv, res)]
    y2 = [x0[c2:] + _dot1(ab, jnp.concatenate([v_, u_], axis=0)) for x0, ab, v_, u_ in zip(xs0, a_kb, vs, u)]
    for p in pairs:
        upd = _dot1(jnp.concatenate([vs[p], -u[p]], axis=0), es[p], _TN)
        state_scr[p] = s0[p] * lanes(gam_end, p) + upd
    y = jnp.concatenate([y_[:c_len] + y_[c_len:] for y_ in y2], axis=1)

    yc = y - head_sums(y) * (1.0 / HEAD_A)
    var = head_sums(yc * yc) * (1.0 / HEAD_A)
    ya = yc * lax.rsqrt(var + LNX_EPS) * lng + lnb + bonus
    yag_ref[...] = (ya * g).astype(BF16)

    @pl.when(ci == n_chunks - 1)
    def _():
        for p in range(n_pairs):
            s_pair = state_scr[p]
            sout_ref[2 * p] = s_pair[:HEAD_A, :HEAD_A]
            sout_ref[2 * p + 1] = s_pair[HEAD_A:, HEAD_A:]


def _rwkv(z3, zap, shift0, s0, mu_p, vecs, w2p, a2p, g2p, c_len, t_valid, da):
    bsz, tp, _ = z3.shape
    n_chunks = tp // c_len
    n_pairs = da // LANES
    n_heads = da // HEAD_A
    gw = zap - 3 * da - LANES
    kern = functools.partial(_rwkv_kernel, c_len=c_len, t_valid=t_valid, da=da, n_chunks=n_chunks)
    state_spec = pl.BlockSpec((None, n_heads, HEAD_A, HEAD_A), lambda b, c: (b, 0, 0, 0))
    return pl.pallas_call(
        kern,
        grid=(bsz, n_chunks),
        in_specs=[pl.BlockSpec((None, c_len, zap), lambda b, c: (b, c, 0)),
                  pl.BlockSpec((None, 1, zap), lambda b, c: (b, 0, 0)),
                  state_spec,
                  pl.BlockSpec((1, zap), lambda b, c: (0, 0)),
                  pl.BlockSpec((SUBLANES, da), lambda b, c: (0, 0)),
                  pl.BlockSpec((LANES, da), lambda b, c: (0, 0)),
                  pl.BlockSpec((LANES, da), lambda b, c: (0, 0)),
                  pl.BlockSpec((gw, da), lambda b, c: (0, 0))],
        out_specs=[pl.BlockSpec((None, c_len, da), lambda b, c: (b, c, 0)), state_spec],
        out_shape=[jax.ShapeDtypeStruct((bsz, tp, da), BF16),
                   jax.ShapeDtypeStruct((bsz, n_heads, HEAD_A, HEAD_A), F32)],
        scratch_shapes=[pltpu.VMEM((n_pairs, LANES, LANES), F32),
                        pltpu.VMEM((SUBLANES, zap), F32)],
        compiler_params=_params(("arbitrary", "arbitrary")),
        name="rwkv",
    )(z3, shift0, s0, mu_p, vecs, w2p, a2p, g2p)


def _topk_mask(gate, valid, n_cand, axis):
    gm = jnp.where(valid, gate, NEG_INF)
    idx = lax.broadcasted_iota(jnp.int32, gate.shape, axis)
    cnt = jnp.zeros(gate.shape, jnp.int32)
    for m in range(n_cand):
        g_m = gm[:, m:m + 1] if axis == 1 else gm[m:m + 1, :]
        ahead = jnp.logical_or(g_m > gm, jnp.logical_and(g_m == gm, m < idx))
        cnt = cnt + ahead.astype(jnp.int32)
    return jnp.logical_and(valid, cnt < MOBA_TOPK)


def _attn_prompt_kernel(q_ref, k_ref, v_ref, o_ref, kmean_scr, vt_scr, *, n_blk):
    i = pl.program_id(2)
    blk = MOBA_BLOCK
    scale = HEAD_B ** -0.5

    @pl.when(i == 0)
    def _():
        kmean_scr[...] = jnp.zeros_like(kmean_scr)
        for n in range(n_blk):
            rows = slice(n * blk, (n + 1) * blk)
            kmean_scr[n:n + 1, :] = jnp.sum(k_ref[rows, :], axis=0, keepdims=True) * (1.0 / blk)
            vt_scr[:, rows] = jnp.transpose(v_ref[rows, :]).astype(BF16)

    q = q_ref[...]
    qb = (q * (scale * LOG2_E)).astype(BF16)
    gate = _dotx(kmean_scr[...], q, _NT)[0:_round_up(n_blk, SUBLANES), :]
    blk_idx = lax.broadcasted_iota(jnp.int32, gate.shape, 0)
    sel = _topk_mask(gate, blk_idx < i, n_blk, 0).astype(F32)
    ki = lax.broadcasted_iota(jnp.int32, (blk, blk), 0)
    qi = lax.broadcasted_iota(jnp.int32, (blk, blk), 1)

    def attend(own):
        width = (own + 1) * blk
        s = _dg(k_ref[0:width, :].astype(BF16), qb, _NT)
        parts = [jnp.where(sel[n:n + 1, :] > 0.0, s[n * blk:(n + 1) * blk, :], NEG_INF) for n in range(own)]
        parts.append(jnp.where(ki <= qi, s[own * blk:, :], NEG_INF))
        m = functools.reduce(jnp.maximum, [jnp.max(x, axis=0, keepdims=True) for x in parts])
        probs = [jnp.exp2(x - m) for x in parts]
        l = functools.reduce(lambda a, b: a + b, [jnp.sum(x, axis=0, keepdims=True) for x in probs])
        pcat = jnp.concatenate([x.astype(BF16) for x in probs], axis=0)
        acc = _dg(vt_scr[:, 0:width], pcat, _NN)
        o_ref[...] = jnp.transpose(acc / l).astype(BF16)

    for own in range(n_blk):
        pl.when(i == own)(functools.partial(attend, own))


def _attn_prompt(z3, q_col, k_col, v_col, n_heads):
    bsz, t_len, _ = z3.shape
    n_blk = t_len // MOBA_BLOCK
    kern = functools.partial(_attn_prompt_kernel, n_blk=n_blk)
    return pl.pallas_call(
        kern,
        grid=(bsz, n_heads, n_blk),
        in_specs=[pl.BlockSpec((None, MOBA_BLOCK, HEAD_B), lambda b, h, i: (b, i, q_col + h)),
                  pl.BlockSpec((None, t_len, HEAD_B), lambda b, h, i: (b, 0, k_col + h)),
                  pl.BlockSpec((None, t_len, HEAD_B), lambda b, h, i: (b, 0, v_col + h))],
        out_specs=pl.BlockSpec((None, MOBA_BLOCK, HEAD_B), lambda b, h, i: (b, i, h)),
        out_shape=jax.ShapeDtypeStruct((bsz, t_len, n_heads * HEAD_B), BF16),
        scratch_shapes=[pltpu.VMEM((LANES, HEAD_B), F32),
                        pltpu.VMEM((HEAD_B, t_len), BF16)],
        compiler_params=_params(("arbitrary", "arbitrary", "arbitrary")),
        name="attn_prompt",
    )(z3, z3, z3)


def _attn_sample_kernel(pt_ref, q_ref, knew_ref, vnew_ref, *refs, n_pages, t_new, n_heads, grp):
    del pt_ref
    ck_refs, cv_refs = refs[:grp], refs[grp:2 * grp]
    o_ref, newk_scr, newv_scr, ksum_scr, sc_scr, sel_scr, acc_scr, l_scr = refs[2 * grp:]
    n_grp = n_pages // grp
    j = pl.program_id(1)
    n_past_blk = n_pages * PAGE_SIZE // MOBA_BLOCK
    pages_per_blk = MOBA_BLOCK // PAGE_SIZE
    n_col = t_new * n_heads
    n_key = PAGE_SIZE * n_heads
    scale = HEAD_B ** -0.5
    ci = lax.broadcasted_iota(jnp.int32, (n_col, n_key), 0)
    li = lax.broadcasted_iota(jnp.int32, (n_col, n_key), 1)
    diag = (li % n_heads) == (ci % n_heads)
    lane = lax.broadcasted_iota(jnp.int32, (n_col, LANES), 1)

    @pl.when(j == 0)
    def _():
        newk_scr[...] = jnp.zeros_like(newk_scr)
        newv_scr[...] = jnp.zeros_like(newv_scr)
        newk_scr[0:t_new] = knew_ref[...]
        newv_scr[0:t_new] = vnew_ref[...]
        ksum_scr[...] = jnp.zeros_like(ksum_scr)
        acc_scr[...] = jnp.zeros_like(acc_scr)

    def score_page(kpage, page):
        k2 = kpage.reshape(n_key, HEAD_B).astype(BF16)
        sc_scr[page] = _dg(q_ref[...].astype(BF16), k2, _NT) * scale

    @pl.when(j < n_grp)
    def _():
        for g in range(grp):
            kpage = ck_refs[g][...]
            page = j * grp + g
            score_page(kpage, page)
            blk = page // pages_per_blk
            ksum_scr[blk] = ksum_scr[blk] + jnp.sum(kpage, axis=0)

    def block_selected(pg):
        blk = pg // pages_per_blk
        return jnp.max(jnp.where(lane == blk, sel_scr[...], 0.0), axis=1, keepdims=True) > 0.0

    @pl.when(j == n_grp - 1)
    def _():
        score_page(newk_scr[...], n_pages)
        n_bh = n_past_blk * n_heads
        kmean = ksum_scr[...].reshape(n_bh, HEAD_B) * (1.0 / MOBA_BLOCK)
        gt = _dotx(q_ref[...], kmean, _NT)
        gci = lax.broadcasted_iota(jnp.int32, (n_col, n_bh), 0)
        gli = lax.broadcasted_iota(jnp.int32, (n_col, n_bh), 1)
        gm = jnp.where((gli % n_heads) == (gci % n_heads), gt, 0.0)
        pick = ((lax.broadcasted_iota(jnp.int32, (n_bh, LANES), 0) // n_heads)
                == lax.broadcasted_iota(jnp.int32, (n_bh, LANES), 1)).astype(BF16)
        gate = _dotr01(gm, pick)
        sel_scr[...] = _topk_mask(gate, lane < n_past_blk, n_past_blk, 1).astype(F32)

        key_row = li // n_heads
        new_ok = jnp.logical_and(diag, jnp.logical_and(key_row < t_new, key_row <= ci // n_heads))
        s_new = sc_scr[n_pages]

        def pg_max(pg, mm):
            ok = jnp.logical_and(block_selected(pg), diag)
            return jnp.maximum(mm, jnp.where(ok, sc_scr[pg], NEG_INF))

        mm = lax.fori_loop(0, n_pages, pg_max, jnp.where(new_ok, s_new, NEG_INF), unroll=grp)
        m = jnp.max(mm, axis=1, keepdims=True)
        p_new = jnp.where(new_ok, jnp.exp(s_new - m), 0.0)
        sc_scr[n_pages] = p_new

        def pg_exp(pg, ll):
            ok = jnp.logical_and(block_selected(pg), diag)
            pr = jnp.where(ok, jnp.exp(sc_scr[pg] - m), 0.0)
            sc_scr[pg] = pr
            return ll + pr

        ll = lax.fori_loop(0, n_pages, pg_exp, p_new, unroll=grp)
        l_scr[...] = jnp.broadcast_to(jnp.sum(ll, axis=1, keepdims=True), l_scr.shape)

    def pv_page(vpage, page):
        v2 = vpage.reshape(n_key, HEAD_B).astype(BF16)
        acc_scr[...] = acc_scr[...] + _dg(sc_scr[page].astype(BF16), v2, _NN)

    @pl.when(j >= n_grp)
    def _():
        for g in range(grp):
            pv_page(cv_refs[g][...], (j - n_grp) * grp + g)

    @pl.when(j == 2 * n_grp - 1)
    def _():
        pv_page(newv_scr[...], n_pages)
        o_ref[...] = (acc_scr[...] / l_scr[...]).astype(BF16)


def _attn_sample(q4, k4, v4, cache_k, cache_v, page_table):
    n_seq, t_new, n_heads, _ = q4.shape
    n_pages = page_table.shape[1]
    n_col = t_new * n_heads
    n_key = PAGE_SIZE * n_heads
    assert (n_pages * PAGE_SIZE) % MOBA_BLOCK == 0 and n_pages * PAGE_SIZE // MOBA_BLOCK <= LANES
    assert n_heads == SUBLANES and t_new <= PAGE_SIZE
    n_past_blk = n_pages * PAGE_SIZE // MOBA_BLOCK
    grp = _pick(n_pages, 16, 1)
    n_grp = n_pages // grp
    kern = functools.partial(_attn_sample_kernel, n_pages=n_pages, t_new=t_new, n_heads=n_heads, grp=grp)
    page_block = (None, PAGE_SIZE, n_heads, HEAD_B)

    def k_spec(g):
        return pl.BlockSpec(page_block, lambda b, j, pt: (pt[b, jnp.minimum(j, n_grp - 1) * grp + g], 0, 0, 0))

    def v_spec(g):
        return pl.BlockSpec(page_block, lambda b, j, pt: (pt[b, jnp.maximum(j - n_grp, 0) * grp + g], 0, 0, 0))

    grid_spec = pltpu.PrefetchScalarGridSpec(
        num_scalar_prefetch=1,
        grid=(n_seq, 2 * n_grp),
        in_specs=[pl.BlockSpec((None, n_col, HEAD_B), lambda b, j, pt: (b, 0, 0)),
                  pl.BlockSpec((None, t_new, n_heads, HEAD_B), lambda b, j, pt: (b, 0, 0, 0)),
                  pl.BlockSpec((None, t_new, n_heads, HEAD_B), lambda b, j, pt: (b, 0, 0, 0))]
                 + [k_spec(g) for g in range(grp)] + [v_spec(g) for g in range(grp)],
        out_specs=pl.BlockSpec((None, n_col, HEAD_B), lambda b, j, pt: (b, 0, 0)),
        scratch_shapes=[pltpu.VMEM((PAGE_SIZE, n_heads, HEAD_B), F32),
                        pltpu.VMEM((PAGE_SIZE, n_heads, HEAD_B), F32),
                        pltpu.VMEM((n_past_blk, n_heads, HEAD_B), F32),
                        pltpu.VMEM((n_pages + 1, n_col, n_key), F32),
                        pltpu.VMEM((n_col, LANES), F32),
                        pltpu.VMEM((n_col, HEAD_B), F32),
                        pltpu.VMEM((n_col, HEAD_B), F32)])
    return pl.pallas_call(
        kern,
        grid_spec=grid_spec,
        out_shape=jax.ShapeDtypeStruct((n_seq, n_col, HEAD_B), BF16),
        compiler_params=_params(("arbitrary", "arbitrary")),
        name="attn_sample",
    )(page_table, q4.reshape(n_seq, n_col, HEAD_B), k4, v4, *([cache_k] * grp), *([cache_v] * grp))


def _mix_kernel(ya_ref, ob_ref, wa_ref, wb_ref, ga_ref, gb_ref, o_ref):
    oa = jnp.dot(ya_ref[...], wa_ref[...], preferred_element_type=F32)
    ob = jnp.dot(ob_ref[...], wb_ref[...], preferred_element_type=F32)
    o_ref[...] = (_sigmoid(ga_ref[...]) * oa + _sigmoid(gb_ref[...]) * ob).astype(BF16)


def _mix(yag, attn, wa_bf, wb_bf, z, ga_col, gb_col, tm, tn):
    n, da = yag.shape
    db = attn.shape[1]
    d = wa_bf.shape[1]
    return pl.pallas_call(
        _mix_kernel,
        grid=(n // tm, d // tn),
        in_specs=[pl.BlockSpec((tm, da), lambda i, j: (i, 0)),
                  pl.BlockSpec((tm, db), lambda i, j: (i, 0)),
                  pl.BlockSpec((da, tn), lambda i, j: (0, j)),
                  pl.BlockSpec((db, tn), lambda i, j: (0, j)),
                  pl.BlockSpec((tm, tn), lambda i, j: (i, ga_col + j)),
                  pl.BlockSpec((tm, tn), lambda i, j: (i, gb_col + j))],
        out_specs=pl.BlockSpec((tm, tn), lambda i, j: (i, j)),
        out_shape=jax.ShapeDtypeStruct((n, d), BF16),
        compiler_params=_params(("arbitrary", "arbitrary")),
        name="mix",
    )(yag, attn, wa_bf, wb_bf, z, z)


def _out_kernel(x_ref, mix_ref, w_ref, mod_ref, o_ref):
    acc = jnp.dot(mix_ref[...], w_ref[...], preferred_element_type=F32)
    o_ref[...] = x_ref[...] + _mod_rows(mod_ref, 2, 0, x_ref.shape[0]) * acc


def _out_proj(x2, mix, w_bf, rows, tn):
    n, d = x2.shape
    tm = rows.tm
    return pl.pallas_call(
        _out_kernel,
        grid=(n // tm, d // tn),
        in_specs=[pl.BlockSpec((tm, tn), lambda i, j: (i, j)),
                  pl.BlockSpec((tm, d), lambda i, j: (i, 0)),
                  pl.BlockSpec((d, tn), lambda i, j: (0, j)),
                  rows.mod_spec(tn, lambda i, j: j)],
        out_specs=pl.BlockSpec((tm, tn), lambda i, j: (i, j)),
        out_shape=jax.ShapeDtypeStruct((n, d), F32),
        compiler_params=_params(("arbitrary", "arbitrary")),
        name="out_proj",
    )(x2, mix, w_bf, rows.mod)


def _gelu_tanh(x):
    return 0.5 * x * (1.0 + jnp.tanh(0.7978845608028654 * (x + 0.044715 * (x * x * x))))


def _up_kernel(x_ref, mod_ref, g_ref, wg_ref, wv_ref, cw_ref, cb_ref, p1_ref, p2_ref,
               f_ref, tail_ref, h_scr, carry_scr, *, sub, seq_len):
    i = pl.program_id(0)
    j = pl.program_id(1)
    tm = x_ref.shape[0]

    @pl.when(j == 0)
    def _():
        for s in range(tm // sub):
            lo = s * sub
            h = _norm_mod(x_ref[lo:lo + sub, :], g_ref[...],
                          _mod_rows(mod_ref, 3, lo, sub), _mod_rows(mod_ref, 4, lo, sub))
            h_scr[lo:lo + sub, :] = h.astype(BF16)

    @pl.when(i == 0)
    def _():
        carry_scr[j] = jnp.zeros(carry_scr.shape[1:], F32)

    hb = h_scr[...]
    ug = jnp.dot(hb, wg_ref[...], preferred_element_type=F32)
    uv = jnp.dot(hb, wv_ref[...], preferred_element_type=F32)
    row = lax.broadcasted_iota(jnp.int32, (tm, 1), 0)
    pos = (i * tm + row) % seq_len
    c0 = carry_scr[j, 0:1, :]
    c1 = carry_scr[j, 1:2, :]
    s1 = jnp.where(row == 0, c1, pltpu.roll(ug, 1, 0))
    s2 = jnp.where(row == 0, c0, jnp.where(row == 1, c1, pltpu.roll(ug, 2, 0)))
    s1 = jnp.where(pos < 1, p1_ref[...], s1)
    s2 = jnp.where(pos < 2, p2_ref[...], s2)
    carry_scr[j, 0:2, :] = ug[tm - 2:tm, :]
    conv = cb_ref[...] + s2 * cw_ref[0:1, :] + s1 * cw_ref[1:2, :] + ug * cw_ref[2:3, :]
    f_ref[...] = (_gelu_tanh(conv) * uv).astype(BF16)
    if tail_ref.shape[0] == tm:
        tail_ref[...] = ug
    else:
        tail_ref[...] = ug[tm - SUBLANES:tm, :]


def _up(x1, rows, norm_g, wup_bf, conv_w, conv_b, p1, p2, tn, full_tail):
    n, d = x1.shape
    dff = conv_w.shape[1]
    tm = rows.tm
    ncol = dff // tn
    sub = _pick(tm, 128, SUBLANES)
    pr = p1.shape[0]
    tail_rows = tm if full_tail else SUBLANES
    prev_spec = (pl.BlockSpec((tm, tn), lambda i, j: (i, j)) if pr != 1
                 else pl.BlockSpec((1, tn), lambda i, j: (0, j)))
    kern = functools.partial(_up_kernel, sub=sub, seq_len=rows.seq_len)
    return pl.pallas_call(
        kern,
        grid=(n // tm, ncol),
        in_specs=[pl.BlockSpec((tm, d), lambda i, j: (i, 0), pipeline_mode=pl.Buffered(1)),
                  rows.mod_spec(d, lambda i, j: 0),
                  pl.BlockSpec((1, d), lambda i, j: (0, 0)),
                  pl.BlockSpec((d, tn), lambda i, j: (0, j)),
                  pl.BlockSpec((d, tn), lambda i, j: (0, ncol + j)),
                  pl.BlockSpec((CONV_W, tn), lambda i, j: (0, j)),
                  pl.BlockSpec((1, tn), lambda i, j: (0, j)),
                  prev_spec, prev_spec],
        out_specs=[pl.BlockSpec((tm, tn), lambda i, j: (i, j)),
                   pl.BlockSpec((tail_rows, tn), lambda i, j: (i, j))],
        out_shape=[jax.ShapeDtypeStruct((n, dff), BF16),
                   jax.ShapeDtypeStruct((n // tm * tail_rows, dff), F32)],
        scratch_shapes=[pltpu.VMEM((tm, d), BF16),
                        pltpu.VMEM((ncol, SUBLANES, tn), F32)],
        compiler_params=_params(("arbitrary", "arbitrary")),
        name="up_proj",
    )(x1, rows.mod, norm_g.reshape(1, d), wup_bf, wup_bf, conv_w, conv_b.reshape(1, dff), p1, p2)


def _down_kernel(f_ref, w_ref, x_ref, mod_ref, g_ref, o_ref, acc_scr, *, n_k):
    k = pl.program_id(1)

    @pl.when(k == 0)
    def _():
        acc_scr[...] = jnp.zeros_like(acc_scr)

    acc_scr[...] += jnp.dot(f_ref[...], w_ref[...], preferred_element_type=F32)

    @pl.when(k == n_k - 1)
    def _():
        x2 = x_ref[...] + _mod_rows(mod_ref, 5, 0, x_ref.shape[0]) * acc_scr[...]
        ms = jnp.mean(x2 * x2, axis=-1, keepdims=True)
        o_ref[...] = (x2 * lax.rsqrt(ms + RMS_EPS)) * g_ref[...]


def _down(f, wd_bf, x1, rows, normf_g, tk):
    n, dff = f.shape
    d = x1.shape[1]
    tm = rows.tm
    n_k = dff // tk
    return pl.pallas_call(
        functools.partial(_down_kernel, n_k=n_k),
        grid=(n // tm, n_k),
        in_specs=[pl.BlockSpec((tm, tk), lambda i, k: (i, k)),
                  pl.BlockSpec((tk, d), lambda i, k: (k, 0)),
                  pl.BlockSpec((tm, d), lambda i, k: (i, 0)),
                  rows.mod_spec(d, lambda i, k: 0),
                  pl.BlockSpec((1, d), lambda i, k: (0, 0))],
        out_specs=pl.BlockSpec((tm, d), lambda i, k: (i, 0)),
        out_shape=jax.ShapeDtypeStruct((n, d), F32),
        scratch_shapes=[pltpu.VMEM((tm, d), F32)],
        compiler_params=_params(("arbitrary", "arbitrary")),
        name="down_proj",
    )(f, wd_bf, x1, rows.mod, normf_g.reshape(1, d))


def _rope_tables(pos):
    half = HEAD_B // 2
    inv = ROPE_THETA ** (-jnp.arange(half, dtype=F32) / half)
    ang = pos.astype(F32)[:, None] * inv[None, :]
    cos, sin = jnp.cos(ang), jnp.sin(ang)
    return jnp.concatenate([cos, cos], axis=1), jnp.concatenate([-sin, sin], axis=1)


def _group(x, mod, pos, wts, shift0, wkv0, conv_prev, cache, tm_target):
    bsz, t_len, d = x.shape
    n = bsz * t_len
    lay = wts["layout"]
    tn, za, zap, da, db, dff = lay["tn"], lay["za"], lay["zap"], lay["da"], lay["db"], lay["dff"]
    x2 = x.reshape(n, d)
    per_row = cache is not None
    if per_row:
        tm = tm_big = n
        mod4 = jnp.repeat(mod.reshape(bsz, 6, d), t_len, axis=0).swapaxes(0, 1)[None]
        cos_t, sin_t = _rope_tables(jnp.tile(pos, bsz))
    else:
        tm = _pick(t_len, tm_target, SUBLANES)
        tm_big = _pick(t_len, 2 * tm_target, SUBLANES)
        mod4 = mod.reshape(bsz, 6, 1, d)
        cos_t, sin_t = _rope_tables(pos)
    rows = _Rows(n, t_len, tm, mod4)
    rows_big = _Rows(n, t_len, tm_big, mod4)

    q_off = zap
    z, k4, v4 = _in_proj(x2, rows_big, wts["norm1_g"], wts["w_in"], wts["w_in_b"], cos_t, sin_t, tn,
                         q_off // tn, db // tn)
    zp = z.shape[1]
    z3 = z.reshape(bsz, t_len, zp)

    c_len = _pick(t_len, 64, SUBLANES) if t_len >= SUBLANES else SUBLANES
    t_pad = _round_up(t_len, c_len)
    z3a = z3 if t_pad == t_len else jnp.pad(z3[:, :, :zap], ((0, 0), (0, t_pad - t_len), (0, 0)))
    shift_p = jnp.pad(shift0, ((0, 0), (0, zap - za)))[:, None, :]
    yag, wkv = _rwkv(z3a, zap, shift_p, wkv0, wts["mu"], wts["vecs"],
                     wts["w2p"], wts["a2p"], wts["g2p"], c_len, min(t_len, c_len), da)
    yag = yag[:, :t_len].reshape(n, da)
    new_shift = z3[:, t_len - 1, :za]

    n_heads = db // HEAD_B
    k_out = k4.reshape(bsz, t_len, n_heads, HEAD_B)
    v_out = v4.reshape(bsz, t_len, n_heads, HEAD_B)
    if cache is None:
        attn = _attn_prompt(z3, q_off // HEAD_B, (q_off + db) // HEAD_B, (q_off + 2 * db) // HEAD_B, n_heads)
    else:
        cache_k, cache_v, page_table = cache
        q4 = z3[:, :, q_off:q_off + db].reshape(bsz, t_len, n_heads, HEAD_B)
        attn = _attn_sample(q4, k_out, v_out, cache_k, cache_v, page_table)
    attn = attn.reshape(n, db)

    mix = _mix(yag, attn, wts["w_proj_a"], wts["w_proj_b"], z, (q_off + 3 * db) // tn,
               (q_off + 3 * db + d) // tn, tm_big, tn)
    x1 = _out_proj(x2, mix, wts["w_out"], rows_big, _pick(d, 1024, tn))

    if per_row:
        zeros = jnp.zeros((bsz, t_len - 1, dff), F32)
        p1 = jnp.concatenate([conv_prev[:, 1:2], zeros], axis=1).reshape(n, dff)
        p2 = jnp.concatenate([conv_prev, zeros[:, :t_len - 2]], axis=1).reshape(n, dff)
    else:
        p1 = p2 = jnp.zeros((1, dff), F32)
    f, tail = _up(x1, rows_big, wts["norm2_g"], wts["w_up"], wts["conv_w"], wts["conv_b"], p1, p2, tn, per_row)
    if per_row:
        new_conv = tail.reshape(bsz, t_len, dff)[:, t_len - (CONV_W - 1):]
    else:
        tps = t_len // tm_big
        new_conv = tail.reshape(bsz, tps, SUBLANES, dff)[:, -1, SUBLANES - (CONV_W - 1):]
    y = _down(f, wts["w_down"], x1, rows, wts["normf_g"], _pick(dff, 1408, LANES)).reshape(bsz, t_len, d)
    return y, k_out, v_out, wkv, new_shift, new_conv


def kernel(x_prompt, x_sample, cache_k, cache_v, state_wkv, state_shift, state_conv, page_table, c_prompt, c_sample, w_ada, b_ada, norm1_g, w_in, mu_shift, rwkv_w0, rwkv_w2, rwkv_a0, rwkv_a2, rwkv_g2, rwkv_kk, rwkv_ka, rwkv_rk, lnx_g, lnx_b, w_proj_a, w_proj_b, w_out, norm2_g, w_up, conv_w, conv_b, w_down, normf_g):
    d = x_prompt.shape[-1]
    da, db = w_proj_a.shape[0], w_proj_b.shape[0]
    za = mu_shift.shape[0]
    dff = conv_w.shape[1]
    lora_g = rwkv_g2.shape[0]
    assert rwkv_w2.shape[0] == LORA_W and rwkv_a2.shape[0] == LORA_A and LORA_W + LORA_A == LANES
    assert za == 3 * da + LORA_W + LORA_A + lora_g and da % LANES == 0 and db % HEAD_B == 0
    tn = 512 if all(v % 512 == 0 for v in (db, d, dff)) else LANES
    zap = _round_up(za, tn)
    gw = zap - 3 * da - LANES
    assert gw >= lora_g

    assert w_in.shape[1] >= zap
    zeros_l = jnp.zeros((LORA_W, da), F32)
    vecs = jnp.stack([rwkv_w0, rwkv_a0, rwkv_kk, rwkv_ka, rwkv_rk.reshape(da), lnx_g, lnx_b,
                      jnp.zeros((da,), F32)])
    wts = {
        "layout": dict(tn=tn, za=za, zap=zap, da=da, db=db, dff=dff),
        "norm1_g": norm1_g, "norm2_g": norm2_g, "normf_g": normf_g,
        "w_in": w_in, "w_in_b": w_in[:, za:],
        "mu": jnp.pad(mu_shift, (0, zap - za)).reshape(1, zap),
        "vecs": vecs,
        "w2p": jnp.concatenate([rwkv_w2, zeros_l], axis=0).astype(BF16),
        "a2p": jnp.concatenate([zeros_l, rwkv_a2], axis=0).astype(BF16),
        "g2p": jnp.pad(rwkv_g2, ((0, gw - lora_g), (0, 0))).astype(BF16),
        "w_proj_a": w_proj_a.astype(BF16), "w_proj_b": w_proj_b.astype(BF16),
        "w_out": w_out.astype(BF16), "w_up": w_up.astype(BF16), "w_down": w_down.astype(BF16),
        "conv_w": conv_w, "conv_b": conv_b,
    }

    n_p, t_p = x_prompt.shape[:2]
    n_s, t_s = x_sample.shape[:2]
    n_c = _round_up(n_p + n_s, SUBLANES)
    c_all = jnp.concatenate([c_prompt, c_sample, jnp.zeros((n_c - n_p - n_s, d), F32)], axis=0)
    mod = _ada(c_all, w_ada, b_ada)

    n_heads_a = da // HEAD_A
    out_p = _group(x_prompt, mod[:n_p], jnp.arange(t_p, dtype=jnp.int32), wts,
                   jnp.zeros((n_p, za), F32), jnp.zeros((n_p, n_heads_a, HEAD_A, HEAD_A), F32),
                   None, None, 512)
    past_len = page_table.shape[1] * PAGE_SIZE
    out_s = _group(x_sample, mod[n_p:n_p + n_s], past_len + jnp.arange(t_s, dtype=jnp.int32), wts,
                   state_shift, state_wkv, state_conv, (cache_k, cache_v, page_table), 512)
    y_p, k_p, v_p, wkv_p, sh_p, cv_p = out_p
    y_s, k_s, v_s, wkv_s, sh_s, cv_s = out_s
    return (y_p, y_s, k_p, v_p, wkv_p, sh_p, cv_p, k_s, v_s, wkv_s, sh_s, cv_s)
```

```python
import functools

import jax
import jax.numpy as jnp
from jax import lax
from jax.experimental import pallas as pl
from jax.experimental.pallas import tpu as pltpu

F32 = jnp.float32
BF16 = jnp.bfloat16

HEAD_A = 64
HEAD_B = 128
MOBA_BLOCK = 256
MOBA_TOPK = 3
PAGE_SIZE = 128
ROPE_THETA = 10000.0
LNX_EPS = 64e-5
RMS_EPS = 1e-6
NEG_INF = -1e30
CONV_W = 3
LORA_W = 64
LORA_A = 64
LOG2_E = 1.4426950408889634

LANES = 128
SUBLANES = 8
VMEM_LIMIT = 52 * 1024 * 1024


def _round_up(x, m):
    return (x + m - 1) // m * m


def _pick(n, target, align):
    if n <= target:
        return n
    t = target - target % align
    while t >= align:
        if n % t == 0:
            return t
        t -= align
    return n


def _params(sem):
    return pltpu.CompilerParams(dimension_semantics=sem, vmem_limit_bytes=VMEM_LIMIT)


def _split3(x):
    hi = x.astype(BF16)
    r1 = x - hi.astype(F32)
    mid = r1.astype(BF16)
    lo = (r1 - mid.astype(F32)).astype(BF16)
    return hi, mid, lo


def _dg(a, b, dims):
    return lax.dot_general(a, b, (dims, ((), ())), preferred_element_type=F32)


_NN = ((1,), (0,))
_NT = ((1,), (1,))
_TN = ((0,), (0,))


def _dot_pieces(a_pieces, b_pieces, dims):
    ca, cb = dims[0][0], dims[1][0]
    if a_pieces[0].shape[ca] % LANES == 0:
        return _dg(jnp.concatenate(a_pieces, axis=ca), jnp.concatenate(b_pieces, axis=cb), dims)
    out = _dg(a_pieces[0], b_pieces[0], dims)
    for x, y in zip(a_pieces[1:], b_pieces[1:]):
        out = out + _dg(x, y, dims)
    return out


def _dotx(a, b, dims=_NN):
    ah, am, _ = _split3(a)
    bh, bm, _ = _split3(b)
    return _dot_pieces([ah, ah, am], [bh, bm, bh], dims)


def _dot1(a, b, dims=_NN):
    return _dg(a.astype(BF16), b.astype(BF16), dims)


def _dot01(a01, b, dims=_NN):
    return _dot_pieces([a01, a01, a01], list(_split3(b)), dims)


def _dotr01(a, b01, dims=_NN):
    return _dot_pieces(list(_split3(a)), [b01, b01, b01], dims)


def _sigmoid(x):
    return 1.0 / (1.0 + jnp.exp(-x))


def _softplus(x):
    return jnp.maximum(x, 0.0) + jnp.log(1.0 + jnp.exp(-jnp.abs(x)))


def _norm_mod(x, g, shift, scale):
    ms = jnp.mean(x * x, axis=-1, keepdims=True)
    y = x * lax.rsqrt(ms + RMS_EPS)
    return (y * g) * (1.0 + scale) + shift


def _mod_rows(mod_ref, idx, lo, n):
    if mod_ref.shape[1] == 1:
        return mod_ref[idx]
    return mod_ref[idx, lo:lo + n, :]


def _ada_kernel(c_ref, w_ref, b_ref, o_ref):
    c = c_ref[...]
    s = c * _sigmoid(c)
    o_ref[...] = _dotx(s, w_ref[...]) + b_ref[...]


def _ada(c_all, w_ada, b_ada):
    n, d = c_all.shape
    n6 = w_ada.shape[1]
    tn = _pick(n6, 512, LANES)
    return pl.pallas_call(
        _ada_kernel,
        grid=(n6 // tn,),
        in_specs=[pl.BlockSpec((n, d), lambda j: (0, 0)),
                  pl.BlockSpec((d, tn), lambda j: (0, j)),
                  pl.BlockSpec((1, tn), lambda j: (0, j))],
        out_specs=pl.BlockSpec((n, tn), lambda j: (0, j)),
        out_shape=jax.ShapeDtypeStruct((n, n6), F32),
        compiler_params=_params(("arbitrary",)),
        name="ada",
    )(c_all, w_ada, b_ada.reshape(1, n6))


class _Rows:
    def __init__(self, n_rows, seq_len, tm, mod):
        self.n_rows, self.seq_len, self.tm, self.mod = n_rows, seq_len, tm, mod
        self.per_row = mod.shape[2] != 1
        if not self.per_row:
            assert seq_len % tm == 0
        else:
            assert tm == n_rows
        self.n_tiles = n_rows // tm
        self.tiles_per_seq = max(seq_len // tm, 1)

    def mod_spec(self, width, col_of):
        r = self.tm if self.per_row else 1
        tps = self.tiles_per_seq
        if self.per_row:
            return pl.BlockSpec((None, 6, r, width), lambda i, j: (0, 0, i, col_of(i, j)))
        return pl.BlockSpec((None, 6, r, width), lambda i, j: (i // tps, 0, 0, col_of(i, j)))


def _w_in_prep_kernel(*refs, q_lo, shift):
    j = pl.program_id(0)
    o_ref = refs[-1]
    wide = jnp.concatenate([r[...] for r in refs[:-1]], axis=0)
    n_out = o_ref.shape[0]

    @pl.when(j < q_lo)
    def _():
        o_ref[...] = wide[:n_out].astype(BF16)

    @pl.when(j >= q_lo)
    def _():
        o_ref[...] = wide[shift:shift + n_out].astype(BF16)


def _w_in_prep(w_t, za, zap, tn):
    z_in, d = w_t.shape
    q_lo = zap // tn
    n_tiles = q_lo + (z_in - za) // tn
    per_tile = tn // LANES
    first_b, shift = za // LANES, za % LANES
    assert shift % SUBLANES == 0 and z_in >= zap and (z_in - za) % tn == 0

    def spec(t):
        def index(j):
            return (jnp.where(j < q_lo, j * per_tile, first_b + (j - q_lo) * per_tile) + t, 0)
        return pl.BlockSpec((LANES, d), index)

    return pl.pallas_call(
        functools.partial(_w_in_prep_kernel, q_lo=q_lo, shift=shift),
        grid=(n_tiles,),
        in_specs=[spec(t) for t in range(per_tile + 1)],
        out_specs=pl.BlockSpec((tn, d), lambda j: (j, 0)),
        out_shape=jax.ShapeDtypeStruct((n_tiles * tn, d), BF16),
        compiler_params=_params(("arbitrary",)),
        name="w_in_prep",
    )(*([w_t] * (per_tile + 1)))


def _in_kernel(x_ref, mod_ref, g_ref, w_ref, cos_ref, sin_ref, o_ref, k4_ref, v4_ref, h_scr,
               *, sub, q_lo, n_qkv):
    j = pl.program_id(1)
    tm = x_ref.shape[0]

    @pl.when(j == 0)
    def _():
        for s in range(tm // sub):
            lo = s * sub
            h = _norm_mod(x_ref[lo:lo + sub, :], g_ref[...],
                          _mod_rows(mod_ref, 0, lo, sub), _mod_rows(mod_ref, 1, lo, sub))
            h_scr[lo:lo + sub, :] = h.astype(BF16)

    heads_per_tile = w_ref.shape[0] // HEAD_B

    def emit(rope, head_ref, tile):
        acc = _dg(h_scr[...], w_ref[...], _NT)
        for c in range(heads_per_tile):
            a = acc[:, c * HEAD_B:(c + 1) * HEAD_B]
            if rope:
                a = a * cos_ref[...] + pltpu.roll(a, HEAD_B // 2, 1) * sin_ref[...]
            o_ref[:, c * HEAD_B:(c + 1) * HEAD_B] = a
            if head_ref is not None:
                head_ref[:, tile * heads_per_tile + c, :] = a

    for t in range(n_qkv):
        pl.when(j == q_lo + t)(functools.partial(emit, True, None, t))
        pl.when(j == q_lo + n_qkv + t)(functools.partial(emit, True, k4_ref, t))
        pl.when(j == q_lo + 2 * n_qkv + t)(functools.partial(emit, False, v4_ref, t))

    @pl.when(jnp.logical_or(j < q_lo, j >= q_lo + 3 * n_qkv))
    def _():
        o_ref[...] = _dg(h_scr[...], w_ref[...], _NT)


def _in_proj(x2, rows, norm_g, w_bf, cos_t, sin_t, tn, q_lo, n_qkv):
    n, d = x2.shape
    zp = w_bf.shape[0]
    tm = rows.tm
    n_tab = cos_t.shape[0] // tm
    sub = _pick(tm, 128, SUBLANES)
    n_heads = n_qkv * tn // HEAD_B
    head_spec = pl.BlockSpec((tm, n_heads, HEAD_B), lambda i, j: (i, 0, 0))
    head_shape = jax.ShapeDtypeStruct((n, n_heads, HEAD_B), F32)
    return pl.pallas_call(
        functools.partial(_in_kernel, sub=sub, q_lo=q_lo, n_qkv=n_qkv),
        grid=(n // tm, zp // tn),
        in_specs=[pl.BlockSpec((tm, d), lambda i, j: (i, 0), pipeline_mode=pl.Buffered(1)),
                  rows.mod_spec(d, lambda i, j: 0),
                  pl.BlockSpec((1, d), lambda i, j: (0, 0)),
                  pl.BlockSpec((tn, d), lambda i, j: (j, 0)),
                  pl.BlockSpec((tm, HEAD_B), lambda i, j: (i % n_tab, 0)),
                  pl.BlockSpec((tm, HEAD_B), lambda i, j: (i % n_tab, 0))],
        out_specs=[pl.BlockSpec((tm, tn), lambda i, j: (i, j)), head_spec, head_spec],
        out_shape=[jax.ShapeDtypeStruct((n, zp), F32), head_shape, head_shape],
        scratch_shapes=[pltpu.VMEM((tm, d), BF16)],
        compiler_params=_params(("arbitrary", "arbitrary")),
        name="in_proj",
    )(x2, rows.mod, norm_g.reshape(1, d), w_bf, cos_t, sin_t)


def _unit_lower_inverse(low, n, c, blk):
    rr = lax.broadcasted_iota(jnp.int32, (n, n), 0)
    cc = lax.broadcasted_iota(jnp.int32, (n, n), 1)
    eye = (rr == cc).astype(F32)
    same = (rr // blk) == (cc // blk)
    dpart = [jnp.where(same, x, 0.0) for x in low]
    inv = [eye - d for d in dpart]
    p = dpart
    k = 2
    while k < blk:
        p = [_dot1(x, x) for x in p]
        inv = [i + _dot1(i, x) for i, x in zip(inv, p)]
        k *= 2
    if c > blk:
        m = [_dot1(i, x - d) for i, x, d in zip(inv, low, dpart)]
        minv = [eye - x for x in m]
        p = m
        k = 2
        while k < c // blk:
            p = [_dot1(x, x) for x in p]
            minv = [i + _dot1(i, x) for i, x in zip(minv, p)]
            k *= 2
        inv = [_dot1(mi, i) for mi, i in zip(minv, inv)]
    return inv


def _rwkv_kernel(z_ref, shift0_ref, s0_ref, mu_ref, vec_ref, w2_ref, a2_ref, g2_ref,
                 yag_ref, sout_ref, state_scr, prev_scr, *, c_len, t_valid, da, n_chunks):
    ci = pl.program_id(1)
    n_pairs = da // LANES
    c2 = 2 * c_len

    @pl.when(ci == 0)
    def _():
        zero = jnp.zeros((HEAD_A, HEAD_A), F32)
        for p in range(n_pairs):
            top = jnp.concatenate([s0_ref[2 * p], zero], axis=1)
            bot = jnp.concatenate([zero, s0_ref[2 * p + 1]], axis=1)
            state_scr[p] = jnp.concatenate([top, bot], axis=0)
        prev_scr[0:1, :] = shift0_ref[...]

    za = z_ref[...]
    row = lax.broadcasted_iota(jnp.int32, (c_len, 1), 0)
    zprev = jnp.where(row == 0, prev_scr[0:1, :], pltpu.roll(za, 1, 0))
    prev_scr[0:1, :] = za[c_len - 1:c_len, :]
    zmix = za + (zprev - za) * mu_ref[...]

    x_wa = zmix[:, 3 * da:3 * da + LANES]
    lane = lax.broadcasted_iota(jnp.int32, (c_len, LANES), 1)
    act_wa = jnp.where(lane < LORA_W, jnp.tanh(x_wa), x_wa)
    sig_g = _sigmoid(zmix[:, 3 * da + LANES:])
    act_wa = act_wa.astype(BF16)
    lw = _dg(act_wa, w2_ref[...], _NN)
    la = _dg(act_wa, a2_ref[...], _NN)
    g = _dg(sig_g.astype(BF16), g2_ref[...], _NN)

    valid = row < t_valid
    head_lo = lane < HEAD_A
    ones_seg = ((lax.broadcasted_iota(jnp.int32, (LANES, LANES), 0) // HEAD_A)
                == (lax.broadcasted_iota(jnp.int32, (LANES, LANES), 1) // HEAD_A)).astype(BF16)
    tr = lax.broadcasted_iota(jnp.int32, (c_len, c_len), 0)
    tc = lax.broadcasted_iota(jnp.int32, (c_len, c_len), 1)
    tri = (tc <= tr).astype(BF16)
    rr = lax.broadcasted_iota(jnp.int32, (c2, c2), 0)
    cc = lax.broadcasted_iota(jnp.int32, (c2, c2), 1)
    same_head = (rr // c_len) == (cc // c_len)
    strict = jnp.logical_and(same_head, (cc % c_len) < (rr % c_len))
    incl = jnp.logical_and(same_head, (cc % c_len) <= (rr % c_len))
    pairs = range(n_pairs)

    def lanes(x, p):
        return x[:, p * LANES:(p + 1) * LANES]

    def head_sums(x):
        xs = jnp.concatenate([lanes(x, p) for p in pairs], axis=0)
        s = _dotr01(xs, ones_seg)
        return jnp.concatenate([s[p * c_len:(p + 1) * c_len] for p in pairs], axis=1)

    def stack(x, p):
        xp = lanes(x, p)
        return jnp.concatenate([jnp.where(head_lo, xp, 0.0), jnp.where(head_lo, 0.0, xp)], axis=0)

    r = zmix[:, 0:da]
    ka = zmix[:, da:2 * da]
    v = zmix[:, 2 * da:3 * da]
    w0, a0, kkw, kaw, rkw, lng, lnb = (vec_ref[i:i + 1, :] for i in range(7))
    w_log = -_softplus(-(w0 + lw)) - 0.5
    logw = jnp.where(valid, -jnp.exp(w_log), 0.0)
    a = _sigmoid(a0 + la)
    kk = ka * kkw
    kk = kk * lax.rsqrt(jnp.maximum(head_sums(kk * kk), 1e-24))
    k_mod = ka * (1.0 + (a - 1.0) * kaw)
    bonus = head_sums(r * k_mod * rkw) * v
    b = jnp.where(valid, kk * a, 0.0)
    k_s = jnp.where(valid, k_mod, 0.0)

    cum = _dot01(tri, logw)
    cum_end = cum[c_len - 1:c_len, :]
    e_neg = jnp.exp(-cum)
    e_end = jnp.exp(cum_end - cum)
    gam_end = jnp.exp(cum_end)
    kt = kk * jnp.exp(cum - logw)
    rt = r * jnp.exp(cum)
    kd = k_s * e_neg
    bd = b * e_neg
    ke = k_s * e_end
    be = b * e_end

    xs = [jnp.concatenate([stack(kt, p), stack(rt, p)], axis=0) for p in pairs]
    ws = [jnp.concatenate([stack(kd, p), stack(bd, p)], axis=0) for p in pairs]
    es = [jnp.concatenate([stack(ke, p), stack(be, p)], axis=0) for p in pairs]
    vs = [stack(v, p) for p in pairs]
    s0 = [state_scr[p] for p in pairs]
    gm = [_dot1(x, w, _NT) for x, w in zip(xs, ws)]
    xs0 = [_dot1(x, s, _NT) for x, s in zip(xs, s0)]
    l_k = [jnp.where(strict, g_[:c2, :c2], 0.0) for g_ in gm]
    l_b = [jnp.where(strict, g_[:c2, c2:], 0.0) for g_ in gm]
    a_kb = [jnp.concatenate([jnp.where(incl, g_[c2:, :c2], 0.0), jnp.where(incl, -g_[c2:, c2:], 0.0)], axis=1)
            for g_ in gm]
    tinv = _unit_lower_inverse(l_b, c2, c_len, min(16, c_len))
    rhs = [x0[:c2] + _dot1(lk, v_) for x0, lk, v_ in zip(xs0, l_k, vs)]
    u = [_dot1(t, r_) for t, r_ in zip(tinv, rhs)]
    res = [r_ - (u_ + _dotx(lb, u_)) for r_, u_, lb in zip(rhs, u, l_b)]
    u = [u_ + _dot1(t, e_) for u_, t, e_ in zip(u, tinv, res)]
    y2 = [x0[c2:] + _dot1(ab, jnp.concatenate([v_, u_], axis=0)) for x0, ab, v_, u_ in zip(xs0, a_kb, vs, u)]
    for p in pairs:
        upd = _dot1(jnp.concatenate([vs[p], -u[p]], axis=0), es[p], _TN)
        state_scr[p] = s0[p] * lanes(gam_end, p) + upd
    y = jnp.concatenate([y_[:c_len] + y_[c_len:] for y_ in y2], axis=1)

    yc = y - head_sums(y) * (1.0 / HEAD_A)
    var = head_sums(yc * yc) * (1.0 / HEAD_A)
    ya = yc * lax.rsqrt(var + LNX_EPS) * lng + lnb + bonus
    yag_ref[...] = (ya * g).astype(BF16)

    @pl.when(ci == n_chunks - 1)
    def _():
        for p in range(n_pairs):
            s_pair = state_scr[p]
            sout_ref[2 * p] = s_pair[:HEAD_A, :HEAD_A]
            sout_ref[2 * p + 1] = s_pair[HEAD_A:, HEAD_A:]


def _rwkv(z3, zap, shift0, s0, mu_p, vecs, w2p, a2p, g2p, c_len, t_valid, da):
    bsz, tp, _ = z3.shape
    n_chunks = tp // c_len
    n_pairs = da // LANES
    n_heads = da // HEAD_A
    gw = zap - 3 * da - LANES
    kern = functools.partial(_rwkv_kernel, c_len=c_len, t_valid=t_valid, da=da, n_chunks=n_chunks)
    state_spec = pl.BlockSpec((None, n_heads, HEAD_A, HEAD_A), lambda b, c: (b, 0, 0, 0))
    return pl.pallas_call(
        kern,
        grid=(bsz, n_chunks),
        in_specs=[pl.BlockSpec((None, c_len, zap), lambda b, c: (b, c, 0)),
                  pl.BlockSpec((None, 1, zap), lambda b, c: (b, 0, 0)),
                  state_spec,
                  pl.BlockSpec((1, zap), lambda b, c: (0, 0)),
                  pl.BlockSpec((SUBLANES, da), lambda b, c: (0, 0)),
                  pl.BlockSpec((LANES, da), lambda b, c: (0, 0)),
                  pl.BlockSpec((LANES, da), lambda b, c: (0, 0)),
                  pl.BlockSpec((gw, da), lambda b, c: (0, 0))],
        out_specs=[pl.BlockSpec((None, c_len, da), lambda b, c: (b, c, 0)), state_spec],
        out_shape=[jax.ShapeDtypeStruct((bsz, tp, da), BF16),
                   jax.ShapeDtypeStruct((bsz, n_heads, HEAD_A, HEAD_A), F32)],
        scratch_shapes=[pltpu.VMEM((n_pairs, LANES, LANES), F32),
                        pltpu.VMEM((SUBLANES, zap), F32)],
        compiler_params=_params(("arbitrary", "arbitrary")),
        name="rwkv",
    )(z3, shift0, s0, mu_p, vecs, w2p, a2p, g2p)


def _topk_mask(gate, valid, n_cand, axis):
    gm = jnp.where(valid, gate, NEG_INF)
    idx = lax.broadcasted_iota(jnp.int32, gate.shape, axis)
    cnt = jnp.zeros(gate.shape, jnp.int32)
    for m in range(n_cand):
        g_m = gm[:, m:m + 1] if axis == 1 else gm[m:m + 1, :]
        ahead = jnp.logical_or(g_m > gm, jnp.logical_and(g_m == gm, m < idx))
        cnt = cnt + ahead.astype(jnp.int32)
    return jnp.logical_and(valid, cnt < MOBA_TOPK)


def _attn_prompt_kernel(q_ref, k_ref, v_ref, o_ref, kmean_scr, vt_scr, *, n_blk):
    i = pl.program_id(2)
    blk = MOBA_BLOCK
    scale = HEAD_B ** -0.5

    @pl.when(i == 0)
    def _():
        kmean_scr[...] = jnp.zeros_like(kmean_scr)
        for n in range(n_blk):
            rows = slice(n * blk, (n + 1) * blk)
            kmean_scr[n:n + 1, :] = jnp.sum(k_ref[rows, :], axis=0, keepdims=True) * (1.0 / blk)
            vt_scr[:, rows] = jnp.transpose(v_ref[rows, :]).astype(BF16)

    q = q_ref[...]
    qb = (q * (scale * LOG2_E)).astype(BF16)
    gate = _dotx(kmean_scr[...], q, _NT)[0:_round_up(n_blk, SUBLANES), :]
    blk_idx = lax.broadcasted_iota(jnp.int32, gate.shape, 0)
    sel = _topk_mask(gate, blk_idx < i, n_blk, 0).astype(F32)
    ki = lax.broadcasted_iota(jnp.int32, (blk, blk), 0)
    qi = lax.broadcasted_iota(jnp.int32, (blk, blk), 1)

    def attend(own):
        width = (own + 1) * blk
        s = _dg(k_ref[0:width, :].astype(BF16), qb, _NT)
        parts = [jnp.where(sel[n:n + 1, :] > 0.0, s[n * blk:(n + 1) * blk, :], NEG_INF) for n in range(own)]
        parts.append(jnp.where(ki <= qi, s[own * blk:, :], NEG_INF))
        m = functools.reduce(jnp.maximum, [jnp.max(x, axis=0, keepdims=True) for x in parts])
        probs = [jnp.exp2(x - m) for x in parts]
        l = functools.reduce(lambda a, b: a + b, [jnp.sum(x, axis=0, keepdims=True) for x in probs])
        pcat = jnp.concatenate([x.astype(BF16) for x in probs], axis=0)
        acc = _dg(vt_scr[:, 0:width], pcat, _NN)
        o_ref[...] = jnp.transpose(acc / l).astype(BF16)

    for own in range(n_blk):
        pl.when(i == own)(functools.partial(attend, own))


def _attn_prompt(z3, q_col, k_col, v_col, n_heads):
    bsz, t_len, _ = z3.shape
    n_blk = t_len // MOBA_BLOCK
    kern = functools.partial(_attn_prompt_kernel, n_blk=n_blk)
    return pl.pallas_call(
        kern,
        grid=(bsz, n_heads, n_blk),
        in_specs=[pl.BlockSpec((None, MOBA_BLOCK, HEAD_B), lambda b, h, i: (b, i, q_col + h)),
                  pl.BlockSpec((None, t_len, HEAD_B), lambda b, h, i: (b, 0, k_col + h)),
                  pl.BlockSpec((None, t_len, HEAD_B), lambda b, h, i: (b, 0, v_col + h))],
        out_specs=pl.BlockSpec((None, MOBA_BLOCK, HEAD_B), lambda b, h, i: (b, i, h)),
        out_shape=jax.ShapeDtypeStruct((bsz, t_len, n_heads * HEAD_B), BF16),
        scratch_shapes=[pltpu.VMEM((LANES, HEAD_B), F32),
                        pltpu.VMEM((HEAD_B, t_len), BF16)],
        compiler_params=_params(("arbitrary", "arbitrary", "arbitrary")),
        name="attn_prompt",
    )(z3, z3, z3)


def _attn_sample_kernel(pt_ref, q_ref, knew_ref, vnew_ref, *refs, n_pages, t_new, n_heads, grp):
    del pt_ref
    ck_refs, cv_refs = refs[:grp], refs[grp:2 * grp]
    o_ref, newk_scr, newv_scr, ksum_scr, sc_scr, sel_scr, acc_scr, l_scr = refs[2 * grp:]
    n_grp = n_pages // grp
    j = pl.program_id(1)
    n_past_blk = n_pages * PAGE_SIZE // MOBA_BLOCK
    pages_per_blk = MOBA_BLOCK // PAGE_SIZE
    n_col = t_new * n_heads
    n_key = PAGE_SIZE * n_heads
    scale = HEAD_B ** -0.5
    ci = lax.broadcasted_iota(jnp.int32, (n_col, n_key), 0)
    li = lax.broadcasted_iota(jnp.int32, (n_col, n_key), 1)
    diag = (li % n_heads) == (ci % n_heads)
    lane = lax.broadcasted_iota(jnp.int32, (n_col, LANES), 1)

    @pl.when(j == 0)
    def _():
        newk_scr[...] = jnp.zeros_like(newk_scr)
        newv_scr[...] = jnp.zeros_like(newv_scr)
        newk_scr[0:t_new] = knew_ref[...]
        newv_scr[0:t_new] = vnew_ref[...]
        ksum_scr[...] = jnp.zeros_like(ksum_scr)
        acc_scr[...] = jnp.zeros_like(acc_scr)

    def score_page(kpage, page):
        k2 = kpage.reshape(n_key, HEAD_B).astype(BF16)
        sc_scr[page] = _dg(q_ref[...].astype(BF16), k2, _NT) * scale

    @pl.when(j < n_grp)
    def _():
        for g in range(grp):
            kpage = ck_refs[g][...]
            page = j * grp + g
            score_page(kpage, page)
            blk = page // pages_per_blk
            ksum_scr[blk] = ksum_scr[blk] + jnp.sum(kpage, axis=0)

    def block_selected(pg):
        blk = pg // pages_per_blk
        return jnp.max(jnp.where(lane == blk, sel_scr[...], 0.0), axis=1, keepdims=True) > 0.0

    @pl.when(j == n_grp - 1)
    def _():
        score_page(newk_scr[...], n_pages)
        n_bh = n_past_blk * n_heads
        kmean = ksum_scr[...].reshape(n_bh, HEAD_B) * (1.0 / MOBA_BLOCK)
        gt = _dotx(q_ref[...], kmean, _NT)
        gci = lax.broadcasted_iota(jnp.int32, (n_col, n_bh), 0)
        gli = lax.broadcasted_iota(jnp.int32, (n_col, n_bh), 1)
        gm = jnp.where((gli % n_heads) == (gci % n_heads), gt, 0.0)
        pick = ((lax.broadcasted_iota(jnp.int32, (n_bh, LANES), 0) // n_heads)
                == lax.broadcasted_iota(jnp.int32, (n_bh, LANES), 1)).astype(BF16)
        gate = _dotr01(gm, pick)
        sel_scr[...] = _topk_mask(gate, lane < n_past_blk, n_past_blk, 1).astype(F32)

        key_row = li // n_heads
        new_ok = jnp.logical_and(diag, jnp.logical_and(key_row < t_new, key_row <= ci // n_heads))
        s_new = sc_scr[n_pages]

        def pg_max(pg, mm):
            ok = jnp.logical_and(block_selected(pg), diag)
            return jnp.maximum(mm, jnp.where(ok, sc_scr[pg], NEG_INF))

        mm = lax.fori_loop(0, n_pages, pg_max, jnp.where(new_ok, s_new, NEG_INF), unroll=grp)
        m = jnp.max(mm, axis=1, keepdims=True)
        p_new = jnp.where(new_ok, jnp.exp(s_new - m), 0.0)
        sc_scr[n_pages] = p_new

        def pg_exp(pg, ll):
            ok = jnp.logical_and(block_selected(pg), diag)
            pr = jnp.where(ok, jnp.exp(sc_scr[pg] - m), 0.0)
            sc_scr[pg] = pr
            return ll + pr

        ll = lax.fori_loop(0, n_pages, pg_exp, p_new, unroll=grp)
        l_scr[...] = jnp.broadcast_to(jnp.sum(ll, axis=1, keepdims=True), l_scr.shape)

    def pv_page(vpage, page):
        v2 = vpage.reshape(n_key, HEAD_B).astype(BF16)
        acc_scr[...] = acc_scr[...] + _dg(sc_scr[page].astype(BF16), v2, _NN)

    @pl.when(j >= n_grp)
    def _():
        for g in range(grp):
            pv_page(cv_refs[g][...], (j - n_grp) * grp + g)

    @pl.when(j == 2 * n_grp - 1)
    def _():
        pv_page(newv_scr[...], n_pages)
        o_ref[...] = (acc_scr[...] / l_scr[...]).astype(BF16)


def _attn_sample(q4, k4, v4, cache_k, cache_v, page_table):
    n_seq, t_new, n_heads, _ = q4.shape
    n_pages = page_table.shape[1]
    n_col = t_new * n_heads
    n_key = PAGE_SIZE * n_heads
    assert (n_pages * PAGE_SIZE) % MOBA_BLOCK == 0 and n_pages * PAGE_SIZE // MOBA_BLOCK <= LANES
    assert n_heads == SUBLANES and t_new <= PAGE_SIZE
    n_past_blk = n_pages * PAGE_SIZE // MOBA_BLOCK
    grp = _pick(n_pages, 16, 1)
    n_grp = n_pages // grp
    kern = functools.partial(_attn_sample_kernel, n_pages=n_pages, t_new=t_new, n_heads=n_heads, grp=grp)
    page_block = (None, PAGE_SIZE, n_heads, HEAD_B)

    def k_spec(g):
        return pl.BlockSpec(page_block, lambda b, j, pt: (pt[b, jnp.minimum(j, n_grp - 1) * grp + g], 0, 0, 0))

    def v_spec(g):
        return pl.BlockSpec(page_block, lambda b, j, pt: (pt[b, jnp.maximum(j - n_grp, 0) * grp + g], 0, 0, 0))

    grid_spec = pltpu.PrefetchScalarGridSpec(
        num_scalar_prefetch=1,
        grid=(n_seq, 2 * n_grp),
        in_specs=[pl.BlockSpec((None, n_col, HEAD_B), lambda b, j, pt: (b, 0, 0)),
                  pl.BlockSpec((None, t_new, n_heads, HEAD_B), lambda b, j, pt: (b, 0, 0, 0)),
                  pl.BlockSpec((None, t_new, n_heads, HEAD_B), lambda b, j, pt: (b, 0, 0, 0))]
                 + [k_spec(g) for g in range(grp)] + [v_spec(g) for g in range(grp)],
        out_specs=pl.BlockSpec((None, n_col, HEAD_B), lambda b, j, pt: (b, 0, 0)),
        scratch_shapes=[pltpu.VMEM((PAGE_SIZE, n_heads, HEAD_B), F32),
                        pltpu.VMEM((PAGE_SIZE, n_heads, HEAD_B), F32),
                        pltpu.VMEM((n_past_blk, n_heads, HEAD_B), F32),
                        pltpu.VMEM((n_pages + 1, n_col, n_key), F32),
                        pltpu.VMEM((n_col, LANES), F32),
                        pltpu.VMEM((n_col, HEAD_B), F32),
                        pltpu.VMEM((n_col, HEAD_B), F32)])
    return pl.pallas_call(
        kern,
        grid_spec=grid_spec,
        out_shape=jax.ShapeDtypeStruct((n_seq, n_col, HEAD_B), BF16),
        compiler_params=_params(("arbitrary", "arbitrary")),
        name="attn_sample",
    )(page_table, q4.reshape(n_seq, n_col, HEAD_B), k4, v4, *([cache_k] * grp), *([cache_v] * grp))


def _mix_kernel(ya_ref, ob_ref, wa_ref, wb_ref, ga_ref, gb_ref, o_ref):
    oa = jnp.dot(ya_ref[...], wa_ref[...], preferred_element_type=F32)
    ob = jnp.dot(ob_ref[...], wb_ref[...], preferred_element_type=F32)
    o_ref[...] = (_sigmoid(ga_ref[...]) * oa + _sigmoid(gb_ref[...]) * ob).astype(BF16)


def _mix(yag, attn, wa_bf, wb_bf, z, ga_col, gb_col, tm, tn):
    n, da = yag.shape
    db = attn.shape[1]
    d = wa_bf.shape[1]
    return pl.pallas_call(
        _mix_kernel,
        grid=(n // tm, d // tn),
        in_specs=[pl.BlockSpec((tm, da), lambda i, j: (i, 0)),
                  pl.BlockSpec((tm, db), lambda i, j: (i, 0)),
                  pl.BlockSpec((da, tn), lambda i, j: (0, j)),
                  pl.BlockSpec((db, tn), lambda i, j: (0, j)),
                  pl.BlockSpec((tm, tn), lambda i, j: (i, ga_col + j)),
                  pl.BlockSpec((tm, tn), lambda i, j: (i, gb_col + j))],
        out_specs=pl.BlockSpec((tm, tn), lambda i, j: (i, j)),
        out_shape=jax.ShapeDtypeStruct((n, d), BF16),
        compiler_params=_params(("arbitrary", "arbitrary")),
        name="mix",
    )(yag, attn, wa_bf, wb_bf, z, z)


def _out_kernel(x_ref, mix_ref, w_ref, mod_ref, o_ref):
    acc = jnp.dot(mix_ref[...], w_ref[...], preferred_element_type=F32)
    o_ref[...] = x_ref[...] + _mod_rows(mod_ref, 2, 0, x_ref.shape[0]) * acc


def _out_proj(x2, mix, w_bf, rows, tn):
    n, d = x2.shape
    tm = rows.tm
    return pl.pallas_call(
        _out_kernel,
        grid=(n // tm, d // tn),
        in_specs=[pl.BlockSpec((tm, tn), lambda i, j: (i, j)),
                  pl.BlockSpec((tm, d), lambda i, j: (i, 0)),
                  pl.BlockSpec((d, tn), lambda i, j: (0, j)),
                  rows.mod_spec(tn, lambda i, j: j)],
        out_specs=pl.BlockSpec((tm, tn), lambda i, j: (i, j)),
        out_shape=jax.ShapeDtypeStruct((n, d), F32),
        compiler_params=_params(("arbitrary", "arbitrary")),
        name="out_proj",
    )(x2, mix, w_bf, rows.mod)


def _gelu_tanh(x):
    return 0.5 * x * (1.0 + jnp.tanh(0.7978845608028654 * (x + 0.044715 * (x * x * x))))


def _up_kernel(x_ref, mod_ref, g_ref, wg_ref, wv_ref, cw_ref, cb_ref, p1_ref, p2_ref,
               f_ref, tail_ref, h_scr, carry_scr, *, sub, seq_len):
    i = pl.program_id(0)
    j = pl.program_id(1)
    tm = x_ref.shape[0]

    @pl.when(j == 0)
    def _():
        for s in range(tm // sub):
            lo = s * sub
            h = _norm_mod(x_ref[lo:lo + sub, :], g_ref[...],
                          _mod_rows(mod_ref, 3, lo, sub), _mod_rows(mod_ref, 4, lo, sub))
            h_scr[lo:lo + sub, :] = h.astype(BF16)

    @pl.when(i == 0)
    def _():
        carry_scr[j] = jnp.zeros(carry_scr.shape[1:], F32)

    hb = h_scr[...]
    ug = jnp.dot(hb, wg_ref[...].astype(BF16), preferred_element_type=F32)
    uv = jnp.dot(hb, wv_ref[...].astype(BF16), preferred_element_type=F32)
    row = lax.broadcasted_iota(jnp.int32, (tm, 1), 0)
    pos = (i * tm + row) % seq_len
    c0 = carry_scr[j, 0:1, :]
    c1 = carry_scr[j, 1:2, :]
    s1 = jnp.where(row == 0, c1, pltpu.roll(ug, 1, 0))
    s2 = jnp.where(row == 0, c0, jnp.where(row == 1, c1, pltpu.roll(ug, 2, 0)))
    s1 = jnp.where(pos < 1, p1_ref[...], s1)
    s2 = jnp.where(pos < 2, p2_ref[...], s2)
    carry_scr[j, 0:2, :] = ug[tm - 2:tm, :]
    conv = cb_ref[...] + s2 * cw_ref[0:1, :] + s1 * cw_ref[1:2, :] + ug * cw_ref[2:3, :]
    f_ref[...] = (_gelu_tanh(conv) * uv).astype(BF16)
    if tail_ref.shape[0] == tm:
        tail_ref[...] = ug
    else:
        tail_ref[...] = ug[tm - SUBLANES:tm, :]


def _up(x1, rows, norm_g, wup_bf, conv_w, conv_b, p1, p2, tn, full_tail):
    n, d = x1.shape
    dff = conv_w.shape[1]
    tm = rows.tm
    ncol = dff // tn
    sub = _pick(tm, 128, SUBLANES)
    pr = p1.shape[0]
    tail_rows = tm if full_tail else SUBLANES
    prev_spec = (pl.BlockSpec((tm, tn), lambda i, j: (i, j)) if pr != 1
                 else pl.BlockSpec((1, tn), lambda i, j: (0, j)))
    kern = functools.partial(_up_kernel, sub=sub, seq_len=rows.seq_len)
    return pl.pallas_call(
        kern,
        grid=(n // tm, ncol),
        in_specs=[pl.BlockSpec((tm, d), lambda i, j: (i, 0), pipeline_mode=pl.Buffered(1)),
                  rows.mod_spec(d, lambda i, j: 0),
                  pl.BlockSpec((1, d), lambda i, j: (0, 0)),
                  pl.BlockSpec((d, tn), lambda i, j: (0, j)),
                  pl.BlockSpec((d, tn), lambda i, j: (0, ncol + j)),
                  pl.BlockSpec((CONV_W, tn), lambda i, j: (0, j)),
                  pl.BlockSpec((1, tn), lambda i, j: (0, j)),
                  prev_spec, prev_spec],
        out_specs=[pl.BlockSpec((tm, tn), lambda i, j: (i, j)),
                   pl.BlockSpec((tail_rows, tn), lambda i, j: (i, j))],
        out_shape=[jax.ShapeDtypeStruct((n, dff), BF16),
                   jax.ShapeDtypeStruct((n // tm * tail_rows, dff), F32)],
        scratch_shapes=[pltpu.VMEM((tm, d), BF16),
                        pltpu.VMEM((ncol, SUBLANES, tn), F32)],
        compiler_params=_params(("arbitrary", "arbitrary")),
        name="up_proj",
    )(x1, rows.mod, norm_g.reshape(1, d), wup_bf, wup_bf, conv_w, conv_b.reshape(1, dff), p1, p2)


def _down_kernel(f_ref, w_ref, x_ref, mod_ref, g_ref, o_ref, acc_scr, *, n_k):
    k = pl.program_id(1)

    @pl.when(k == 0)
    def _():
        acc_scr[...] = jnp.zeros_like(acc_scr)

    acc_scr[...] += jnp.dot(f_ref[...], w_ref[...], preferred_element_type=F32)

    @pl.when(k == n_k - 1)
    def _():
        x2 = x_ref[...] + _mod_rows(mod_ref, 5, 0, x_ref.shape[0]) * acc_scr[...]
        ms = jnp.mean(x2 * x2, axis=-1, keepdims=True)
        o_ref[...] = (x2 * lax.rsqrt(ms + RMS_EPS)) * g_ref[...]


def _down(f, wd_bf, x1, rows, normf_g, tk):
    n, dff = f.shape
    d = x1.shape[1]
    tm = rows.tm
    n_k = dff // tk
    return pl.pallas_call(
        functools.partial(_down_kernel, n_k=n_k),
        grid=(n // tm, n_k),
        in_specs=[pl.BlockSpec((tm, tk), lambda i, k: (i, k)),
                  pl.BlockSpec((tk, d), lambda i, k: (k, 0)),
                  pl.BlockSpec((tm, d), lambda i, k: (i, 0)),
                  rows.mod_spec(d, lambda i, k: 0),
                  pl.BlockSpec((1, d), lambda i, k: (0, 0))],
        out_specs=pl.BlockSpec((tm, d), lambda i, k: (i, 0)),
        out_shape=jax.ShapeDtypeStruct((n, d), F32),
        scratch_shapes=[pltpu.VMEM((tm, d), F32)],
        compiler_params=_params(("arbitrary", "arbitrary")),
        name="down_proj",
    )(f, wd_bf, x1, rows.mod, normf_g.reshape(1, d))


def _rope_tables(pos):
    half = HEAD_B // 2
    inv = ROPE_THETA ** (-jnp.arange(half, dtype=F32) / half)
    ang = pos.astype(F32)[:, None] * inv[None, :]
    cos, sin = jnp.cos(ang), jnp.sin(ang)
    return jnp.concatenate([cos, cos], axis=1), jnp.concatenate([-sin, sin], axis=1)


def _group(x, mod, pos, wts, shift0, wkv0, conv_prev, cache, tm_target):
    bsz, t_len, d = x.shape
    n = bsz * t_len
    lay = wts["layout"]
    tn, za, zap, da, db, dff = lay["tn"], lay["za"], lay["zap"], lay["da"], lay["db"], lay["dff"]
    x2 = x.reshape(n, d)
    per_row = cache is not None
    if per_row:
        tm = tm_big = n
        mod4 = jnp.repeat(mod.reshape(bsz, 6, d), t_len, axis=0).swapaxes(0, 1)[None]
        cos_t, sin_t = _rope_tables(jnp.tile(pos, bsz))
    else:
        tm = _pick(t_len, tm_target, SUBLANES)
        tm_big = _pick(t_len, 2 * tm_target, SUBLANES)
        mod4 = mod.reshape(bsz, 6, 1, d)
        cos_t, sin_t = _rope_tables(pos)
    rows = _Rows(n, t_len, tm, mod4)
    rows_big = _Rows(n, t_len, tm_big, mod4)

    q_off = zap
    z, k4, v4 = _in_proj(x2, rows_big, wts["norm1_g"], wts["w_in"], cos_t, sin_t, tn, q_off // tn, db // tn)
    zp = z.shape[1]
    z3 = z.reshape(bsz, t_len, zp)

    c_len = _pick(t_len, 64, SUBLANES) if t_len >= SUBLANES else SUBLANES
    t_pad = _round_up(t_len, c_len)
    z3a = z3 if t_pad == t_len else jnp.pad(z3[:, :, :zap], ((0, 0), (0, t_pad - t_len), (0, 0)))
    shift_p = jnp.pad(shift0, ((0, 0), (0, zap - za)))[:, None, :]
    yag, wkv = _rwkv(z3a, zap, shift_p, wkv0, wts["mu"], wts["vecs"],
                     wts["w2p"], wts["a2p"], wts["g2p"], c_len, min(t_len, c_len), da)
    yag = yag[:, :t_len].reshape(n, da)
    new_shift = z3[:, t_len - 1, :za]

    n_heads = db // HEAD_B
    k_out = k4.reshape(bsz, t_len, n_heads, HEAD_B)
    v_out = v4.reshape(bsz, t_len, n_heads, HEAD_B)
    if cache is None:
        attn = _attn_prompt(z3, q_off // HEAD_B, (q_off + db) // HEAD_B, (q_off + 2 * db) // HEAD_B, n_heads)
    else:
        cache_k, cache_v, page_table = cache
        q4 = z3[:, :, q_off:q_off + db].reshape(bsz, t_len, n_heads, HEAD_B)
        attn = _attn_sample(q4, k_out, v_out, cache_k, cache_v, page_table)
    attn = attn.reshape(n, db)

    mix = _mix(yag, attn, wts["w_proj_a"], wts["w_proj_b"], z, (q_off + 3 * db) // tn,
               (q_off + 3 * db + d) // tn, tm_big, tn)
    x1 = _out_proj(x2, mix, wts["w_out"], rows_big, _pick(d, 1024, tn))

    if per_row:
        zeros = jnp.zeros((bsz, t_len - 1, dff), F32)
        p1 = jnp.concatenate([conv_prev[:, 1:2], zeros], axis=1).reshape(n, dff)
        p2 = jnp.concatenate([conv_prev, zeros[:, :t_len - 2]], axis=1).reshape(n, dff)
    else:
        p1 = p2 = jnp.zeros((1, dff), F32)
    f, tail = _up(x1, rows_big, wts["norm2_g"], wts["w_up"], wts["conv_w"], wts["conv_b"], p1, p2, tn, per_row)
    if per_row:
        new_conv = tail.reshape(bsz, t_len, dff)[:, t_len - (CONV_W - 1):]
    else:
        tps = t_len // tm_big
        new_conv = tail.reshape(bsz, tps, SUBLANES, dff)[:, -1, SUBLANES - (CONV_W - 1):]
    y = _down(f, wts["w_down"], x1, rows, wts["normf_g"], _pick(dff, 1408, LANES)).reshape(bsz, t_len, d)
    return y, k_out, v_out, wkv, new_shift, new_conv


def kernel(x_prompt, x_sample, cache_k, cache_v, state_wkv, state_shift, state_conv, page_table, c_prompt, c_sample, w_ada, b_ada, norm1_g, w_in, mu_shift, rwkv_w0, rwkv_w2, rwkv_a0, rwkv_a2, rwkv_g2, rwkv_kk, rwkv_ka, rwkv_rk, lnx_g, lnx_b, w_proj_a, w_proj_b, w_out, norm2_g, w_up, conv_w, conv_b, w_down, normf_g):
    d = x_prompt.shape[-1]
    da, db = w_proj_a.shape[0], w_proj_b.shape[0]
    za = mu_shift.shape[0]
    dff = conv_w.shape[1]
    lora_g = rwkv_g2.shape[0]
    assert rwkv_w2.shape[0] == LORA_W and rwkv_a2.shape[0] == LORA_A and LORA_W + LORA_A == LANES
    assert za == 3 * da + LORA_W + LORA_A + lora_g and da % LANES == 0 and db % HEAD_B == 0
    tn = 512 if all(v % 512 == 0 for v in (db, d, dff)) else LANES
    zap = _round_up(za, tn)
    gw = zap - 3 * da - LANES
    assert gw >= lora_g

    zeros_l = jnp.zeros((LORA_W, da), F32)
    vecs = jnp.stack([rwkv_w0, rwkv_a0, rwkv_kk, rwkv_ka, rwkv_rk.reshape(da), lnx_g, lnx_b,
                      jnp.zeros((da,), F32)])
    wts = {
        "layout": dict(tn=tn, za=za, zap=zap, da=da, db=db, dff=dff),
        "norm1_g": norm1_g, "norm2_g": norm2_g, "normf_g": normf_g,
        "w_in": _w_in_prep(w_in.T, za, zap, tn),
        "mu": jnp.pad(mu_shift, (0, zap - za)).reshape(1, zap),
        "vecs": vecs,
        "w2p": jnp.concatenate([rwkv_w2, zeros_l], axis=0).astype(BF16),
        "a2p": jnp.concatenate([zeros_l, rwkv_a2], axis=0).astype(BF16),
        "g2p": jnp.pad(rwkv_g2, ((0, gw - lora_g), (0, 0))).astype(BF16),
        "w_proj_a": w_proj_a.astype(BF16), "w_proj_b": w_proj_b.astype(BF16),
        "w_out": w_out.astype(BF16), "w_up": w_up, "w_down": w_down.astype(BF16),
        "conv_w": conv_w, "conv_b": conv_b,
    }

    n_p, t_p = x_prompt.shape[:2]
    n_s, t_s = x_sample.shape[:2]
    n_c = _round_up(n_p + n_s, SUBLANES)
    c_all = jnp.concatenate([c_prompt, c_sample, jnp.zeros((n_c - n_p - n_s, d), F32)], axis=0)
    mod = _ada(c_all, w_ada, b_ada)

    n_heads_a = da // HEAD_A
    out_p = _group(x_prompt, mod[:n_p], jnp.arange(t_p, dtype=jnp.int32), wts,
                   jnp.zeros((n_p, za), F32), jnp.zeros((n_p, n_heads_a, HEAD_A, HEAD_A), F32),
                   None, None, 512)
    past_len = page_table.shape[1] * PAGE_SIZE
    out_s = _group(x_sample, mod[n_p:n_p + n_s], past_len + jnp.arange(t_s, dtype=jnp.int32), wts,
                   state_shift, state_wkv, state_conv, (cache_k, cache_v, page_table), 512)
    y_p, k_p, v_p, wkv_p, sh_p, cv_p = out_p
    y_s, k_s, v_s, wkv_s, sh_s, cv_s = out_s
    return (y_p, y_s, k_p, v_p, wkv_p, sh_p, cv_p, k_s, v_s, wkv_s, sh_s, cv_s)
```

```python
import functools

import jax
import jax.numpy as jnp
from jax import lax
from jax.experimental import pallas as pl
from jax.experimental.pallas import tpu as pltpu

F32 = jnp.float32
BF16 = jnp.bfloat16

HEAD_A = 64
HEAD_B = 128
MOBA_BLOCK = 256
MOBA_TOPK = 3
PAGE_SIZE = 128
ROPE_THETA = 10000.0
LNX_EPS = 64e-5
RMS_EPS = 1e-6
NEG_INF = -1e30
CONV_W = 3
LORA_W = 64
LORA_A = 64
LOG2_E = 1.4426950408889634

LANES = 128
SUBLANES = 8
VMEM_LIMIT = 52 * 1024 * 1024


def _round_up(x, m):
    return (x + m - 1) // m * m


def _pick(n, target, align):
    if n <= target:
        return n
    t = target - target % align
    while t >= align:
        if n % t == 0:
            return t
        t -= align
    return n


def _params(sem):
    return pltpu.CompilerParams(dimension_semantics=sem, vmem_limit_bytes=VMEM_LIMIT)


def _split3(x):
    hi = x.astype(BF16)
    r1 = x - hi.astype(F32)
    mid = r1.astype(BF16)
    lo = (r1 - mid.astype(F32)).astype(BF16)
    return hi, mid, lo


def _dg(a, b, dims):
    return lax.dot_general(a, b, (dims, ((), ())), preferred_element_type=F32)


_NN = ((1,), (0,))
_NT = ((1,), (1,))
_TN = ((0,), (0,))


def _dot_pieces(a_pieces, b_pieces, dims):
    ca, cb = dims[0][0], dims[1][0]
    if a_pieces[0].shape[ca] % LANES == 0:
        return _dg(jnp.concatenate(a_pieces, axis=ca), jnp.concatenate(b_pieces, axis=cb), dims)
    out = _dg(a_pieces[0], b_pieces[0], dims)
    for x, y in zip(a_pieces[1:], b_pieces[1:]):
        out = out + _dg(x, y, dims)
    return out


def _dotx(a, b, dims=_NN):
    ah, am, _ = _split3(a)
    bh, bm, _ = _split3(b)
    return _dot_pieces([ah, ah, am], [bh, bm, bh], dims)


def _dot1(a, b, dims=_NN):
    return _dg(a.astype(BF16), b.astype(BF16), dims)


def _dot01(a01, b, dims=_NN):
    return _dot_pieces([a01, a01, a01], list(_split3(b)), dims)


def _dotr01(a, b01, dims=_NN):
    return _dot_pieces(list(_split3(a)), [b01, b01, b01], dims)


def _sigmoid(x):
    return 1.0 / (1.0 + jnp.exp(-x))


def _softplus(x):
    return jnp.maximum(x, 0.0) + jnp.log(1.0 + jnp.exp(-jnp.abs(x)))


def _norm_mod(x, g, shift, scale):
    ms = jnp.mean(x * x, axis=-1, keepdims=True)
    y = x * lax.rsqrt(ms + RMS_EPS)
    return (y * g) * (1.0 + scale) + shift


def _mod_rows(mod_ref, idx, lo, n):
    if mod_ref.shape[1] == 1:
        return mod_ref[idx]
    return mod_ref[idx, lo:lo + n, :]


def _ada_kernel(c_ref, w_ref, b_ref, o_ref):
    c = c_ref[...]
    s = c * _sigmoid(c)
    o_ref[...] = _dotx(s, w_ref[...]) + b_ref[...]


def _ada(c_all, w_ada, b_ada):
    n, d = c_all.shape
    n6 = w_ada.shape[1]
    tn = _pick(n6, 512, LANES)
    return pl.pallas_call(
        _ada_kernel,
        grid=(n6 // tn,),
        in_specs=[pl.BlockSpec((n, d), lambda j: (0, 0)),
                  pl.BlockSpec((d, tn), lambda j: (0, j)),
                  pl.BlockSpec((1, tn), lambda j: (0, j))],
        out_specs=pl.BlockSpec((n, tn), lambda j: (0, j)),
        out_shape=jax.ShapeDtypeStruct((n, n6), F32),
        compiler_params=_params(("arbitrary",)),
        name="ada",
    )(c_all, w_ada, b_ada.reshape(1, n6))


class _Rows:
    def __init__(self, n_rows, seq_len, tm, mod):
        self.n_rows, self.seq_len, self.tm, self.mod = n_rows, seq_len, tm, mod
        self.per_row = mod.shape[2] != 1
        if not self.per_row:
            assert seq_len % tm == 0
        else:
            assert tm == n_rows
        self.n_tiles = n_rows // tm
        self.tiles_per_seq = max(seq_len // tm, 1)

    def mod_spec(self, width, col_of):
        r = self.tm if self.per_row else 1
        tps = self.tiles_per_seq
        if self.per_row:
            return pl.BlockSpec((None, 6, r, width), lambda i, j: (0, 0, i, col_of(i, j)))
        return pl.BlockSpec((None, 6, r, width), lambda i, j: (i // tps, 0, 0, col_of(i, j)))


def _w_in_prep_kernel(*refs, q_lo, shift):
    j = pl.program_id(0)
    o_ref = refs[-1]
    wide = jnp.concatenate([r[...] for r in refs[:-1]], axis=0)
    n_out = o_ref.shape[0]

    @pl.when(j < q_lo)
    def _():
        o_ref[...] = wide[:n_out].astype(BF16)

    @pl.when(j >= q_lo)
    def _():
        o_ref[...] = wide[shift:shift + n_out].astype(BF16)


def _w_in_prep(w_t, za, zap, tn):
    z_in, d = w_t.shape
    q_lo = zap // tn
    n_tiles = q_lo + (z_in - za) // tn
    per_tile = tn // LANES
    first_b, shift = za // LANES, za % LANES
    assert shift % SUBLANES == 0 and z_in >= zap and (z_in - za) % tn == 0

    def spec(t):
        def index(j):
            return (jnp.where(j < q_lo, j * per_tile, first_b + (j - q_lo) * per_tile) + t, 0)
        return pl.BlockSpec((LANES, d), index)

    return pl.pallas_call(
        functools.partial(_w_in_prep_kernel, q_lo=q_lo, shift=shift),
        grid=(n_tiles,),
        in_specs=[spec(t) for t in range(per_tile + 1)],
        out_specs=pl.BlockSpec((tn, d), lambda j: (j, 0)),
        out_shape=jax.ShapeDtypeStruct((n_tiles * tn, d), BF16),
        compiler_params=_params(("arbitrary",)),
        name="w_in_prep",
    )(*([w_t] * (per_tile + 1)))


def _in_kernel(x_ref, mod_ref, g_ref, w_ref, cos_ref, sin_ref, o_ref, k4_ref, v4_ref, h_scr,
               *, sub, q_lo, n_qkv):
    j = pl.program_id(1)
    tm = x_ref.shape[0]

    @pl.when(j == 0)
    def _():
        for s in range(tm // sub):
            lo = s * sub
            h = _norm_mod(x_ref[lo:lo + sub, :], g_ref[...],
                          _mod_rows(mod_ref, 0, lo, sub), _mod_rows(mod_ref, 1, lo, sub))
            h_scr[lo:lo + sub, :] = h.astype(BF16)

    heads_per_tile = w_ref.shape[0] // HEAD_B

    def emit(rope, head_ref, tile):
        acc = _dg(h_scr[...], w_ref[...], _NT)
        for c in range(heads_per_tile):
            a = acc[:, c * HEAD_B:(c + 1) * HEAD_B]
            if rope:
                a = a * cos_ref[...] + pltpu.roll(a, HEAD_B // 2, 1) * sin_ref[...]
            o_ref[:, c * HEAD_B:(c + 1) * HEAD_B] = a
            if head_ref is not None:
                head_ref[:, tile * heads_per_tile + c, :] = a

    for t in range(n_qkv):
        pl.when(j == q_lo + t)(functools.partial(emit, True, None, t))
        pl.when(j == q_lo + n_qkv + t)(functools.partial(emit, True, k4_ref, t))
        pl.when(j == q_lo + 2 * n_qkv + t)(functools.partial(emit, False, v4_ref, t))

    @pl.when(jnp.logical_or(j < q_lo, j >= q_lo + 3 * n_qkv))
    def _():
        o_ref[...] = _dg(h_scr[...], w_ref[...], _NT)


def _in_proj(x2, rows, norm_g, w_bf, cos_t, sin_t, tn, q_lo, n_qkv):
    n, d = x2.shape
    zp = w_bf.shape[0]
    tm = rows.tm
    n_tab = cos_t.shape[0] // tm
    sub = _pick(tm, 128, SUBLANES)
    n_heads = n_qkv * tn // HEAD_B
    head_spec = pl.BlockSpec((tm, n_heads, HEAD_B), lambda i, j: (i, 0, 0))
    head_shape = jax.ShapeDtypeStruct((n, n_heads, HEAD_B), F32)
    return pl.pallas_call(
        functools.partial(_in_kernel, sub=sub, q_lo=q_lo, n_qkv=n_qkv),
        grid=(n // tm, zp // tn),
        in_specs=[pl.BlockSpec((tm, d), lambda i, j: (i, 0), pipeline_mode=pl.Buffered(1)),
                  rows.mod_spec(d, lambda i, j: 0),
                  pl.BlockSpec((1, d), lambda i, j: (0, 0)),
                  pl.BlockSpec((tn, d), lambda i, j: (j, 0)),
                  pl.BlockSpec((tm, HEAD_B), lambda i, j: (i % n_tab, 0)),
                  pl.BlockSpec((tm, HEAD_B), lambda i, j: (i % n_tab, 0))],
        out_specs=[pl.BlockSpec((tm, tn), lambda i, j: (i, j)), head_spec, head_spec],
        out_shape=[jax.ShapeDtypeStruct((n, zp), F32), head_shape, head_shape],
        scratch_shapes=[pltpu.VMEM((tm, d), BF16)],
        compiler_params=_params(("arbitrary", "arbitrary")),
        name="in_proj",
    )(x2, rows.mod, norm_g.reshape(1, d), w_bf, cos_t, sin_t)


def _unit_lower_inverse(low, n, c, blk):
    rr = lax.broadcasted_iota(jnp.int32, (n, n), 0)
    cc = lax.broadcasted_iota(jnp.int32, (n, n), 1)
    eye = (rr == cc).astype(F32)
    same = (rr // blk) == (cc // blk)
    dpart = [jnp.where(same, x, 0.0) for x in low]
    inv = [eye - d for d in dpart]
    p = dpart
    k = 2
    while k < blk:
        p = [_dot1(x, x) for x in p]
        inv = [i + _dot1(i, x) for i, x in zip(inv, p)]
        k *= 2
    if c > blk:
        m = [_dot1(i, x - d) for i, x, d in zip(inv, low, dpart)]
        minv = [eye - x for x in m]
        p = m
        k = 2
        while k < c // blk:
            p = [_dot1(x, x) for x in p]
            minv = [i + _dot1(i, x) for i, x in zip(minv, p)]
            k *= 2
        inv = [_dot1(mi, i) for mi, i in zip(minv, inv)]
    return inv


def _rwkv_kernel(z_ref, shift0_ref, s0_ref, mu_ref, vec_ref, w2_ref, a2_ref, g2_ref,
                 yag_ref, sout_ref, state_scr, prev_scr, *, c_len, t_valid, da, n_chunks):
    ci = pl.program_id(1)
    n_pairs = da // LANES
    c2 = 2 * c_len

    @pl.when(ci == 0)
    def _():
        zero = jnp.zeros((HEAD_A, HEAD_A), F32)
        for p in range(n_pairs):
            top = jnp.concatenate([s0_ref[2 * p], zero], axis=1)
            bot = jnp.concatenate([zero, s0_ref[2 * p + 1]], axis=1)
            state_scr[p] = jnp.concatenate([top, bot], axis=0)
        prev_scr[0:1, :] = shift0_ref[...]

    za = z_ref[...]
    row = lax.broadcasted_iota(jnp.int32, (c_len, 1), 0)
    zprev = jnp.where(row == 0, prev_scr[0:1, :], pltpu.roll(za, 1, 0))
    prev_scr[0:1, :] = za[c_len - 1:c_len, :]
    zmix = za + (zprev - za) * mu_ref[...]

    x_wa = zmix[:, 3 * da:3 * da + LANES]
    lane = lax.broadcasted_iota(jnp.int32, (c_len, LANES), 1)
    act_wa = jnp.where(lane < LORA_W, jnp.tanh(x_wa), x_wa)
    sig_g = _sigmoid(zmix[:, 3 * da + LANES:])
    act_wa = act_wa.astype(BF16)
    lw = _dg(act_wa, w2_ref[...], _NN)
    la = _dg(act_wa, a2_ref[...], _NN)
    g = _dg(sig_g.astype(BF16), g2_ref[...], _NN)

    valid = row < t_valid
    head_lo = lane < HEAD_A
    ones_seg = ((lax.broadcasted_iota(jnp.int32, (LANES, LANES), 0) // HEAD_A)
                == (lax.broadcasted_iota(jnp.int32, (LANES, LANES), 1) // HEAD_A)).astype(BF16)
    tr = lax.broadcasted_iota(jnp.int32, (c_len, c_len), 0)
    tc = lax.broadcasted_iota(jnp.int32, (c_len, c_len), 1)
    tri = (tc <= tr).astype(BF16)
    rr = lax.broadcasted_iota(jnp.int32, (c2, c2), 0)
    cc = lax.broadcasted_iota(jnp.int32, (c2, c2), 1)
    same_head = (rr // c_len) == (cc // c_len)
    strict = jnp.logical_and(same_head, (cc % c_len) < (rr % c_len))
    incl = jnp.logical_and(same_head, (cc % c_len) <= (rr % c_len))
    pairs = range(n_pairs)

    def lanes(x, p):
        return x[:, p * LANES:(p + 1) * LANES]

    def head_sums(x):
        xs = jnp.concatenate([lanes(x, p) for p in pairs], axis=0)
        s = _dotr01(xs, ones_seg)
        return jnp.concatenate([s[p * c_len:(p + 1) * c_len] for p in pairs], axis=1)

    def stack(x, p):
        xp = lanes(x, p)
        return jnp.concatenate([jnp.where(head_lo, xp, 0.0), jnp.where(head_lo, 0.0, xp)], axis=0)

    r = zmix[:, 0:da]
    ka = zmix[:, da:2 * da]
    v = zmix[:, 2 * da:3 * da]
    w0, a0, kkw, kaw, rkw, lng, lnb = (vec_ref[i:i + 1, :] for i in range(7))
    w_log = -_softplus(-(w0 + lw)) - 0.5
    logw = jnp.where(valid, -jnp.exp(w_log), 0.0)
    a = _sigmoid(a0 + la)
    kk = ka * kkw
    kk = kk * lax.rsqrt(jnp.maximum(head_sums(kk * kk), 1e-24))
    k_mod = ka * (1.0 + (a - 1.0) * kaw)
    bonus = head_sums(r * k_mod * rkw) * v
    b = jnp.where(valid, kk * a, 0.0)
    k_s = jnp.where(valid, k_mod, 0.0)

    cum = _dot01(tri, logw)
    cum_end = cum[c_len - 1:c_len, :]
    e_neg = jnp.exp(-cum)
    e_end = jnp.exp(cum_end - cum)
    gam_end = jnp.exp(cum_end)
    kt = kk * jnp.exp(cum - logw)
    rt = r * jnp.exp(cum)
    kd = k_s * e_neg
    bd = b * e_neg
    ke = k_s * e_end
    be = b * e_end

    xs = [jnp.concatenate([stack(kt, p), stack(rt, p)], axis=0) for p in pairs]
    ws = [jnp.concatenate([stack(kd, p), stack(bd, p)], axis=0) for p in pairs]
    es = [jnp.concatenate([stack(ke, p), stack(be, p)], axis=0) for p in pairs]
    vs = [stack(v, p) for p in pairs]
    s0 = [state_scr[p] for p in pairs]
    gm = [_dot1(x, w, _NT) for x, w in zip(xs, ws)]
    xs0 = [_dot1(x, s, _NT) for x, s in zip(xs, s0)]
    l_k = [jnp.where(strict, g_[:c2, :c2], 0.0) for g_ in gm]
    l_b = [jnp.where(strict, g_[:c2, c2:], 0.0) for g_ in gm]
    a_kb = [jnp.concatenate([jnp.where(incl, g_[c2:, :c2], 0.0), jnp.where(incl, -g_[c2:, c2:], 0.0)], axis=1)
            for g_ in gm]
    tinv = _unit_lower_inverse(l_b, c2, c_len, min(16, c_len))
    rhs = [x0[:c2] + _dot1(lk, v_) for x0, lk, v_ in zip(xs0, l_k, vs)]
    u = [_dot1(t, r_) for t, r_ in zip(tinv, rhs)]
    res = [r_ - (u_ + _dotx(lb, u_)) for r_, u_, lb in zip(rhs, u, l_b)]
    u = [u_ + _dot1(t, e_) for u_, t, e_ in zip(u, tinv, res)]
    y2 = [x0[c2:] + _dot1(ab, jnp.concatenate([v_, u_], axis=0)) for x0, ab, v_, u_ in zip(xs0, a_kb, vs, u)]
    for p in pairs:
        upd = _dot1(jnp.concatenate([vs[p], -u[p]], axis=0), es[p], _TN)
        state_scr[p] = s0[p] * lanes(gam_end, p) + upd
    y = jnp.concatenate([y_[:c_len] + y_[c_len:] for y_ in y2], axis=1)

    yc = y - head_sums(y) * (1.0 / HEAD_A)
    var = head_sums(yc * yc) * (1.0 / HEAD_A)
    ya = yc * lax.rsqrt(var + LNX_EPS) * lng + lnb + bonus
    yag_ref[...] = (ya * g).astype(BF16)

    @pl.when(ci == n_chunks - 1)
    def _():
        for p in range(n_pairs):
            s_pair = state_scr[p]
            sout_ref[2 * p] = s_pair[:HEAD_A, :HEAD_A]
            sout_ref[2 * p + 1] = s_pair[HEAD_A:, HEAD_A:]


def _rwkv(z3, zap, shift0, s0, mu_p, vecs, w2p, a2p, g2p, c_len, t_valid, da):
    bsz, tp, _ = z3.shape
    n_chunks = tp // c_len
    n_pairs = da // LANES
    n_heads = da // HEAD_A
    gw = zap - 3 * da - LANES
    kern = functools.partial(_rwkv_kernel, c_len=c_len, t_valid=t_valid, da=da, n_chunks=n_chunks)
    state_spec = pl.BlockSpec((None, n_heads, HEAD_A, HEAD_A), lambda b, c: (b, 0, 0, 0))
    return pl.pallas_call(
        kern,
        grid=(bsz, n_chunks),
        in_specs=[pl.BlockSpec((None, c_len, zap), lambda b, c: (b, c, 0)),
                  pl.BlockSpec((None, 1, zap), lambda b, c: (b, 0, 0)),
                  state_spec,
                  pl.BlockSpec((1, zap), lambda b, c: (0, 0)),
                  pl.BlockSpec((SUBLANES, da), lambda b, c: (0, 0)),
                  pl.BlockSpec((LANES, da), lambda b, c: (0, 0)),
                  pl.BlockSpec((LANES, da), lambda b, c: (0, 0)),
                  pl.BlockSpec((gw, da), lambda b, c: (0, 0))],
        out_specs=[pl.BlockSpec((None, c_len, da), lambda b, c: (b, c, 0)), state_spec],
        out_shape=[jax.ShapeDtypeStruct((bsz, tp, da), BF16),
                   jax.ShapeDtypeStruct((bsz, n_heads, HEAD_A, HEAD_A), F32)],
        scratch_shapes=[pltpu.VMEM((n_pairs, LANES, LANES), F32),
                        pltpu.VMEM((SUBLANES, zap), F32)],
        compiler_params=_params(("arbitrary", "arbitrary")),
        name="rwkv",
    )(z3, shift0, s0, mu_p, vecs, w2p, a2p, g2p)


def _topk_mask(gate, valid, n_cand, axis):
    gm = jnp.where(valid, gate, NEG_INF)
    idx = lax.broadcasted_iota(jnp.int32, gate.shape, axis)
    cnt = jnp.zeros(gate.shape, jnp.int32)
    for m in range(n_cand):
        g_m = gm[:, m:m + 1] if axis == 1 else gm[m:m + 1, :]
        ahead = jnp.logical_or(g_m > gm, jnp.logical_and(g_m == gm, m < idx))
        cnt = cnt + ahead.astype(jnp.int32)
    return jnp.logical_and(valid, cnt < MOBA_TOPK)


def _attn_prompt_kernel(q_ref, k_ref, v_ref, o_ref, kmean_scr, vt_scr, *, n_blk):
    i = pl.program_id(2)
    blk = MOBA_BLOCK
    scale = HEAD_B ** -0.5

    @pl.when(i == 0)
    def _():
        kmean_scr[...] = jnp.zeros_like(kmean_scr)
        for n in range(n_blk):
            rows = slice(n * blk, (n + 1) * blk)
            kmean_scr[n:n + 1, :] = jnp.sum(k_ref[rows, :], axis=0, keepdims=True) * (1.0 / blk)
            vt_scr[:, rows] = jnp.transpose(v_ref[rows, :]).astype(BF16)

    q = q_ref[...]
    qb = (q * (scale * LOG2_E)).astype(BF16)
    gate = _dotx(kmean_scr[...], q, _NT)[0:_round_up(n_blk, SUBLANES), :]
    blk_idx = lax.broadcasted_iota(jnp.int32, gate.shape, 0)
    sel = _topk_mask(gate, blk_idx < i, n_blk, 0).astype(F32)
    ki = lax.broadcasted_iota(jnp.int32, (blk, blk), 0)
    qi = lax.broadcasted_iota(jnp.int32, (blk, blk), 1)

    def attend(own):
        width = (own + 1) * blk
        s = _dg(k_ref[0:width, :].astype(BF16), qb, _NT)
        parts = [jnp.where(sel[n:n + 1, :] > 0.0, s[n * blk:(n + 1) * blk, :], NEG_INF) for n in range(own)]
        parts.append(jnp.where(ki <= qi, s[own * blk:, :], NEG_INF))
        m = functools.reduce(jnp.maximum, [jnp.max(x, axis=0, keepdims=True) for x in parts])
        probs = [jnp.exp2(x - m) for x in parts]
        l = functools.reduce(lambda a, b: a + b, [jnp.sum(x, axis=0, keepdims=True) for x in probs])
        pcat = jnp.concatenate([x.astype(BF16) for x in probs], axis=0)
        acc = _dg(vt_scr[:, 0:width], pcat, _NN)
        o_ref[...] = jnp.transpose(acc / l).astype(BF16)

    for own in range(n_blk):
        pl.when(i == own)(functools.partial(attend, own))


def _attn_prompt(z3, q_col, k_col, v_col, n_heads):
    bsz, t_len, _ = z3.shape
    n_blk = t_len // MOBA_BLOCK
    kern = functools.partial(_attn_prompt_kernel, n_blk=n_blk)
    return pl.pallas_call(
        kern,
        grid=(bsz, n_heads, n_blk),
        in_specs=[pl.BlockSpec((None, MOBA_BLOCK, HEAD_B), lambda b, h, i: (b, i, q_col + h)),
                  pl.BlockSpec((None, t_len, HEAD_B), lambda b, h, i: (b, 0, k_col + h)),
                  pl.BlockSpec((None, t_len, HEAD_B), lambda b, h, i: (b, 0, v_col + h))],
        out_specs=pl.BlockSpec((None, MOBA_BLOCK, HEAD_B), lambda b, h, i: (b, i, h)),
        out_shape=jax.ShapeDtypeStruct((bsz, t_len, n_heads * HEAD_B), BF16),
        scratch_shapes=[pltpu.VMEM((LANES, HEAD_B), F32),
                        pltpu.VMEM((HEAD_B, t_len), BF16)],
        compiler_params=_params(("arbitrary", "arbitrary", "arbitrary")),
        name="attn_prompt",
    )(z3, z3, z3)


def _attn_sample_kernel(pt_ref, q_ref, knew_ref, vnew_ref, *refs, n_pages, t_new, n_heads, grp):
    del pt_ref
    ck_refs, cv_refs = refs[:grp], refs[grp:2 * grp]
    o_ref, newk_scr, newv_scr, ksum_scr, sc_scr, sel_scr, acc_scr, l_scr = refs[2 * grp:]
    n_grp = n_pages // grp
    j = pl.program_id(1)
    n_past_blk = n_pages * PAGE_SIZE // MOBA_BLOCK
    pages_per_blk = MOBA_BLOCK // PAGE_SIZE
    n_row = n_heads * SUBLANES
    pair = 2 * SUBLANES
    scale = HEAD_B ** -0.5
    lane = lax.broadcasted_iota(jnp.int32, (n_row, LANES), 1)
    row = lax.broadcasted_iota(jnp.int32, (n_row, LANES), 0)

    @pl.when(j == 0)
    def _():
        newk_scr[...] = jnp.zeros_like(newk_scr)
        newv_scr[...] = jnp.zeros_like(newv_scr)
        newk_scr[0:t_new * n_heads, :] = knew_ref[...]
        newv_scr[0:t_new * n_heads, :] = vnew_ref[...]
        ksum_scr[...] = jnp.zeros_like(ksum_scr)
        acc_scr[...] = jnp.zeros_like(acc_scr)

    def score_page(k_ref, page):
        qb = q_ref[...].astype(BF16)
        for h in range(n_heads):
            lo, half = (h // 2) * pair, (h % 2) * SUBLANES
            k_h = k_ref[pl.ds(h, PAGE_SIZE, stride=n_heads), :]
            s2 = _dg(qb[lo:lo + pair], k_h.astype(BF16), _NT)
            sc_scr[page, h * SUBLANES:(h + 1) * SUBLANES, :] = s2[half:half + SUBLANES] * scale

    @pl.when(j < n_grp)
    def _():
        for g in range(grp):
            page = j * grp + g
            score_page(ck_refs[g], page)
            blk = page // pages_per_blk
            kpage = ck_refs[g][...].reshape(PAGE_SIZE, n_heads, HEAD_B)
            ksum_scr[blk] = ksum_scr[blk] + jnp.sum(kpage, axis=0)

    def block_selected(pg):
        blk = pg // pages_per_blk
        return jnp.max(jnp.where(lane == blk, sel_scr[...], 0.0), axis=1, keepdims=True) > 0.0

    @pl.when(j == n_grp - 1)
    def _():
        score_page(newk_scr, n_pages)
        q = q_ref[...]
        gates = []
        for h in range(n_heads):
            kmean_h = ksum_scr[:, h, :] * (1.0 / MOBA_BLOCK)
            gates.append(_dotx(q, kmean_h, _NT)[h * SUBLANES:(h + 1) * SUBLANES])
        gate = jnp.concatenate(gates, axis=0)
        sel_scr[...] = _topk_mask(gate, lane < n_past_blk, n_past_blk, 1).astype(F32)

        new_ok = jnp.logical_and(lane < t_new, lane <= row % SUBLANES)
        s_new = sc_scr[n_pages]

        def pg_max(pg, mm):
            ok = block_selected(pg)
            return jnp.maximum(mm, jnp.where(ok, sc_scr[pg], NEG_INF))

        mm = lax.fori_loop(0, n_pages, pg_max, jnp.where(new_ok, s_new, NEG_INF), unroll=grp)
        m = jnp.max(mm, axis=1, keepdims=True)
        p_new = jnp.where(new_ok, jnp.exp(s_new - m), 0.0)
        sc_scr[n_pages] = p_new

        def pg_exp(pg, ll):
            ok = block_selected(pg)
            pr = jnp.where(ok, jnp.exp(sc_scr[pg] - m), 0.0)
            sc_scr[pg] = pr
            return ll + pr

        ll = lax.fori_loop(0, n_pages, pg_exp, p_new, unroll=grp)
        l_scr[...] = jnp.broadcast_to(jnp.sum(ll, axis=1, keepdims=True), l_scr.shape)

    def pv_page(v_ref, page):
        pb = sc_scr[page].astype(BF16)
        for h in range(n_heads):
            lo, half = (h // 2) * pair, (h % 2) * SUBLANES
            v_h = v_ref[pl.ds(h, PAGE_SIZE, stride=n_heads), :]
            o2 = _dg(pb[lo:lo + pair], v_h.astype(BF16), _NN)
            rows = slice(h * SUBLANES, (h + 1) * SUBLANES)
            acc_scr[rows, :] = acc_scr[rows, :] + o2[half:half + SUBLANES]

    @pl.when(j >= n_grp)
    def _():
        for g in range(grp):
            pv_page(cv_refs[g], (j - n_grp) * grp + g)

    @pl.when(j == 2 * n_grp - 1)
    def _():
        pv_page(newv_scr, n_pages)
        o_ref[...] = (acc_scr[...] / l_scr[...]).astype(BF16)


def _attn_sample(q4, k4, v4, cache_k, cache_v, page_table):
    n_seq, t_new, n_heads, _ = q4.shape
    n_pages = page_table.shape[1]
    n_row = n_heads * SUBLANES
    assert (n_pages * PAGE_SIZE) % MOBA_BLOCK == 0 and n_pages * PAGE_SIZE // MOBA_BLOCK <= LANES
    assert n_heads % 2 == 0 and t_new <= SUBLANES
    q_rows = jnp.pad(jnp.swapaxes(q4, 1, 2), ((0, 0), (0, 0), (0, SUBLANES - t_new), (0, 0)))
    q_rows = q_rows.reshape(n_seq, n_row, HEAD_B)
    grp = _pick(n_pages, 16, 1)
    n_grp = n_pages // grp
    kern = functools.partial(_attn_sample_kernel, n_pages=n_pages, t_new=t_new, n_heads=n_heads, grp=grp)
    n_key = PAGE_SIZE * n_heads
    page_block = (None, n_key, HEAD_B)
    as_rows = lambda t: t.reshape(t.shape[0], -1, HEAD_B)

    def k_spec(g):
        return pl.BlockSpec(page_block, lambda b, j, pt: (pt[b, jnp.minimum(j, n_grp - 1) * grp + g], 0, 0))

    def v_spec(g):
        return pl.BlockSpec(page_block, lambda b, j, pt: (pt[b, jnp.maximum(j - n_grp, 0) * grp + g], 0, 0))

    grid_spec = pltpu.PrefetchScalarGridSpec(
        num_scalar_prefetch=1,
        grid=(n_seq, 2 * n_grp),
        in_specs=[pl.BlockSpec((None, n_row, HEAD_B), lambda b, j, pt: (b, 0, 0)),
                  pl.BlockSpec((None, t_new * n_heads, HEAD_B), lambda b, j, pt: (b, 0, 0)),
                  pl.BlockSpec((None, t_new * n_heads, HEAD_B), lambda b, j, pt: (b, 0, 0))]
                 + [k_spec(g) for g in range(grp)] + [v_spec(g) for g in range(grp)],
        out_specs=pl.BlockSpec((None, n_row, HEAD_B), lambda b, j, pt: (b, 0, 0)),
        scratch_shapes=[pltpu.VMEM((n_key, HEAD_B), F32),
                        pltpu.VMEM((n_key, HEAD_B), F32),
                        pltpu.VMEM((LANES, n_heads, HEAD_B), F32),
                        pltpu.VMEM((n_pages + 1, n_row, PAGE_SIZE), F32),
                        pltpu.VMEM((n_row, LANES), F32),
                        pltpu.VMEM((n_row, HEAD_B), F32),
                        pltpu.VMEM((n_row, HEAD_B), F32)])
    out = pl.pallas_call(
        kern,
        grid_spec=grid_spec,
        out_shape=jax.ShapeDtypeStruct((n_seq, n_row, HEAD_B), BF16),
        compiler_params=_params(("arbitrary", "arbitrary")),
        name="attn_sample",
    )(page_table, q_rows, as_rows(k4), as_rows(v4), *([as_rows(cache_k)] * grp), *([as_rows(cache_v)] * grp))
    out = out.reshape(n_seq, n_heads, SUBLANES, HEAD_B)[:, :, :t_new]
    return jnp.swapaxes(out, 1, 2).reshape(n_seq, t_new, n_heads * HEAD_B)


def _mix_kernel(ya_ref, ob_ref, wa_ref, wb_ref, ga_ref, gb_ref, o_ref):
    oa = jnp.dot(ya_ref[...], wa_ref[...], preferred_element_type=F32)
    ob = jnp.dot(ob_ref[...], wb_ref[...], preferred_element_type=F32)
    o_ref[...] = (_sigmoid(ga_ref[...]) * oa + _sigmoid(gb_ref[...]) * ob).astype(BF16)


def _mix(yag, attn, wa_bf, wb_bf, z, ga_col, gb_col, tm, tn):
    n, da = yag.shape
    db = attn.shape[1]
    d = wa_bf.shape[1]
    return pl.pallas_call(
        _mix_kernel,
        grid=(n // tm, d // tn),
        in_specs=[pl.BlockSpec((tm, da), lambda i, j: (i, 0)),
                  pl.BlockSpec((tm, db), lambda i, j: (i, 0)),
                  pl.BlockSpec((da, tn), lambda i, j: (0, j)),
                  pl.BlockSpec((db, tn), lambda i, j: (0, j)),
                  pl.BlockSpec((tm, tn), lambda i, j: (i, ga_col + j)),
                  pl.BlockSpec((tm, tn), lambda i, j: (i, gb_col + j))],
        out_specs=pl.BlockSpec((tm, tn), lambda i, j: (i, j)),
        out_shape=jax.ShapeDtypeStruct((n, d), BF16),
        compiler_params=_params(("arbitrary", "arbitrary")),
        name="mix",
    )(yag, attn, wa_bf, wb_bf, z, z)


def _out_kernel(x_ref, mix_ref, w_ref, mod_ref, o_ref):
    acc = jnp.dot(mix_ref[...], w_ref[...], preferred_element_type=F32)
    o_ref[...] = x_ref[...] + _mod_rows(mod_ref, 2, 0, x_ref.shape[0]) * acc


def _out_proj(x2, mix, w_bf, rows, tn):
    n, d = x2.shape
    tm = rows.tm
    return pl.pallas_call(
        _out_kernel,
        grid=(n // tm, d // tn),
        in_specs=[pl.BlockSpec((tm, tn), lambda i, j: (i, j)),
                  pl.BlockSpec((tm, d), lambda i, j: (i, 0)),
                  pl.BlockSpec((d, tn), lambda i, j: (0, j)),
                  rows.mod_spec(tn, lambda i, j: j)],
        out_specs=pl.BlockSpec((tm, tn), lambda i, j: (i, j)),
        out_shape=jax.ShapeDtypeStruct((n, d), F32),
        compiler_params=_params(("arbitrary", "arbitrary")),
        name="out_proj",
    )(x2, mix, w_bf, rows.mod)


def _gelu_tanh(x):
    return 0.5 * x * (1.0 + jnp.tanh(0.7978845608028654 * (x + 0.044715 * (x * x * x))))


def _up_kernel(x_ref, mod_ref, g_ref, wg_ref, wv_ref, cw_ref, cb_ref, p1_ref, p2_ref,
               f_ref, tail_ref, h_scr, carry_scr, *, sub, seq_len):
    i = pl.program_id(0)
    j = pl.program_id(1)
    tm = x_ref.shape[0]

    @pl.when(j == 0)
    def _():
        for s in range(tm // sub):
            lo = s * sub
            h = _norm_mod(x_ref[lo:lo + sub, :], g_ref[...],
                          _mod_rows(mod_ref, 3, lo, sub), _mod_rows(mod_ref, 4, lo, sub))
            h_scr[lo:lo + sub, :] = h.astype(BF16)

    @pl.when(i == 0)
    def _():
        carry_scr[j] = jnp.zeros(carry_scr.shape[1:], F32)

    hb = h_scr[...]
    ug = jnp.dot(hb, wg_ref[...].astype(BF16), preferred_element_type=F32)
    uv = jnp.dot(hb, wv_ref[...].astype(BF16), preferred_element_type=F32)
    row = lax.broadcasted_iota(jnp.int32, (tm, 1), 0)
    pos = (i * tm + row) % seq_len
    c0 = carry_scr[j, 0:1, :]
    c1 = carry_scr[j, 1:2, :]
    s1 = jnp.where(row == 0, c1, pltpu.roll(ug, 1, 0))
    s2 = jnp.where(row == 0, c0, jnp.where(row == 1, c1, pltpu.roll(ug, 2, 0)))
    s1 = jnp.where(pos < 1, p1_ref[...], s1)
    s2 = jnp.where(pos < 2, p2_ref[...], s2)
    carry_scr[j, 0:2, :] = ug[tm - 2:tm, :]
    conv = cb_ref[...] + s2 * cw_ref[0:1, :] + s1 * cw_ref[1:2, :] + ug * cw_ref[2:3, :]
    f_ref[...] = (_gelu_tanh(conv) * uv).astype(BF16)
    if tail_ref.shape[0] == tm:
        tail_ref[...] = ug
    else:
        tail_ref[...] = ug[tm - SUBLANES:tm, :]


def _up(x1, rows, norm_g, wup_bf, conv_w, conv_b, p1, p2, tn, full_tail):
    n, d = x1.shape
    dff = conv_w.shape[1]
    tm = rows.tm
    ncol = dff // tn
    sub = _pick(tm, 128, SUBLANES)
    pr = p1.shape[0]
    tail_rows = tm if full_tail else SUBLANES
    prev_spec = (pl.BlockSpec((tm, tn), lambda i, j: (i, j)) if pr != 1
                 else pl.BlockSpec((1, tn), lambda i, j: (0, j)))
    kern = functools.partial(_up_kernel, sub=sub, seq_len=rows.seq_len)
    return pl.pallas_call(
        kern,
        grid=(n // tm, ncol),
        in_specs=[pl.BlockSpec((tm, d), lambda i, j: (i, 0), pipeline_mode=pl.Buffered(1)),
                  rows.mod_spec(d, lambda i, j: 0),
                  pl.BlockSpec((1, d), lambda i, j: (0, 0)),
                  pl.BlockSpec((d, tn), lambda i, j: (0, j)),
                  pl.BlockSpec((d, tn), lambda i, j: (0, ncol + j)),
                  pl.BlockSpec((CONV_W, tn), lambda i, j: (0, j)),
                  pl.BlockSpec((1, tn), lambda i, j: (0, j)),
                  prev_spec, prev_spec],
        out_specs=[pl.BlockSpec((tm, tn), lambda i, j: (i, j)),
                   pl.BlockSpec((tail_rows, tn), lambda i, j: (i, j))],
        out_shape=[jax.ShapeDtypeStruct((n, dff), BF16),
                   jax.ShapeDtypeStruct((n // tm * tail_rows, dff), F32)],
        scratch_shapes=[pltpu.VMEM((tm, d), BF16),
                        pltpu.VMEM((ncol, SUBLANES, tn), F32)],
        compiler_params=_params(("arbitrary", "arbitrary")),
        name="up_proj",
    )(x1, rows.mod, norm_g.reshape(1, d), wup_bf, wup_bf, conv_w, conv_b.reshape(1, dff), p1, p2)


def _down_kernel(f_ref, w_ref, x_ref, mod_ref, g_ref, o_ref, acc_scr, *, n_k):
    k = pl.program_id(1)

    @pl.when(k == 0)
    def _():
        acc_scr[...] = jnp.zeros_like(acc_scr)

    acc_scr[...] += jnp.dot(f_ref[...], w_ref[...], preferred_element_type=F32)

    @pl.when(k == n_k - 1)
    def _():
        x2 = x_ref[...] + _mod_rows(mod_ref, 5, 0, x_ref.shape[0]) * acc_scr[...]
        ms = jnp.mean(x2 * x2, axis=-1, keepdims=True)
        o_ref[...] = (x2 * lax.rsqrt(ms + RMS_EPS)) * g_ref[...]


def _down(f, wd_bf, x1, rows, normf_g, tk):
    n, dff = f.shape
    d = x1.shape[1]
    tm = rows.tm
    n_k = dff // tk
    return pl.pallas_call(
        functools.partial(_down_kernel, n_k=n_k),
        grid=(n // tm, n_k),
        in_specs=[pl.BlockSpec((tm, tk), lambda i, k: (i, k)),
                  pl.BlockSpec((tk, d), lambda i, k: (k, 0)),
                  pl.BlockSpec((tm, d), lambda i, k: (i, 0)),
                  rows.mod_spec(d, lambda i, k: 0),
                  pl.BlockSpec((1, d), lambda i, k: (0, 0))],
        out_specs=pl.BlockSpec((tm, d), lambda i, k: (i, 0)),
        out_shape=jax.ShapeDtypeStruct((n, d), F32),
        scratch_shapes=[pltpu.VMEM((tm, d), F32)],
        compiler_params=_params(("arbitrary", "arbitrary")),
        name="down_proj",
    )(f, wd_bf, x1, rows.mod, normf_g.reshape(1, d))


def _rope_tables(pos):
    half = HEAD_B // 2
    inv = ROPE_THETA ** (-jnp.arange(half, dtype=F32) / half)
    ang = pos.astype(F32)[:, None] * inv[None, :]
    cos, sin = jnp.cos(ang), jnp.sin(ang)
    return jnp.concatenate([cos, cos], axis=1), jnp.concatenate([-sin, sin], axis=1)


def _group(x, mod, pos, wts, shift0, wkv0, conv_prev, cache, tm_target):
    bsz, t_len, d = x.shape
    n = bsz * t_len
    lay = wts["layout"]
    tn, za, zap, da, db, dff = lay["tn"], lay["za"], lay["zap"], lay["da"], lay["db"], lay["dff"]
    x2 = x.reshape(n, d)
    per_row = cache is not None
    if per_row:
        tm = tm_big = n
        mod4 = jnp.repeat(mod.reshape(bsz, 6, d), t_len, axis=0).swapaxes(0, 1)[None]
        cos_t, sin_t = _rope_tables(jnp.tile(pos, bsz))
    else:
        tm = _pick(t_len, tm_target, SUBLANES)
        tm_big = _pick(t_len, 2 * tm_target, SUBLANES)
        mod4 = mod.reshape(bsz, 6, 1, d)
        cos_t, sin_t = _rope_tables(pos)
    rows = _Rows(n, t_len, tm, mod4)
    rows_big = _Rows(n, t_len, tm_big, mod4)

    q_off = zap
    z, k4, v4 = _in_proj(x2, rows_big, wts["norm1_g"], wts["w_in"], cos_t, sin_t, tn, q_off // tn, db // tn)
    zp = z.shape[1]
    z3 = z.reshape(bsz, t_len, zp)

    c_len = _pick(t_len, 64, SUBLANES) if t_len >= SUBLANES else SUBLANES
    t_pad = _round_up(t_len, c_len)
    z3a = z3 if t_pad == t_len else jnp.pad(z3[:, :, :zap], ((0, 0), (0, t_pad - t_len), (0, 0)))
    shift_p = jnp.pad(shift0, ((0, 0), (0, zap - za)))[:, None, :]
    yag, wkv = _rwkv(z3a, zap, shift_p, wkv0, wts["mu"], wts["vecs"],
                     wts["w2p"], wts["a2p"], wts["g2p"], c_len, min(t_len, c_len), da)
    yag = yag[:, :t_len].reshape(n, da)
    new_shift = z3[:, t_len - 1, :za]

    n_heads = db // HEAD_B
    k_out = k4.reshape(bsz, t_len, n_heads, HEAD_B)
    v_out = v4.reshape(bsz, t_len, n_heads, HEAD_B)
    if cache is None:
        attn = _attn_prompt(z3, q_off // HEAD_B, (q_off + db) // HEAD_B, (q_off + 2 * db) // HEAD_B, n_heads)
    else:
        cache_k, cache_v, page_table = cache
        q4 = z3[:, :, q_off:q_off + db].reshape(bsz, t_len, n_heads, HEAD_B)
        attn = _attn_sample(q4, k_out, v_out, cache_k, cache_v, page_table)
    attn = attn.reshape(n, db)

    mix = _mix(yag, attn, wts["w_proj_a"], wts["w_proj_b"], z, (q_off + 3 * db) // tn,
               (q_off + 3 * db + d) // tn, tm_big, tn)
    x1 = _out_proj(x2, mix, wts["w_out"], rows_big, _pick(d, 1024, tn))

    if per_row:
        zeros = jnp.zeros((bsz, t_len - 1, dff), F32)
        p1 = jnp.concatenate([conv_prev[:, 1:2], zeros], axis=1).reshape(n, dff)
        p2 = jnp.concatenate([conv_prev, zeros[:, :t_len - 2]], axis=1).reshape(n, dff)
    else:
        p1 = p2 = jnp.zeros((1, dff), F32)
    f, tail = _up(x1, rows_big, wts["norm2_g"], wts["w_up"], wts["conv_w"], wts["conv_b"], p1, p2, tn, per_row)
    if per_row:
        new_conv = tail.reshape(bsz, t_len, dff)[:, t_len - (CONV_W - 1):]
    else:
        tps = t_len // tm_big
        new_conv = tail.reshape(bsz, tps, SUBLANES, dff)[:, -1, SUBLANES - (CONV_W - 1):]
    y = _down(f, wts["w_down"], x1, rows, wts["normf_g"], _pick(dff, 1408, LANES)).reshape(bsz, t_len, d)
    return y, k_out, v_out, wkv, new_shift, new_conv


def kernel(x_prompt, x_sample, cache_k, cache_v, state_wkv, state_shift, state_conv, page_table, c_prompt, c_sample, w_ada, b_ada, norm1_g, w_in, mu_shift, rwkv_w0, rwkv_w2, rwkv_a0, rwkv_a2, rwkv_g2, rwkv_kk, rwkv_ka, rwkv_rk, lnx_g, lnx_b, w_proj_a, w_proj_b, w_out, norm2_g, w_up, conv_w, conv_b, w_down, normf_g):
    d = x_prompt.shape[-1]
    da, db = w_proj_a.shape[0], w_proj_b.shape[0]
    za = mu_shift.shape[0]
    dff = conv_w.shape[1]
    lora_g = rwkv_g2.shape[0]
    assert rwkv_w2.shape[0] == LORA_W and rwkv_a2.shape[0] == LORA_A and LORA_W + LORA_A == LANES
    assert za == 3 * da + LORA_W + LORA_A + lora_g and da % LANES == 0 and db % HEAD_B == 0
    tn = 512 if all(v % 512 == 0 for v in (db, d, dff)) else LANES
    zap = _round_up(za, tn)
    gw = zap - 3 * da - LANES
    assert gw >= lora_g

    zeros_l = jnp.zeros((LORA_W, da), F32)
    vecs = jnp.stack([rwkv_w0, rwkv_a0, rwkv_kk, rwkv_ka, rwkv_rk.reshape(da), lnx_g, lnx_b,
                      jnp.zeros((da,), F32)])
    wts = {
        "layout": dict(tn=tn, za=za, zap=zap, da=da, db=db, dff=dff),
        "norm1_g": norm1_g, "norm2_g": norm2_g, "normf_g": normf_g,
        "w_in": _w_in_prep(w_in.T, za, zap, tn),
        "mu": jnp.pad(mu_shift, (0, zap - za)).reshape(1, zap),
        "vecs": vecs,
        "w2p": jnp.concatenate([rwkv_w2, zeros_l], axis=0).astype(BF16),
        "a2p": jnp.concatenate([zeros_l, rwkv_a2], axis=0).astype(BF16),
        "g2p": jnp.pad(rwkv_g2, ((0, gw - lora_g), (0, 0))).astype(BF16),
        "w_proj_a": w_proj_a.astype(BF16), "w_proj_b": w_proj_b.astype(BF16),
        "w_out": w_out.astype(BF16), "w_up": w_up, "w_down": w_down.astype(BF16),
        "conv_w": conv_w, "conv_b": conv_b,
    }

    n_p, t_p = x_prompt.shape[:2]
    n_s, t_s = x_sample.shape[:2]
    n_c = _round_up(n_p + n_s, SUBLANES)
    c_all = jnp.concatenate([c_prompt, c_sample, jnp.zeros((n_c - n_p - n_s, d), F32)], axis=0)
    mod = _ada(c_all, w_ada, b_ada)

    n_heads_a = da // HEAD_A
    out_p = _group(x_prompt, mod[:n_p], jnp.arange(t_p, dtype=jnp.int32), wts,
                   jnp.zeros((n_p, za), F32), jnp.zeros((n_p, n_heads_a, HEAD_A, HEAD_A), F32),
                   None, None, 512)
    past_len = page_table.shape[1] * PAGE_SIZE
    out_s = _group(x_sample, mod[n_p:n_p + n_s], past_len + jnp.arange(t_s, dtype=jnp.int32), wts,
                   state_shift, state_wkv, state_conv, (cache_k, cache_v, page_table), 512)
    y_p, k_p, v_p, wkv_p, sh_p, cv_p = out_p
    y_s, k_s, v_s, wkv_s, sh_s, cv_s = out_s
    return (y_p, y_s, k_p, v_p, wkv_p, sh_p, cv_p, k_s, v_s, wkv_s, sh_s, cv_s)
```

```python
import functools

import jax
import jax.numpy as jnp
from jax import lax
from jax.experimental import pallas as pl
from jax.experimental.pallas import tpu as pltpu

F32 = jnp.float32
BF16 = jnp.bfloat16

HEAD_A = 64
HEAD_B = 128
MOBA_BLOCK = 256
MOBA_TOPK = 3
PAGE_SIZE = 128
ROPE_THETA = 10000.0
LNX_EPS = 64e-5
RMS_EPS = 1e-6
NEG_INF = -1e30
CONV_W = 3
LORA_W = 64
LORA_A = 64
LOG2_E = 1.4426950408889634

LANES = 128
SUBLANES = 8
VMEM_LIMIT = 52 * 1024 * 1024


def _round_up(x, m):
    return (x + m - 1) // m * m


def _pick(n, target, align):
    if n <= target:
        return n
    t = target - target % align
    while t >= align:
        if n % t == 0:
            return t
        t -= align
    return n


def _params(sem):
    return pltpu.CompilerParams(dimension_semantics=sem, vmem_limit_bytes=VMEM_LIMIT)


def _split3(x):
    hi = x.astype(BF16)
    r1 = x - hi.astype(F32)
    mid = r1.astype(BF16)
    lo = (r1 - mid.astype(F32)).astype(BF16)
    return hi, mid, lo


def _dg(a, b, dims):
    return lax.dot_general(a, b, (dims, ((), ())), preferred_element_type=F32)


_NN = ((1,), (0,))
_NT = ((1,), (1,))
_TN = ((0,), (0,))


def _dot_pieces(a_pieces, b_pieces, dims):
    ca, cb = dims[0][0], dims[1][0]
    if a_pieces[0].shape[ca] % LANES == 0:
        return _dg(jnp.concatenate(a_pieces, axis=ca), jnp.concatenate(b_pieces, axis=cb), dims)
    out = _dg(a_pieces[0], b_pieces[0], dims)
    for x, y in zip(a_pieces[1:], b_pieces[1:]):
        out = out + _dg(x, y, dims)
    return out


def _dotx(a, b, dims=_NN):
    ah, am, _ = _split3(a)
    bh, bm, _ = _split3(b)
    return _dot_pieces([ah, ah, am], [bh, bm, bh], dims)


def _dot1(a, b, dims=_NN):
    return _dg(a.astype(BF16), b.astype(BF16), dims)


def _dot01(a01, b, dims=_NN):
    return _dot_pieces([a01, a01, a01], list(_split3(b)), dims)


def _dotr01(a, b01, dims=_NN):
    return _dot_pieces(list(_split3(a)), [b01, b01, b01], dims)


def _sigmoid(x):
    return 1.0 / (1.0 + jnp.exp(-x))


def _softplus(x):
    return jnp.maximum(x, 0.0) + jnp.log(1.0 + jnp.exp(-jnp.abs(x)))


def _norm_mod(x, g, shift, scale):
    ms = jnp.mean(x * x, axis=-1, keepdims=True)
    y = x * lax.rsqrt(ms + RMS_EPS)
    return (y * g) * (1.0 + scale) + shift


def _mod_rows(mod_ref, idx, lo, n):
    if mod_ref.shape[1] == 1:
        return mod_ref[idx]
    return mod_ref[idx, lo:lo + n, :]


def _ada_kernel(c_ref, w_ref, b_ref, o_ref):
    c = c_ref[...]
    s = c * _sigmoid(c)
    o_ref[...] = _dotx(s, w_ref[...]) + b_ref[...]


def _ada(c_all, w_ada, b_ada):
    n, d = c_all.shape
    n6 = w_ada.shape[1]
    tn = _pick(n6, 512, LANES)
    return pl.pallas_call(
        _ada_kernel,
        grid=(n6 // tn,),
        in_specs=[pl.BlockSpec((n, d), lambda j: (0, 0)),
                  pl.BlockSpec((d, tn), lambda j: (0, j)),
                  pl.BlockSpec((1, tn), lambda j: (0, j))],
        out_specs=pl.BlockSpec((n, tn), lambda j: (0, j)),
        out_shape=jax.ShapeDtypeStruct((n, n6), F32),
        compiler_params=_params(("arbitrary",)),
        name="ada",
    )(c_all, w_ada, b_ada.reshape(1, n6))


class _Rows:
    def __init__(self, n_rows, seq_len, tm, mod):
        self.n_rows, self.seq_len, self.tm, self.mod = n_rows, seq_len, tm, mod
        self.per_row = mod.shape[2] != 1
        if not self.per_row:
            assert seq_len % tm == 0
        else:
            assert tm == n_rows
        self.n_tiles = n_rows // tm
        self.tiles_per_seq = max(seq_len // tm, 1)

    def mod_spec(self, width, col_of):
        r = self.tm if self.per_row else 1
        tps = self.tiles_per_seq
        if self.per_row:
            return pl.BlockSpec((None, 6, r, width), lambda i, j: (0, 0, i, col_of(i, j)))
        return pl.BlockSpec((None, 6, r, width), lambda i, j: (i // tps, 0, 0, col_of(i, j)))


def _w_in_prep_kernel(*refs, q_lo, shift):
    j = pl.program_id(0)
    o_ref = refs[-1]
    wide = jnp.concatenate([r[...] for r in refs[:-1]], axis=0)
    n_out = o_ref.shape[0]

    @pl.when(j < q_lo)
    def _():
        o_ref[...] = wide[:n_out].astype(BF16)

    @pl.when(j >= q_lo)
    def _():
        o_ref[...] = wide[shift:shift + n_out].astype(BF16)


def _w_in_prep(w_t, za, zap, tn):
    z_in, d = w_t.shape
    q_lo = zap // tn
    n_tiles = q_lo + (z_in - za) // tn
    per_tile = tn // LANES
    first_b, shift = za // LANES, za % LANES
    assert shift % SUBLANES == 0 and z_in >= zap and (z_in - za) % tn == 0

    def spec(t):
        def index(j):
            return (jnp.where(j < q_lo, j * per_tile, first_b + (j - q_lo) * per_tile) + t, 0)
        return pl.BlockSpec((LANES, d), index)

    return pl.pallas_call(
        functools.partial(_w_in_prep_kernel, q_lo=q_lo, shift=shift),
        grid=(n_tiles,),
        in_specs=[spec(t) for t in range(per_tile + 1)],
        out_specs=pl.BlockSpec((tn, d), lambda j: (j, 0)),
        out_shape=jax.ShapeDtypeStruct((n_tiles * tn, d), BF16),
        compiler_params=_params(("arbitrary",)),
        name="w_in_prep",
    )(*([w_t] * (per_tile + 1)))


def _in_kernel(x_ref, mod_ref, g_ref, w_ref, cos_ref, sin_ref, o_ref, k4_ref, v4_ref, h_scr,
               *, sub, q_lo, n_qkv):
    j = pl.program_id(1)
    tm = x_ref.shape[0]

    @pl.when(j == 0)
    def _():
        for s in range(tm // sub):
            lo = s * sub
            h = _norm_mod(x_ref[lo:lo + sub, :], g_ref[...],
                          _mod_rows(mod_ref, 0, lo, sub), _mod_rows(mod_ref, 1, lo, sub))
            h_scr[lo:lo + sub, :] = h.astype(BF16)

    heads_per_tile = w_ref.shape[0] // HEAD_B

    def emit(rope, head_ref, tile):
        acc = _dg(h_scr[...], w_ref[...], _NT)
        for c in range(heads_per_tile):
            a = acc[:, c * HEAD_B:(c + 1) * HEAD_B]
            if rope:
                a = a * cos_ref[...] + pltpu.roll(a, HEAD_B // 2, 1) * sin_ref[...]
            o_ref[:, c * HEAD_B:(c + 1) * HEAD_B] = a
            if head_ref is not None:
                head_ref[:, tile * heads_per_tile + c, :] = a

    for t in range(n_qkv):
        pl.when(j == q_lo + t)(functools.partial(emit, True, None, t))
        pl.when(j == q_lo + n_qkv + t)(functools.partial(emit, True, k4_ref, t))
        pl.when(j == q_lo + 2 * n_qkv + t)(functools.partial(emit, False, v4_ref, t))

    @pl.when(jnp.logical_or(j < q_lo, j >= q_lo + 3 * n_qkv))
    def _():
        o_ref[...] = _dg(h_scr[...], w_ref[...], _NT)


def _in_proj(x2, rows, norm_g, w_bf, cos_t, sin_t, tn, q_lo, n_qkv):
    n, d = x2.shape
    zp = w_bf.shape[0]
    tm = rows.tm
    n_tab = cos_t.shape[0] // tm
    sub = _pick(tm, 128, SUBLANES)
    n_heads = n_qkv * tn // HEAD_B
    head_spec = pl.BlockSpec((tm, n_heads, HEAD_B), lambda i, j: (i, 0, 0))
    head_shape = jax.ShapeDtypeStruct((n, n_heads, HEAD_B), F32)
    return pl.pallas_call(
        functools.partial(_in_kernel, sub=sub, q_lo=q_lo, n_qkv=n_qkv),
        grid=(n // tm, zp // tn),
        in_specs=[pl.BlockSpec((tm, d), lambda i, j: (i, 0), pipeline_mode=pl.Buffered(1)),
                  rows.mod_spec(d, lambda i, j: 0),
                  pl.BlockSpec((1, d), lambda i, j: (0, 0)),
                  pl.BlockSpec((tn, d), lambda i, j: (j, 0)),
                  pl.BlockSpec((tm, HEAD_B), lambda i, j: (i % n_tab, 0)),
                  pl.BlockSpec((tm, HEAD_B), lambda i, j: (i % n_tab, 0))],
        out_specs=[pl.BlockSpec((tm, tn), lambda i, j: (i, j)), head_spec, head_spec],
        out_shape=[jax.ShapeDtypeStruct((n, zp), F32), head_shape, head_shape],
        scratch_shapes=[pltpu.VMEM((tm, d), BF16)],
        compiler_params=_params(("arbitrary", "arbitrary")),
        name="in_proj",
    )(x2, rows.mod, norm_g.reshape(1, d), w_bf, cos_t, sin_t)


def _unit_lower_inverse(low, n, c, blk):
    rr = lax.broadcasted_iota(jnp.int32, (n, n), 0)
    cc = lax.broadcasted_iota(jnp.int32, (n, n), 1)
    eye = (rr == cc).astype(F32)
    same = (rr // blk) == (cc // blk)
    dpart = [jnp.where(same, x, 0.0) for x in low]
    inv = [eye - d for d in dpart]
    p = dpart
    k = 2
    while k < blk:
        p = [_dot1(x, x) for x in p]
        inv = [i + _dot1(i, x) for i, x in zip(inv, p)]
        k *= 2
    if c > blk:
        m = [_dot1(i, x - d) for i, x, d in zip(inv, low, dpart)]
        minv = [eye - x for x in m]
        p = m
        k = 2
        while k < c // blk:
            p = [_dot1(x, x) for x in p]
            minv = [i + _dot1(i, x) for i, x in zip(minv, p)]
            k *= 2
        inv = [_dot1(mi, i) for mi, i in zip(minv, inv)]
    return inv


def _rwkv_kernel(z_ref, shift0_ref, s0_ref, mu_ref, vec_ref, w2_ref, a2_ref, g2_ref,
                 yag_ref, sout_ref, state_scr, prev_scr, *, c_len, t_valid, da, n_chunks):
    ci = pl.program_id(1)
    n_pairs = da // LANES
    c2 = 2 * c_len

    @pl.when(ci == 0)
    def _():
        zero = jnp.zeros((HEAD_A, HEAD_A), F32)
        for p in range(n_pairs):
            top = jnp.concatenate([s0_ref[2 * p], zero], axis=1)
            bot = jnp.concatenate([zero, s0_ref[2 * p + 1]], axis=1)
            state_scr[p] = jnp.concatenate([top, bot], axis=0)
        prev_scr[0:1, :] = shift0_ref[...]

    za = z_ref[...]
    row = lax.broadcasted_iota(jnp.int32, (c_len, 1), 0)
    zprev = jnp.where(row == 0, prev_scr[0:1, :], pltpu.roll(za, 1, 0))
    prev_scr[0:1, :] = za[c_len - 1:c_len, :]
    zmix = za + (zprev - za) * mu_ref[...]

    x_wa = zmix[:, 3 * da:3 * da + LANES]
    lane = lax.broadcasted_iota(jnp.int32, (c_len, LANES), 1)
    act_wa = jnp.where(lane < LORA_W, jnp.tanh(x_wa), x_wa)
    sig_g = _sigmoid(zmix[:, 3 * da + LANES:])
    act_wa = act_wa.astype(BF16)
    lw = _dg(act_wa, w2_ref[...], _NN)
    la = _dg(act_wa, a2_ref[...], _NN)
    g = _dg(sig_g.astype(BF16), g2_ref[...], _NN)

    valid = row < t_valid
    head_lo = lane < HEAD_A
    ones_seg = ((lax.broadcasted_iota(jnp.int32, (LANES, LANES), 0) // HEAD_A)
                == (lax.broadcasted_iota(jnp.int32, (LANES, LANES), 1) // HEAD_A)).astype(BF16)
    tr = lax.broadcasted_iota(jnp.int32, (c_len, c_len), 0)
    tc = lax.broadcasted_iota(jnp.int32, (c_len, c_len), 1)
    tri = (tc <= tr).astype(BF16)
    rr = lax.broadcasted_iota(jnp.int32, (c2, c2), 0)
    cc = lax.broadcasted_iota(jnp.int32, (c2, c2), 1)
    same_head = (rr // c_len) == (cc // c_len)
    strict = jnp.logical_and(same_head, (cc % c_len) < (rr % c_len))
    incl = jnp.logical_and(same_head, (cc % c_len) <= (rr % c_len))
    pairs = range(n_pairs)

    def lanes(x, p):
        return x[:, p * LANES:(p + 1) * LANES]

    def head_sums(x):
        xs = jnp.concatenate([lanes(x, p) for p in pairs], axis=0)
        s = _dotr01(xs, ones_seg)
        return jnp.concatenate([s[p * c_len:(p + 1) * c_len] for p in pairs], axis=1)

    def stack(x, p):
        xp = lanes(x, p)
        return jnp.concatenate([jnp.where(head_lo, xp, 0.0), jnp.where(head_lo, 0.0, xp)], axis=0)

    r = zmix[:, 0:da]
    ka = zmix[:, da:2 * da]
    v = zmix[:, 2 * da:3 * da]
    w0, a0, kkw, kaw, rkw, lng, lnb = (vec_ref[i:i + 1, :] for i in range(7))
    w_log = -_softplus(-(w0 + lw)) - 0.5
    logw = jnp.where(valid, -jnp.exp(w_log), 0.0)
    a = _sigmoid(a0 + la)
    kk = ka * kkw
    kk = kk * lax.rsqrt(jnp.maximum(head_sums(kk * kk), 1e-24))
    k_mod = ka * (1.0 + (a - 1.0) * kaw)
    bonus = head_sums(r * k_mod * rkw) * v
    b = jnp.where(valid, kk * a, 0.0)
    k_s = jnp.where(valid, k_mod, 0.0)

    cum = _dot01(tri, logw)
    cum_end = cum[c_len - 1:c_len, :]
    e_neg = jnp.exp(-cum)
    e_end = jnp.exp(cum_end - cum)
    gam_end = jnp.exp(cum_end)
    kt = kk * jnp.exp(cum - logw)
    rt = r * jnp.exp(cum)
    kd = k_s * e_neg
    bd = b * e_neg
    ke = k_s * e_end
    be = b * e_end

    xs = [jnp.concatenate([stack(kt, p), stack(rt, p)], axis=0) for p in pairs]
    ws = [jnp.concatenate([stack(kd, p), stack(bd, p)], axis=0) for p in pairs]
    es = [jnp.concatenate([stack(ke, p), stack(be, p)], axis=0) for p in pairs]
    vs = [stack(v, p) for p in pairs]
    s0 = [state_scr[p] for p in pairs]
    gm = [_dot1(x, w, _NT) for x, w in zip(xs, ws)]
    xs0 = [_dot1(x, s, _NT) for x, s in zip(xs, s0)]
    l_k = [jnp.where(strict, g_[:c2, :c2], 0.0) for g_ in gm]
    l_b = [jnp.where(strict, g_[:c2, c2:], 0.0) for g_ in gm]
    a_kb = [jnp.concatenate([jnp.where(incl, g_[c2:, :c2], 0.0), jnp.where(incl, -g_[c2:, c2:], 0.0)], axis=1)
            for g_ in gm]
    tinv = _unit_lower_inverse(l_b, c2, c_len, min(16, c_len))
    rhs = [x0[:c2] + _dot1(lk, v_) for x0, lk, v_ in zip(xs0, l_k, vs)]
    u = [_dot1(t, r_) for t, r_ in zip(tinv, rhs)]
    res = [r_ - (u_ + _dotx(lb, u_)) for r_, u_, lb in zip(rhs, u, l_b)]
    u = [u_ + _dot1(t, e_) for u_, t, e_ in zip(u, tinv, res)]
    y2 = [x0[c2:] + _dot1(ab, jnp.concatenate([v_, u_], axis=0)) for x0, ab, v_, u_ in zip(xs0, a_kb, vs, u)]
    for p in pairs:
        upd = _dot1(jnp.concatenate([vs[p], -u[p]], axis=0), es[p], _TN)
        state_scr[p] = s0[p] * lanes(gam_end, p) + upd
    y = jnp.concatenate([y_[:c_len] + y_[c_len:] for y_ in y2], axis=1)

    yc = y - head_sums(y) * (1.0 / HEAD_A)
    var = head_sums(yc * yc) * (1.0 / HEAD_A)
    ya = yc * lax.rsqrt(var + LNX_EPS) * lng + lnb + bonus
    yag_ref[...] = (ya * g).astype(BF16)

    @pl.when(ci == n_chunks - 1)
    def _():
        for p in range(n_pairs):
            s_pair = state_scr[p]
            sout_ref[2 * p] = s_pair[:HEAD_A, :HEAD_A]
            sout_ref[2 * p + 1] = s_pair[HEAD_A:, HEAD_A:]


def _rwkv(z3, zap, shift0, s0, mu_p, vecs, w2p, a2p, g2p, c_len, t_valid, da):
    bsz, tp, _ = z3.shape
    n_chunks = tp // c_len
    n_pairs = da // LANES
    n_heads = da // HEAD_A
    gw = zap - 3 * da - LANES
    kern = functools.partial(_rwkv_kernel, c_len=c_len, t_valid=t_valid, da=da, n_chunks=n_chunks)
    state_spec = pl.BlockSpec((None, n_heads, HEAD_A, HEAD_A), lambda b, c: (b, 0, 0, 0))
    return pl.pallas_call(
        kern,
        grid=(bsz, n_chunks),
        in_specs=[pl.BlockSpec((None, c_len, zap), lambda b, c: (b, c, 0)),
                  pl.BlockSpec((None, 1, zap), lambda b, c: (b, 0, 0)),
                  state_spec,
                  pl.BlockSpec((1, zap), lambda b, c: (0, 0)),
                  pl.BlockSpec((SUBLANES, da), lambda b, c: (0, 0)),
                  pl.BlockSpec((LANES, da), lambda b, c: (0, 0)),
                  pl.BlockSpec((LANES, da), lambda b, c: (0, 0)),
                  pl.BlockSpec((gw, da), lambda b, c: (0, 0))],
        out_specs=[pl.BlockSpec((None, c_len, da), lambda b, c: (b, c, 0)), state_spec],
        out_shape=[jax.ShapeDtypeStruct((bsz, tp, da), BF16),
                   jax.ShapeDtypeStruct((bsz, n_heads, HEAD_A, HEAD_A), F32)],
        scratch_shapes=[pltpu.VMEM((n_pairs, LANES, LANES), F32),
                        pltpu.VMEM((SUBLANES, zap), F32)],
        compiler_params=_params(("arbitrary", "arbitrary")),
        name="rwkv",
    )(z3, shift0, s0, mu_p, vecs, w2p, a2p, g2p)


def _topk_mask(gate, valid, n_cand, axis):
    gm = jnp.where(valid, gate, NEG_INF)
    idx = lax.broadcasted_iota(jnp.int32, gate.shape, axis)
    cnt = jnp.zeros(gate.shape, jnp.int32)
    for m in range(n_cand):
        g_m = gm[:, m:m + 1] if axis == 1 else gm[m:m + 1, :]
        ahead = jnp.logical_or(g_m > gm, jnp.logical_and(g_m == gm, m < idx))
        cnt = cnt + ahead.astype(jnp.int32)
    return jnp.logical_and(valid, cnt < MOBA_TOPK)


def _attn_prompt_kernel(q_ref, qall_ref, k_ref, v_ref, o_ref, kmean_scr, vt_scr, sel_scr, *, n_blk):
    i = pl.program_id(2)
    blk = MOBA_BLOCK
    scale = HEAD_B ** -0.5

    @pl.when(i == 0)
    def _():
        kmean_scr[...] = jnp.zeros_like(kmean_scr)
        for n in range(n_blk):
            rows = slice(n * blk, (n + 1) * blk)
            kmean_scr[n:n + 1, :] = jnp.sum(k_ref[rows, :], axis=0, keepdims=True) * (1.0 / blk)
            vt_scr[:, rows] = jnp.transpose(v_ref[rows, :]).astype(BF16)
        gate = _dotx(kmean_scr[...], qall_ref[...], _NT)[0:sel_scr.shape[1], :]
        blk_idx = lax.broadcasted_iota(jnp.int32, gate.shape, 0)
        own_blk = lax.broadcasted_iota(jnp.int32, gate.shape, 1) // blk
        sel_all = _topk_mask(gate, blk_idx < own_blk, n_blk, 0).astype(F32)
        for n in range(n_blk):
            sel_scr[n] = sel_all[:, n * blk:(n + 1) * blk]

    qb = (q_ref[...] * (scale * LOG2_E)).astype(BF16)
    sel = sel_scr[i]
    ki = lax.broadcasted_iota(jnp.int32, (blk, blk), 0)
    qi = lax.broadcasted_iota(jnp.int32, (blk, blk), 1)

    def attend(own):
        width = (own + 1) * blk
        s = _dg(k_ref[0:width, :].astype(BF16), qb, _NT)
        parts = [jnp.where(sel[n:n + 1, :] > 0.0, s[n * blk:(n + 1) * blk, :], NEG_INF) for n in range(own)]
        parts.append(jnp.where(ki <= qi, s[own * blk:, :], NEG_INF))
        m = functools.reduce(jnp.maximum, [jnp.max(x, axis=0, keepdims=True) for x in parts])
        probs = [jnp.exp2(x - m) for x in parts]
        l = functools.reduce(lambda a, b: a + b, [jnp.sum(x, axis=0, keepdims=True) for x in probs])
        pcat = jnp.concatenate([x.astype(BF16) for x in probs], axis=0)
        acc = _dg(vt_scr[:, 0:width], pcat, _NN)
        o_ref[...] = jnp.transpose(acc / l).astype(BF16)

    for own in range(n_blk):
        pl.when(i == own)(functools.partial(attend, own))


def _attn_prompt(z3, q_col, k_col, v_col, n_heads):
    bsz, t_len, _ = z3.shape
    n_blk = t_len // MOBA_BLOCK
    kern = functools.partial(_attn_prompt_kernel, n_blk=n_blk)
    return pl.pallas_call(
        kern,
        grid=(bsz, n_heads, n_blk),
        in_specs=[pl.BlockSpec((None, MOBA_BLOCK, HEAD_B), lambda b, h, i: (b, i, q_col + h)),
                  pl.BlockSpec((None, t_len, HEAD_B), lambda b, h, i: (b, 0, q_col + h)),
                  pl.BlockSpec((None, t_len, HEAD_B), lambda b, h, i: (b, 0, k_col + h)),
                  pl.BlockSpec((None, t_len, HEAD_B), lambda b, h, i: (b, 0, v_col + h))],
        out_specs=pl.BlockSpec((None, MOBA_BLOCK, HEAD_B), lambda b, h, i: (b, i, h)),
        out_shape=jax.ShapeDtypeStruct((bsz, t_len, n_heads * HEAD_B), BF16),
        scratch_shapes=[pltpu.VMEM((LANES, HEAD_B), F32),
                        pltpu.VMEM((HEAD_B, t_len), BF16),
                        pltpu.VMEM((n_blk, _round_up(n_blk, SUBLANES), MOBA_BLOCK), F32)],
        compiler_params=_params(("arbitrary", "arbitrary", "arbitrary")),
        name="attn_prompt",
    )(z3, z3, z3, z3)


def _attn_sample_kernel(pt_ref, q_ref, knew_ref, vnew_ref, *refs, n_pages, t_new, n_heads, grp):
    del pt_ref
    ck_refs, cv_refs = refs[:grp], refs[grp:2 * grp]
    o_ref, newk_scr, newv_scr, ksum_scr, sc_scr, sel_scr, acc_scr, l_scr = refs[2 * grp:]
    n_grp = n_pages // grp
    j = pl.program_id(1)
    n_past_blk = n_pages * PAGE_SIZE // MOBA_BLOCK
    pages_per_blk = MOBA_BLOCK // PAGE_SIZE
    n_col = t_new * n_heads
    n_key = PAGE_SIZE * n_heads
    scale = HEAD_B ** -0.5
    ci = lax.broadcasted_iota(jnp.int32, (n_col, n_key), 0)
    li = lax.broadcasted_iota(jnp.int32, (n_col, n_key), 1)
    diag = (li % n_heads) == (ci % n_heads)
    lane = lax.broadcasted_iota(jnp.int32, (n_col, LANES), 1)

    @pl.when(j == 0)
    def _():
        newk_scr[...] = jnp.zeros_like(newk_scr)
        newv_scr[...] = jnp.zeros_like(newv_scr)
        newk_scr[0:t_new] = knew_ref[...]
        newv_scr[0:t_new] = vnew_ref[...]
        ksum_scr[...] = jnp.zeros_like(ksum_scr)
        acc_scr[...] = jnp.zeros_like(acc_scr)

    def score_page(kpage, page):
        k2 = kpage.reshape(n_key, HEAD_B).astype(BF16)
        sc_scr[page] = _dg(q_ref[...].astype(BF16), k2, _NT) * scale

    @pl.when(j < n_grp)
    def _():
        for g in range(grp):
            kpage = ck_refs[g][...]
            page = j * grp + g
            score_page(kpage, page)
            blk = page // pages_per_blk
            ksum_scr[blk] = ksum_scr[blk] + jnp.sum(kpage, axis=0)

    def block_selected(pg):
        blk = pg // pages_per_blk
        return jnp.max(jnp.where(lane == blk, sel_scr[...], 0.0), axis=1, keepdims=True) > 0.0

    @pl.when(j == n_grp)
    def _():
        score_page(newk_scr[...], n_pages)
        n_bh = n_past_blk * n_heads
        kmean = ksum_scr[...].reshape(n_bh, HEAD_B) * (1.0 / MOBA_BLOCK)
        gt = _dotx(q_ref[...], kmean, _NT)
        gci = lax.broadcasted_iota(jnp.int32, (n_col, n_bh), 0)
        gli = lax.broadcasted_iota(jnp.int32, (n_col, n_bh), 1)
        gm = jnp.where((gli % n_heads) == (gci % n_heads), gt, 0.0)
        pick = ((lax.broadcasted_iota(jnp.int32, (n_bh, LANES), 0) // n_heads)
                == lax.broadcasted_iota(jnp.int32, (n_bh, LANES), 1)).astype(BF16)
        gate = _dotr01(gm, pick)
        sel_scr[...] = _topk_mask(gate, lane < n_past_blk, n_past_blk, 1).astype(F32)

        key_row = li // n_heads
        new_ok = jnp.logical_and(diag, jnp.logical_and(key_row < t_new, key_row <= ci // n_heads))
        s_new = sc_scr[n_pages]

        def pg_max(pg, mm):
            ok = jnp.logical_and(block_selected(pg), diag)
            return jnp.maximum(mm, jnp.where(ok, sc_scr[pg], NEG_INF))

        mm = lax.fori_loop(0, n_pages, pg_max, jnp.where(new_ok, s_new, NEG_INF), unroll=grp)
        m = jnp.max(mm, axis=1, keepdims=True)
        p_new = jnp.where(new_ok, jnp.exp(s_new - m), 0.0)
        sc_scr[n_pages] = p_new

        def pg_exp(pg, ll):
            ok = jnp.logical_and(block_selected(pg), diag)
            pr = jnp.where(ok, jnp.exp(sc_scr[pg] - m), 0.0)
            sc_scr[pg] = pr
            return ll + pr

        ll = lax.fori_loop(0, n_pages, pg_exp, p_new, unroll=grp)
        l_scr[...] = jnp.broadcast_to(jnp.sum(ll, axis=1, keepdims=True), l_scr.shape)

    def pv_page(vpage, page):
        v2 = vpage.reshape(n_key, HEAD_B).astype(BF16)
        acc_scr[...] = acc_scr[...] + _dg(sc_scr[page].astype(BF16), v2, _NN)

    @pl.when(j >= n_grp)
    def _():
        for g in range(grp):
            pv_page(cv_refs[g][...], (j - n_grp) * grp + g)

    @pl.when(j == 2 * n_grp - 1)
    def _():
        pv_page(newv_scr[...], n_pages)
        o_ref[...] = (acc_scr[...] / l_scr[...]).astype(BF16)


def _attn_sample(q4, k4, v4, cache_k, cache_v, page_table):
    n_seq, t_new, n_heads, _ = q4.shape
    n_pages = page_table.shape[1]
    n_col = t_new * n_heads
    n_key = PAGE_SIZE * n_heads
    assert (n_pages * PAGE_SIZE) % MOBA_BLOCK == 0 and n_pages * PAGE_SIZE // MOBA_BLOCK <= LANES
    assert n_heads == SUBLANES and t_new <= PAGE_SIZE
    n_past_blk = n_pages * PAGE_SIZE // MOBA_BLOCK
    grp = _pick(n_pages, 16, 1)
    n_grp = n_pages // grp
    kern = functools.partial(_attn_sample_kernel, n_pages=n_pages, t_new=t_new, n_heads=n_heads, grp=grp)
    page_block = (None, PAGE_SIZE, n_heads, HEAD_B)

    def k_spec(g):
        return pl.BlockSpec(page_block, lambda b, j, pt: (pt[b, jnp.minimum(j, n_grp - 1) * grp + g], 0, 0, 0))

    def v_spec(g):
        return pl.BlockSpec(page_block, lambda b, j, pt: (pt[b, jnp.maximum(j - n_grp, 0) * grp + g], 0, 0, 0))

    grid_spec = pltpu.PrefetchScalarGridSpec(
        num_scalar_prefetch=1,
        grid=(n_seq, 2 * n_grp),
        in_specs=[pl.BlockSpec((None, n_col, HEAD_B), lambda b, j, pt: (b, 0, 0)),
                  pl.BlockSpec((None, t_new, n_heads, HEAD_B), lambda b, j, pt: (b, 0, 0, 0)),
                  pl.BlockSpec((None, t_new, n_heads, HEAD_B), lambda b, j, pt: (b, 0, 0, 0))]
                 + [k_spec(g) for g in range(grp)] + [v_spec(g) for g in range(grp)],
        out_specs=pl.BlockSpec((None, n_col, HEAD_B), lambda b, j, pt: (b, 0, 0)),
        scratch_shapes=[pltpu.VMEM((PAGE_SIZE, n_heads, HEAD_B), F32),
                        pltpu.VMEM((PAGE_SIZE, n_heads, HEAD_B), F32),
                        pltpu.VMEM((n_past_blk, n_heads, HEAD_B), F32),
                        pltpu.VMEM((n_pages + 1, n_col, n_key), F32),
                        pltpu.VMEM((n_col, LANES), F32),
                        pltpu.VMEM((n_col, HEAD_B), F32),
                        pltpu.VMEM((n_col, HEAD_B), F32)])
    return pl.pallas_call(
        kern,
        grid_spec=grid_spec,
        out_shape=jax.ShapeDtypeStruct((n_seq, n_col, HEAD_B), BF16),
        compiler_params=_params(("arbitrary", "arbitrary")),
        name="attn_sample",
    )(page_table, q4.reshape(n_seq, n_col, HEAD_B), k4, v4, *([cache_k] * grp), *([cache_v] * grp))


def _mix_kernel(ya_ref, ob_ref, wa_ref, wb_ref, ga_ref, gb_ref, o_ref):
    oa = jnp.dot(ya_ref[...], wa_ref[...], preferred_element_type=F32)
    ob = jnp.dot(ob_ref[...], wb_ref[...], preferred_element_type=F32)
    o_ref[...] = (_sigmoid(ga_ref[...]) * oa + _sigmoid(gb_ref[...]) * ob).astype(BF16)


def _mix(yag, attn, wa_bf, wb_bf, z, ga_col, gb_col, tm, tn):
    n, da = yag.shape
    db = attn.shape[1]
    d = wa_bf.shape[1]
    return pl.pallas_call(
        _mix_kernel,
        grid=(n // tm, d // tn),
        in_specs=[pl.BlockSpec((tm, da), lambda i, j: (i, 0)),
                  pl.BlockSpec((tm, db), lambda i, j: (i, 0)),
                  pl.BlockSpec((da, tn), lambda i, j: (0, j)),
                  pl.BlockSpec((db, tn), lambda i, j: (0, j)),
                  pl.BlockSpec((tm, tn), lambda i, j: (i, ga_col + j)),
                  pl.BlockSpec((tm, tn), lambda i, j: (i, gb_col + j))],
        out_specs=pl.BlockSpec((tm, tn), lambda i, j: (i, j)),
        out_shape=jax.ShapeDtypeStruct((n, d), BF16),
        compiler_params=_params(("arbitrary", "arbitrary")),
        name="mix",
    )(yag, attn, wa_bf, wb_bf, z, z)


def _out_kernel(x_ref, mix_ref, w_ref, mod_ref, o_ref):
    acc = jnp.dot(mix_ref[...], w_ref[...], preferred_element_type=F32)
    o_ref[...] = x_ref[...] + _mod_rows(mod_ref, 2, 0, x_ref.shape[0]) * acc


def _out_proj(x2, mix, w_bf, rows, tn):
    n, d = x2.shape
    tm = rows.tm
    return pl.pallas_call(
        _out_kernel,
        grid=(n // tm, d // tn),
        in_specs=[pl.BlockSpec((tm, tn), lambda i, j: (i, j)),
                  pl.BlockSpec((tm, d), lambda i, j: (i, 0)),
                  pl.BlockSpec((d, tn), lambda i, j: (0, j)),
                  rows.mod_spec(tn, lambda i, j: j)],
        out_specs=pl.BlockSpec((tm, tn), lambda i, j: (i, j)),
        out_shape=jax.ShapeDtypeStruct((n, d), F32),
        compiler_params=_params(("arbitrary", "arbitrary")),
        name="out_proj",
    )(x2, mix, w_bf, rows.mod)


def _gelu_tanh(x):
    return 0.5 * x * (1.0 + jnp.tanh(0.7978845608028654 * (x + 0.044715 * (x * x * x))))


def _up_kernel(x_ref, mod_ref, g_ref, wg_ref, wv_ref, cw_ref, cb_ref, p1_ref, p2_ref,
               f_ref, tail_ref, h_scr, carry_scr, *, sub, seq_len):
    i = pl.program_id(0)
    j = pl.program_id(1)
    tm = x_ref.shape[0]

    @pl.when(j == 0)
    def _():
        for s in range(tm // sub):
            lo = s * sub
            h = _norm_mod(x_ref[lo:lo + sub, :], g_ref[...],
                          _mod_rows(mod_ref, 3, lo, sub), _mod_rows(mod_ref, 4, lo, sub))
            h_scr[lo:lo + sub, :] = h.astype(BF16)

    @pl.when(i == 0)
    def _():
        carry_scr[j] = jnp.zeros(carry_scr.shape[1:], F32)

    hb = h_scr[...]
    ug = jnp.dot(hb, wg_ref[...].astype(BF16), preferred_element_type=F32)
    uv = jnp.dot(hb, wv_ref[...].astype(BF16), preferred_element_type=F32)
    row = lax.broadcasted_iota(jnp.int32, (tm, 1), 0)
    pos = (i * tm + row) % seq_len
    c0 = carry_scr[j, 0:1, :]
    c1 = carry_scr[j, 1:2, :]
    s1 = jnp.where(row == 0, c1, pltpu.roll(ug, 1, 0))
    s2 = jnp.where(row == 0, c0, jnp.where(row == 1, c1, pltpu.roll(ug, 2, 0)))
    s1 = jnp.where(pos < 1, p1_ref[...], s1)
    s2 = jnp.where(pos < 2, p2_ref[...], s2)
    carry_scr[j, 0:2, :] = ug[tm - 2:tm, :]
    conv = cb_ref[...] + s2 * cw_ref[0:1, :] + s1 * cw_ref[1:2, :] + ug * cw_ref[2:3, :]
    f_ref[...] = (_gelu_tanh(conv) * uv).astype(BF16)
    if tail_ref.shape[0] == tm:
        tail_ref[...] = ug
    else:
        tail_ref[...] = ug[tm - SUBLANES:tm, :]


def _up(x1, rows, norm_g, wup_bf, conv_w, conv_b, p1, p2, tn, full_tail):
    n, d = x1.shape
    dff = conv_w.shape[1]
    tm = rows.tm
    ncol = dff // tn
    sub = _pick(tm, 128, SUBLANES)
    pr = p1.shape[0]
    tail_rows = tm if full_tail else SUBLANES
    prev_spec = (pl.BlockSpec((tm, tn), lambda i, j: (i, j)) if pr != 1
                 else pl.BlockSpec((1, tn), lambda i, j: (0, j)))
    kern = functools.partial(_up_kernel, sub=sub, seq_len=rows.seq_len)
    return pl.pallas_call(
        kern,
        grid=(n // tm, ncol),
        in_specs=[pl.BlockSpec((tm, d), lambda i, j: (i, 0), pipeline_mode=pl.Buffered(1)),
                  rows.mod_spec(d, lambda i, j: 0),
                  pl.BlockSpec((1, d), lambda i, j: (0, 0)),
                  pl.BlockSpec((d, tn), lambda i, j: (0, j)),
                  pl.BlockSpec((d, tn), lambda i, j: (0, ncol + j)),
                  pl.BlockSpec((CONV_W, tn), lambda i, j: (0, j)),
                  pl.BlockSpec((1, tn), lambda i, j: (0, j)),
                  prev_spec, prev_spec],
        out_specs=[pl.BlockSpec((tm, tn), lambda i, j: (i, j)),
                   pl.BlockSpec((tail_rows, tn), lambda i, j: (i, j))],
        out_shape=[jax.ShapeDtypeStruct((n, dff), BF16),
                   jax.ShapeDtypeStruct((n // tm * tail_rows, dff), F32)],
        scratch_shapes=[pltpu.VMEM((tm, d), BF16),
                        pltpu.VMEM((ncol, SUBLANES, tn), F32)],
        compiler_params=_params(("arbitrary", "arbitrary")),
        name="up_proj",
    )(x1, rows.mod, norm_g.reshape(1, d), wup_bf, wup_bf, conv_w, conv_b.reshape(1, dff), p1, p2)


def _down_kernel(f_ref, w_ref, x_ref, mod_ref, g_ref, o_ref, acc_scr, *, n_k):
    k = pl.program_id(1)

    @pl.when(k == 0)
    def _():
        acc_scr[...] = jnp.zeros_like(acc_scr)

    acc_scr[...] += jnp.dot(f_ref[...], w_ref[...], preferred_element_type=F32)

    @pl.when(k == n_k - 1)
    def _():
        x2 = x_ref[...] + _mod_rows(mod_ref, 5, 0, x_ref.shape[0]) * acc_scr[...]
        ms = jnp.mean(x2 * x2, axis=-1, keepdims=True)
        o_ref[...] = (x2 * lax.rsqrt(ms + RMS_EPS)) * g_ref[...]


def _down(f, wd_bf, x1, rows, normf_g, tk):
    n, dff = f.shape
    d = x1.shape[1]
    tm = rows.tm
    n_k = dff // tk
    return pl.pallas_call(
        functools.partial(_down_kernel, n_k=n_k),
        grid=(n // tm, n_k),
        in_specs=[pl.BlockSpec((tm, tk), lambda i, k: (i, k)),
                  pl.BlockSpec((tk, d), lambda i, k: (k, 0)),
                  pl.BlockSpec((tm, d), lambda i, k: (i, 0)),
                  rows.mod_spec(d, lambda i, k: 0),
                  pl.BlockSpec((1, d), lambda i, k: (0, 0))],
        out_specs=pl.BlockSpec((tm, d), lambda i, k: (i, 0)),
        out_shape=jax.ShapeDtypeStruct((n, d), F32),
        scratch_shapes=[pltpu.VMEM((tm, d), F32)],
        compiler_params=_params(("arbitrary", "arbitrary")),
        name="down_proj",
    )(f, wd_bf, x1, rows.mod, normf_g.reshape(1, d))


def _rope_tables(pos):
    half = HEAD_B // 2
    inv = ROPE_THETA ** (-jnp.arange(half, dtype=F32) / half)
    ang = pos.astype(F32)[:, None] * inv[None, :]
    cos, sin = jnp.cos(ang), jnp.sin(ang)
    return jnp.concatenate([cos, cos], axis=1), jnp.concatenate([-sin, sin], axis=1)


def _group(x, mod, pos, wts, shift0, wkv0, conv_prev, cache, tm_target):
    bsz, t_len, d = x.shape
    n = bsz * t_len
    lay = wts["layout"]
    tn, za, zap, da, db, dff = lay["tn"], lay["za"], lay["zap"], lay["da"], lay["db"], lay["dff"]
    x2 = x.reshape(n, d)
    per_row = cache is not None
    if per_row:
        tm = tm_big = n
        mod4 = jnp.repeat(mod.reshape(bsz, 6, d), t_len, axis=0).swapaxes(0, 1)[None]
        cos_t, sin_t = _rope_tables(jnp.tile(pos, bsz))
    else:
        tm = _pick(t_len, tm_target, SUBLANES)
        tm_big = _pick(t_len, 2 * tm_target, SUBLANES)
        mod4 = mod.reshape(bsz, 6, 1, d)
        cos_t, sin_t = _rope_tables(pos)
    rows = _Rows(n, t_len, tm, mod4)
    rows_big = _Rows(n, t_len, tm_big, mod4)

    q_off = zap
    z, k4, v4 = _in_proj(x2, rows_big, wts["norm1_g"], wts["w_in"], cos_t, sin_t, tn, q_off // tn, db // tn)
    zp = z.shape[1]
    z3 = z.reshape(bsz, t_len, zp)

    c_len = _pick(t_len, 64, SUBLANES) if t_len >= SUBLANES else SUBLANES
    t_pad = _round_up(t_len, c_len)
    z3a = z3 if t_pad == t_len else jnp.pad(z3[:, :, :zap], ((0, 0), (0, t_pad - t_len), (0, 0)))
    shift_p = jnp.pad(shift0, ((0, 0), (0, zap - za)))[:, None, :]
    yag, wkv = _rwkv(z3a, zap, shift_p, wkv0, wts["mu"], wts["vecs"],
                     wts["w2p"], wts["a2p"], wts["g2p"], c_len, min(t_len, c_len), da)
    yag = yag[:, :t_len].reshape(n, da)
    new_shift = z3[:, t_len - 1, :za]

    n_heads = db // HEAD_B
    k_out = k4.reshape(bsz, t_len, n_heads, HEAD_B)
    v_out = v4.reshape(bsz, t_len, n_heads, HEAD_B)
    if cache is None:
        attn = _attn_prompt(z3, q_off // HEAD_B, (q_off + db) // HEAD_B, (q_off + 2 * db) // HEAD_B, n_heads)
    else:
        cache_k, cache_v, page_table = cache
        q4 = z3[:, :, q_off:q_off + db].reshape(bsz, t_len, n_heads, HEAD_B)
        attn = _attn_sample(q4, k_out, v_out, cache_k, cache_v, page_table)
    attn = attn.reshape(n, db)

    mix = _mix(yag, attn, wts["w_proj_a"], wts["w_proj_b"], z, (q_off + 3 * db) // tn,
               (q_off + 3 * db + d) // tn, tm_big, tn)
    x1 = _out_proj(x2, mix, wts["w_out"], rows_big, _pick(d, 1024, tn))

    if per_row:
        zeros = jnp.zeros((bsz, t_len - 1, dff), F32)
        p1 = jnp.concatenate([conv_prev[:, 1:2], zeros], axis=1).reshape(n, dff)
        p2 = jnp.concatenate([conv_prev, zeros[:, :t_len - 2]], axis=1).reshape(n, dff)
    else:
        p1 = p2 = jnp.zeros((1, dff), F32)
    f, tail = _up(x1, rows_big, wts["norm2_g"], wts["w_up"], wts["conv_w"], wts["conv_b"], p1, p2, tn, per_row)
    if per_row:
        new_conv = tail.reshape(bsz, t_len, dff)[:, t_len - (CONV_W - 1):]
    else:
        tps = t_len // tm_big
        new_conv = tail.reshape(bsz, tps, SUBLANES, dff)[:, -1, SUBLANES - (CONV_W - 1):]
    y = _down(f, wts["w_down"], x1, rows, wts["normf_g"], _pick(dff, 1408, LANES)).reshape(bsz, t_len, d)
    return y, k_out, v_out, wkv, new_shift, new_conv


def kernel(x_prompt, x_sample, cache_k, cache_v, state_wkv, state_shift, state_conv, page_table, c_prompt, c_sample, w_ada, b_ada, norm1_g, w_in, mu_shift, rwkv_w0, rwkv_w2, rwkv_a0, rwkv_a2, rwkv_g2, rwkv_kk, rwkv_ka, rwkv_rk, lnx_g, lnx_b, w_proj_a, w_proj_b, w_out, norm2_g, w_up, conv_w, conv_b, w_down, normf_g):
    d = x_prompt.shape[-1]
    da, db = w_proj_a.shape[0], w_proj_b.shape[0]
    za = mu_shift.shape[0]
    dff = conv_w.shape[1]
    lora_g = rwkv_g2.shape[0]
    assert rwkv_w2.shape[0] == LORA_W and rwkv_a2.shape[0] == LORA_A and LORA_W + LORA_A == LANES
    assert za == 3 * da + LORA_W + LORA_A + lora_g and da % LANES == 0 and db % HEAD_B == 0
    tn = 512 if all(v % 512 == 0 for v in (db, d, dff)) else LANES
    zap = _round_up(za, tn)
    gw = zap - 3 * da - LANES
    assert gw >= lora_g

    zeros_l = jnp.zeros((LORA_W, da), F32)
    vecs = jnp.stack([rwkv_w0, rwkv_a0, rwkv_kk, rwkv_ka, rwkv_rk.reshape(da), lnx_g, lnx_b,
                      jnp.zeros((da,), F32)])
    wts = {
        "layout": dict(tn=tn, za=za, zap=zap, da=da, db=db, dff=dff),
        "norm1_g": norm1_g, "norm2_g": norm2_g, "normf_g": normf_g,
        "w_in": _w_in_prep(w_in.T, za, zap, tn),
        "mu": jnp.pad(mu_shift, (0, zap - za)).reshape(1, zap),
        "vecs": vecs,
        "w2p": jnp.concatenate([rwkv_w2, zeros_l], axis=0).astype(BF16),
        "a2p": jnp.concatenate([zeros_l, rwkv_a2], axis=0).astype(BF16),
        "g2p": jnp.pad(rwkv_g2, ((0, gw - lora_g), (0, 0))).astype(BF16),
        "w_proj_a": w_proj_a.astype(BF16), "w_proj_b": w_proj_b.astype(BF16),
        "w_out": w_out.astype(BF16), "w_up": w_up, "w_down": w_down.astype(BF16),
        "conv_w": conv_w, "conv_b": conv_b,
    }

    n_p, t_p = x_prompt.shape[:2]
    n_s, t_s = x_sample.shape[:2]
    n_c = _round_up(n_p + n_s, SUBLANES)
    c_all = jnp.concatenate([c_prompt, c_sample, jnp.zeros((n_c - n_p - n_s, d), F32)], axis=0)
    mod = _ada(c_all, w_ada, b_ada)

    n_heads_a = da // HEAD_A
    out_p = _group(x_prompt, mod[:n_p], jnp.arange(t_p, dtype=jnp.int32), wts,
                   jnp.zeros((n_p, za), F32), jnp.zeros((n_p, n_heads_a, HEAD_A, HEAD_A), F32),
                   None, None, 512)
    past_len = page_table.shape[1] * PAGE_SIZE
    out_s = _group(x_sample, mod[n_p:n_p + n_s], past_len + jnp.arange(t_s, dtype=jnp.int32), wts,
                   state_shift, state_wkv, state_conv, (cache_k, cache_v, page_table), 512)
    y_p, k_p, v_p, wkv_p, sh_p, cv_p = out_p
    y_s, k_s, v_s, wkv_s, sh_s, cv_s = out_s
    return (y_p, y_s, k_p, v_p, wkv_p, sh_p, cv_p, k_s, v_s, wkv_s, sh_s, cv_s)
```
